```python
import math
import jax, jax.numpy as jnp
from jax import lax
import numpy as np

D_MODEL = 1024
BATCH = 8
SEQ = 8192
DEPTH = 1

CHUNK = 64
EPS = 1e-6
MIX_WIDTH = D_MODEL
POOL_WIDTH = MIX_WIDTH // 2
POOL_WINDOWS = (2, 4, 8, 16)
N_POOL_GROUPS = len(POOL_WINDOWS)
POOL_GROUP = POOL_WIDTH // N_POOL_GROUPS
ATTN_WIDTH = MIX_WIDTH - POOL_WIDTH
HEAD_DIM = 64
N_HEADS = ATTN_WIDTH // HEAD_DIM
IDX_HEADS = 8
IDX_DIM = 64
TOPK_MAX = 256
Q_BLOCK = 128
PEER_HEADS = 8
PEER_KEYS = 128
PEER_EXPERTS = PEER_KEYS * PEER_KEYS
PEER_QDIM = 256
PEER_TOPK = 16
PEER_BLOCK = 128
NEG = -1e30

IN_SIZES = (POOL_WIDTH, ATTN_WIDTH, ATTN_WIDTH, ATTN_WIDTH, IDX_HEADS * IDX_DIM, IDX_DIM, IDX_HEADS)
IN_WIDTH = sum(IN_SIZES)
IN_SPLITS = tuple(int(s) for s in np.cumsum(IN_SIZES)[:-1])

kernel_name = "hymba_pool_dsa_peer_block"


def rms_norm(x, g):
    xf = x.astype(jnp.float32)
    y = xf * lax.rsqrt(jnp.mean(xf * xf, axis=-1, keepdims=True) + EPS)
    return (y * g.astype(jnp.float32)).astype(x.dtype)


def alibi_slopes(n_heads):
    h = jnp.arange(1, n_heads + 1, dtype=jnp.float32)
    return jnp.exp2(-8.0 * h / n_heads)


def pool_mixer(u, pool_w, pool_scale, pool_out_g):
    B, S, _ = u.shape
    uf = u.astype(jnp.float32)
    cs = jnp.cumsum(uf, axis=1)
    t = jnp.arange(S)
    parts = []
    for gi, win in enumerate(POOL_WINDOWS):
        lo, hi = gi * POOL_GROUP, (gi + 1) * POOL_GROUP
        cg = cs[..., lo:hi]
        prev = jnp.pad(cg, ((0, 0), (win, 0), (0, 0)))[:, :S]
        cnt = jnp.minimum(t + 1, win).astype(jnp.float32)[None, :, None]
        parts.append((cg - prev) / cnt - uf[..., lo:hi])
    pooled = jnp.stack(parts, axis=2).astype(u.dtype)
    mixed = jnp.einsum('bsgc,gcd->bsgd', pooled, pool_w).reshape(B, S, POOL_WIDTH)
    return rms_norm(mixed * pool_scale, pool_out_g)


def sparse_attention(q, k, v, qi, ki, wi, slopes):
    B, S = q.shape[0], q.shape[1]
    topk = min(TOPK_MAX, S // 4)
    s_chunk = jnp.arange(S) // CHUNK
    scale = HEAD_DIM ** -0.5

    def block(j):
        start = j * Q_BLOCK
        qb = lax.dynamic_slice_in_dim(q, start, Q_BLOCK, axis=1)
        qib = lax.dynamic_slice_in_dim(qi, start, Q_BLOCK, axis=1)
        wib = lax.dynamic_slice_in_dim(wi, start, Q_BLOCK, axis=1)
        t = start + jnp.arange(Q_BLOCK)
        t_chunk = t // CHUNK
        dots = jnp.einsum('bqhd,bsd->bqhs', qib, ki).astype(jnp.float32) * (IDX_DIM ** -0.5)
        score = jnp.einsum('bqh,bqhs->bqs', wib.astype(jnp.float32), jax.nn.relu(dots))
        admissible = s_chunk[None, :] <= t_chunk[:, None]
        score = jnp.where(admissible[None], score, NEG)
        _, sel = lax.top_k(score, topk)
        valid = (sel // CHUNK) <= t_chunk[None, :, None]
        kg = jax.vmap(lambda kb, ib: kb[ib])(k, sel)
        vg = jax.vmap(lambda vb, ib: vb[ib])(v, sel)
        logits = jnp.einsum('bqhd,bqkhd->bqhk', qb, kg).astype(jnp.float32) * scale
        dist = jnp.abs(t[None, :, None] - sel).astype(jnp.float32)
        logits = logits - slopes[None, None, :, None] * dist[:, :, None, :]
        logits = jnp.where(valid[:, :, None, :], logits, NEG)
        p = jax.nn.softmax(logits, axis=-1).astype(v.dtype)
        return jnp.einsum('bqhk,bqkhd->bqhd', p, vg)

    outs = lax.map(block, jnp.arange(S // Q_BLOCK))
    return jnp.swapaxes(outs, 0, 1).reshape(B, S, N_HEADS, HEAD_DIM)


def peer_ffn(h, wq, sub_keys, u_tab, v_tab):
    B, S, D = h.shape
    hb = h.reshape((B * S) // PEER_BLOCK, PEER_BLOCK, D)
    half = PEER_QDIM // 2

    def block(xb):
        q = (xb @ wq).reshape(PEER_BLOCK, PEER_HEADS, 2, half)
        s = jnp.einsum('thpd,hpnd->thpn', q, sub_keys).astype(jnp.float32)
        s1, i1 = lax.top_k(s[:, :, 0], PEER_TOPK)
        s2, i2 = lax.top_k(s[:, :, 1], PEER_TOPK)
        cand = (s1[..., :, None] + s2[..., None, :]).reshape(PEER_BLOCK, PEER_HEADS, PEER_TOPK * PEER_TOPK)
        cand_id = (i1[..., :, None] * PEER_KEYS + i2[..., None, :]).reshape(PEER_BLOCK, PEER_HEADS, PEER_TOPK * PEER_TOPK)
        top_s, pos = lax.top_k(cand, PEER_TOPK)
        eid = jnp.take_along_axis(cand_id, pos, axis=-1)
        g = jax.nn.softmax(top_s, axis=-1)
        ug = u_tab[eid]
        a = jax.nn.gelu(jnp.einsum('td,thkd->thk', xb, ug).astype(jnp.float32), approximate=False)
        vg = v_tab[eid]
        return jnp.einsum('thk,thkd->td', (g * a).astype(xb.dtype), vg)

    return lax.map(block, hb).reshape(B, S, D)


def setup_inputs(seed: int = 0) -> dict:
    key = jax.random.key(seed)
    ks = jax.random.split(key, 18)
    f32 = jnp.float32
    L = DEPTH
    D = D_MODEL

    def nrm(k, shape, std):
        return std * jax.random.normal(k, shape, f32)

    def gain(k, shape):
        return 1.0 + 0.1 * jax.random.normal(k, shape, f32)

    return {
        "x": jax.random.normal(ks[0], (BATCH, SEQ, D), f32),
        "c": jax.random.normal(ks[1], (BATCH, D), f32),
        "ada_w": nrm(ks[2], (L, D, 6 * D), 0.5 * D ** -0.5),
        "ada_b": nrm(ks[3], (L, 6 * D), 0.02),
        "norm1_g": gain(ks[4], (L, D)),
        "w_in": nrm(ks[5], (L, D, IN_WIDTH), D ** -0.5),
        "pool_w": nrm(ks[6], (L, N_POOL_GROUPS, POOL_GROUP, POOL_GROUP), POOL_GROUP ** -0.5),
        "pool_scale": gain(ks[7], (L, POOL_WIDTH)),
        "pool_out_g": gain(ks[8], (L, POOL_WIDTH)),
        "q_norm_g": gain(ks[9], (L, HEAD_DIM)),
        "k_norm_g": gain(ks[10], (L, HEAD_DIM)),
        "attn_out_g": gain(ks[11], (L, N_HEADS, HEAD_DIM)),
        "w_out": nrm(ks[12], (L, MIX_WIDTH, D), MIX_WIDTH ** -0.5),
        "norm2_g": gain(ks[13], (L, D)),
        "peer_wq": nrm(ks[14], (L, D, PEER_HEADS * PEER_QDIM), D ** -0.5),
        "peer_subkeys": nrm(ks[15], (L, PEER_HEADS, 2, PEER_KEYS, PEER_QDIM // 2), (PEER_QDIM // 2) ** -0.5),
        "peer_u": nrm(ks[16], (L, PEER_EXPERTS, D), D ** -0.5),
        "peer_v": nrm(ks[17], (L, PEER_EXPERTS, D), PEER_HEADS ** -0.5),
    }


def reference(x, c, ada_w, ada_b, norm1_g, w_in, pool_w, pool_scale, pool_out_g,
              q_norm_g, k_norm_g, attn_out_g, w_out, norm2_g, peer_wq, peer_subkeys,
              peer_u, peer_v):
    B, S, D = x.shape
    slopes = alibi_slopes(N_HEADS)
    cond = jax.nn.silu(c)
    for l in range(DEPTH):
        mod = (cond @ ada_w[l] + ada_b[l])[:, None, :]
        sh1, sc1, g1, sh2, sc2, g2 = jnp.split(mod, 6, axis=-1)

        h = rms_norm(x, norm1_g[l]) * (1 + sc1) + sh1
        proj = h @ w_in[l]
        u, q, k, v, qi, ki, wi = jnp.split(proj, IN_SPLITS, axis=-1)
        pool_out = pool_mixer(u, pool_w[l], pool_scale[l], pool_out_g[l])
        q = rms_norm(q.reshape(B, S, N_HEADS, HEAD_DIM), q_norm_g[l])
        k = rms_norm(k.reshape(B, S, N_HEADS, HEAD_DIM), k_norm_g[l])
        v = v.reshape(B, S, N_HEADS, HEAD_DIM)
        qi = qi.reshape(B, S, IDX_HEADS, IDX_DIM)
        wi = wi * (IDX_HEADS ** -0.5)
        attn = sparse_attention(q, k, v, qi, ki, wi, slopes)
        attn_out = rms_norm(attn, attn_out_g[l]).reshape(B, S, ATTN_WIDTH)
        mixed = jnp.concatenate([pool_out, attn_out], axis=-1) @ w_out[l]
        x = x + g1 * mixed

        h2 = rms_norm(x, norm2_g[l]) * (1 + sc2) + sh2
        x = x + g2 * peer_ffn(h2, peer_wq[l], peer_subkeys[l], peer_u[l], peer_v[l])
    return x
```

```python
import functools
import math

import jax
import jax.numpy as jnp
from jax import lax
from jax.experimental import pallas as pl
from jax.experimental.pallas import tpu as pltpu

F32 = jnp.float32
BF16 = jnp.bfloat16
I32 = jnp.int32

CHUNK = 64
EPS = 1e-6
POOL_WINDOWS = (2, 4, 8, 16)
POOL_GROUP = 128
POOL_WIDTH = 512
ATTN_WIDTH = 512
HEAD_DIM = 64
N_HEADS = 8
IDX_HEADS = 8
IDX_DIM = 64
TOPK_MAX = 256
PEER_HEADS = 8
PEER_KEYS = 128
PEER_TOPK = 16
NEG = -1e30

LANES = 128
POOL_HALO = 16

NT_DIMS = (((1,), (1,)), ((), ()))


def _dot(a, b):
    return jnp.dot(a, b, preferred_element_type=F32)


def _dot_nt(a, b):
    return lax.dot_general(a, b, NT_DIMS, preferred_element_type=F32)


def _ada_kernel(c_ref, w_ref, b_ref, o_ref):
    c = c_ref[...]
    cond = c * jax.nn.sigmoid(c)
    o_ref[...] = jnp.dot(cond, w_ref[...], preferred_element_type=F32,
                         precision=lax.Precision.HIGHEST) + b_ref[...]


def _ada(c, w, b):
    bsz, d = c.shape
    n = w.shape[1]
    tn = 1024
    return pl.pallas_call(
        _ada_kernel,
        grid=(n // tn,),
        in_specs=[pl.BlockSpec((bsz, d), lambda j: (0, 0)),
                  pl.BlockSpec((d, tn), lambda j: (0, j)),
                  pl.BlockSpec((1, tn), lambda j: (0, j))],
        out_specs=pl.BlockSpec((bsz, tn), lambda j: (0, j)),
        out_shape=jax.ShapeDtypeStruct((bsz, n), F32),
        name="ada",
    )(c, w, b.reshape(1, n))


def _group_sumsq(t, ones_bd):
    sq = t * t
    hi = sq.astype(BF16)
    lo = (sq - hi.astype(F32)).astype(BF16)
    return _dot(hi, ones_bd) + _dot(lo, ones_bd)


def _inproj_kernel(x_ref, sh_ref, sc_ref, g_ref, wnat_ref, wki_ref, wvt_ref, wwt_ref, ones_ref,
                   qg_ref, kg_ref, poolw_ref, pscale_ref, pg_ref,
                   q_ref, k_ref, qi_ref, kid_ref, vt_ref, wt_ref, pool_ref,
                   ubuf, *, tm):
    i = pl.program_id(1)
    x = x_ref[0]
    ms = jnp.mean(x * x, axis=-1, keepdims=True)
    h = (x * lax.rsqrt(ms + EPS) * g_ref[...]) * (1.0 + sc_ref[0]) + sh_ref[0]
    hb = h.astype(BF16)

    proj = _dot(hb, wnat_ref[...])
    u = proj[:, 0:512]
    q = proj[:, 512:1024]
    k = proj[:, 1024:1536]
    qi = proj[:, 1536:2048]

    ones_bd = ones_ref[...]
    qn = q * lax.rsqrt(_group_sumsq(q, ones_bd) * (1.0 / HEAD_DIM) + EPS) * qg_ref[...]
    kn = k * lax.rsqrt(_group_sumsq(k, ones_bd) * (1.0 / HEAD_DIM) + EPS) * kg_ref[...]
    q_ref[0] = qn.astype(BF16)
    k_ref[0] = kn.astype(BF16)
    qi_ref[0] = (qi * (IDX_DIM ** -0.5)).astype(BF16)
    kid_ref[0] = _dot(hb, wki_ref[...]).astype(BF16)
    vt_ref[0, 0] = _dot_nt(wvt_ref[...], hb).astype(BF16)
    wt = _dot_nt(wwt_ref[...], hb)
    wt_ref[0] = wt[0:IDX_HEADS, :] * (IDX_HEADS ** -0.5)

    @pl.when(i == 0)
    def _():
        ubuf[0:POOL_HALO, :] = jnp.zeros((POOL_HALO, POOL_WIDTH), F32)

    ubuf[POOL_HALO:POOL_HALO + tm, :] = u
    tpos = i * tm + lax.broadcasted_iota(I32, (tm, POOL_GROUP), 0)
    parts = []
    for gi, win in enumerate(POOL_WINDOWS):
        lo_l, hi_l = gi * POOL_GROUP, (gi + 1) * POOL_GROUP
        ws = ubuf[POOL_HALO:POOL_HALO + tm, lo_l:hi_l]
        for j in range(1, win):
            ws = ws + ubuf[POOL_HALO - j:POOL_HALO - j + tm, lo_l:hi_l]
        cnt = jnp.minimum(tpos + 1, win).astype(F32)
        pooled = ws / cnt - u[:, lo_l:hi_l]
        parts.append(_dot(pooled.astype(BF16), poolw_ref[gi]))
    mixed = jnp.concatenate(parts, axis=-1) * pscale_ref[...]
    pms = jnp.mean(mixed * mixed, axis=-1, keepdims=True)
    pool_ref[0] = (mixed * lax.rsqrt(pms + EPS) * pg_ref[...]).astype(BF16)
    ubuf[0:POOL_HALO, :] = ubuf[tm:tm + POOL_HALO, :]


def _inproj(x, sh1, sc1, norm_g, w_in, pool_w, pool_scale, pool_out_g, q_norm_g, k_norm_g, tm):
    b, s, d = x.shape
    nt = s // tm
    w_u, w_q, w_k, w_v, w_qi, w_ki, w_wi = jnp.split(
        w_in, (512, 1024, 1536, 2048, 2560, 2624), axis=-1)
    wnat = jnp.concatenate([w_u, w_q, w_k, w_qi], axis=-1).astype(BF16)
    wki = jnp.concatenate([w_ki, w_ki], axis=-1).astype(BF16)
    wvt = w_v.T.astype(BF16)
    wwt = jnp.concatenate([w_wi.T, jnp.zeros((16 - IDX_HEADS, d), F32)], axis=0).astype(BF16)
    gid = jnp.arange(ATTN_WIDTH) // HEAD_DIM
    ones_bd = (gid[:, None] == gid[None, :]).astype(BF16)
    qg = (jnp.tile(q_norm_g, N_HEADS) * (HEAD_DIM ** -0.5)).reshape(1, ATTN_WIDTH)
    kg = jnp.tile(k_norm_g, N_HEADS).reshape(1, ATTN_WIDTH)

    full = lambda shape: pl.BlockSpec(shape, lambda bi, ti: (0,) * len(shape))
    tok = lambda w: pl.BlockSpec((1, tm, w), lambda bi, ti: (bi, ti, 0))
    per_b = pl.BlockSpec((1, 1, d), lambda bi, ti: (bi, 0, 0))
    out_shapes = (
        jax.ShapeDtypeStruct((b, s, ATTN_WIDTH), BF16),
        jax.ShapeDtypeStruct((b, s, ATTN_WIDTH), BF16),
        jax.ShapeDtypeStruct((b, s, IDX_HEADS * IDX_DIM), BF16),
        jax.ShapeDtypeStruct((b, s, 2 * IDX_DIM), BF16),
        jax.ShapeDtypeStruct((b, nt, ATTN_WIDTH, tm), BF16),
        jax.ShapeDtypeStruct((b, IDX_HEADS, s), F32),
        jax.ShapeDtypeStruct((b, s, POOL_WIDTH), BF16),
    )
    out_specs = (
        tok(ATTN_WIDTH), tok(ATTN_WIDTH), tok(IDX_HEADS * IDX_DIM), tok(2 * IDX_DIM),
        pl.BlockSpec((1, 1, ATTN_WIDTH, tm), lambda bi, ti: (bi, ti, 0, 0)),
        pl.BlockSpec((1, IDX_HEADS, tm), lambda bi, ti: (bi, 0, ti)),
        tok(POOL_WIDTH),
    )
    return pl.pallas_call(
        functools.partial(_inproj_kernel, tm=tm),
        grid=(b, nt),
        in_specs=[tok(d), per_b, per_b, full((1, d)), full((d, 2048)), full((d, 2 * IDX_DIM)),
                  full((ATTN_WIDTH, d)), full((16, d)), full((ATTN_WIDTH, ATTN_WIDTH)),
                  full((1, ATTN_WIDTH)), full((1, ATTN_WIDTH)),
                  full((len(POOL_WINDOWS), POOL_GROUP, POOL_GROUP)),
                  full((1, POOL_WIDTH)), full((1, POOL_WIDTH))],
        out_specs=out_specs,
        out_shape=out_shapes,
        scratch_shapes=[pltpu.VMEM((tm + POOL_HALO, POOL_WIDTH), F32)],
        compiler_params=pltpu.CompilerParams(
            dimension_semantics=("arbitrary", "arbitrary"), vmem_limit_bytes=48 * 1024 * 1024),
        name="inproj",
    )(x, sh1, sc1, norm_g.reshape(1, d), wnat, wki, wvt, wwt, ones_bd, qg, kg,
      pool_w.astype(BF16), pool_scale.reshape(1, -1), pool_out_g.reshape(1, -1))


def _attn_kernel(q_ref, qi_ref, wt_ref, kid_ref, k_ref, vt_ref, aog_ref, o_ref,
                 key_ref, acc_ref, m_ref, l_ref, qim_ref, qm_ref, cut_ref,
                 *, tq, tk, seq, topk):
    j = pl.program_id(1)
    nkt = (j * tq + tq + tk - 1) // tk
    qpos = j * tq + lax.broadcasted_iota(I32, (1, tq), 1)
    qchunk = qpos // CHUNK

    lane = lax.broadcasted_iota(I32, (tq, LANES), 1)
    for h in range(N_HEADS):
        p = h // 2
        keep = (lane < HEAD_DIM) if h % 2 == 0 else (lane >= HEAD_DIM)
        qi_slab = qi_ref[0, :, p * LANES:(p + 1) * LANES]
        q_slab = q_ref[0, :, p * LANES:(p + 1) * LANES]
        qim_ref[h] = jnp.where(keep, qi_slab, jnp.zeros_like(qi_slab))
        qm_ref[h] = jnp.where(keep, q_slab, jnp.zeros_like(q_slab))

    def key_pos(r0):
        return r0 + lax.broadcasted_iota(I32, (tk, tq), 0)

    def p1(kt, carry):
        r0 = pl.multiple_of(kt * tk, tk)
        kd = kid_ref[0, pl.ds(r0, tk), :]
        acc = jnp.zeros((tk, tq), F32)
        for h in range(IDX_HEADS):
            d = _dot_nt(kd, qim_ref[h])
            acc = acc + wt_ref[0, h:h + 1, :] * jnp.maximum(d, 0.0)
        adm = (key_pos(r0) // CHUNK) <= qchunk
        sc = jnp.where(adm, acc, NEG)
        bits = pltpu.bitcast(sc, I32)
        key_ref[pl.ds(r0, tk), :] = jnp.where(bits < 0, bits ^ 0x7FFFFFFF, bits)
        return carry

    lax.fori_loop(0, nkt, p1, 0)

    def count(pred):
        def body(kt, c8):
            r0 = pl.multiple_of(kt * tk, tk)
            m = pred(key_ref[pl.ds(r0, tk), :], r0).astype(I32)
            return c8 + m.reshape(tk // 8, 8, tq).sum(axis=0)
        c8 = lax.fori_loop(0, nkt, body, jnp.zeros((8, tq), I32))
        return c8.sum(axis=0, keepdims=True)

    c_nonneg = count(lambda kk, r0: kk >= 0)
    lo0 = jnp.where(c_nonneg >= topk, 0, -(2 ** 31)).astype(I32)

    def bit_body(b, lo):
        trial = lo + jnp.left_shift(jnp.int32(1), 30 - b)
        c = count(lambda kk, r0: kk >= trial)
        return jnp.where(c >= topk, trial, lo)

    lo = lax.fori_loop(0, 31, bit_body, lo0)

    c_gt = count(lambda kk, r0: kk > lo)
    c_ge = count(lambda kk, r0: kk >= lo)
    need = topk - c_gt
    cut_ref[...] = jnp.full((1, tq), seq, I32)

    @pl.when(jnp.max(c_ge) > topk)
    def _():
        nbits = max(1, (seq - 1).bit_length())

        def idx_body(b, jv):
            trial = jv + jnp.left_shift(jnp.int32(1), nbits - 1 - b)
            c = count(lambda kk, r0: (kk == lo) & (key_pos(r0) < trial))
            return jnp.where(c < need, trial, jv)

        jv = lax.fori_loop(0, nbits, idx_body, jnp.zeros((1, tq), I32))
        cut_ref[...] = jv + 1

    cut = cut_ref[...]

    m_ref[...] = jnp.full((N_HEADS, tq), NEG, F32)
    l_ref[...] = jnp.zeros((N_HEADS, tq), F32)
    acc_ref[...] = jnp.zeros((ATTN_WIDTH, tq), F32)

    def p3(kt, carry):
        r0 = pl.multiple_of(kt * tk, tk)
        kk = key_ref[pl.ds(r0, tk), :]
        kpos = key_pos(r0)
        adm = (kpos // CHUNK) <= qchunk
        sel = adm & ((kk > lo) | ((kk == lo) & (kpos < cut)))
        dist = jnp.abs(qpos - kpos).astype(F32)
        for h in range(N_HEADS):
            p = h // 2
            slope = 2.0 ** (-8.0 * (h + 1) / N_HEADS)
            ks = k_ref[0, pl.ds(r0, tk), p * LANES:(p + 1) * LANES]
            lg = _dot_nt(ks, qm_ref[h]) - slope * dist
            lg = jnp.where(sel, lg, NEG)
            m_old = m_ref[h:h + 1, :]
            m_new = jnp.maximum(m_old, jnp.max(lg, axis=0, keepdims=True))
            alpha = jnp.exp(m_old - m_new)
            pr = jnp.where(sel, jnp.exp(lg - m_new), 0.0)
            l_ref[h:h + 1, :] = alpha * l_ref[h:h + 1, :] + jnp.sum(pr, axis=0, keepdims=True)
            m_ref[h:h + 1, :] = m_new
            vt = vt_ref[0, kt, h * HEAD_DIM:(h + 1) * HEAD_DIM, :]
            rows = slice(h * HEAD_DIM, (h + 1) * HEAD_DIM)
            acc_ref[rows, :] = alpha * acc_ref[rows, :] + _dot(vt, pr.astype(BF16))
        return carry

    lax.fori_loop(0, nkt, p3, 0)

    outs = []
    for h in range(N_HEADS):
        rows = slice(h * HEAD_DIM, (h + 1) * HEAD_DIM)
        o = acc_ref[rows, :] / l_ref[h:h + 1, :]
        oms = jnp.mean(o * o, axis=0, keepdims=True)
        outs.append(o * lax.rsqrt(oms + EPS))
    ot = jnp.concatenate(outs, axis=0)
    o_ref[0] = (ot.T * aog_ref[...]).astype(BF16)


def _attn(q, k, qi, kid, vt, wt, attn_out_g, tq, tk):
    b, s, _ = q.shape
    topk = min(TOPK_MAX, s // 4)
    assert tk >= topk and s % tk == 0 and tk % tq == 0 and vt.shape[3] == tk
    blk = lambda w: pl.BlockSpec((1, tq, w), lambda bi, ji: (bi, ji, 0))
    whole = lambda w: pl.BlockSpec((1, s, w), lambda bi, ji: (bi, 0, 0))
    return pl.pallas_call(
        functools.partial(_attn_kernel, tq=tq, tk=tk, seq=s, topk=topk),
        grid=(b, s // tq),
        in_specs=[blk(ATTN_WIDTH), blk(IDX_HEADS * IDX_DIM),
                  pl.BlockSpec((1, IDX_HEADS, tq), lambda bi, ji: (bi, 0, ji)),
                  whole(2 * IDX_DIM), whole(ATTN_WIDTH),
                  pl.BlockSpec((1, s // tk, ATTN_WIDTH, tk), lambda bi, ji: (bi, 0, 0, 0)),
                  pl.BlockSpec((1, ATTN_WIDTH), lambda bi, ji: (0, 0))],
        out_specs=blk(ATTN_WIDTH),
        out_shape=jax.ShapeDtypeStruct((b, s, ATTN_WIDTH), BF16),
        scratch_shapes=[pltpu.VMEM((s, tq), I32),
                        pltpu.VMEM((ATTN_WIDTH, tq), F32),
                        pltpu.VMEM((N_HEADS, tq), F32),
                        pltpu.VMEM((N_HEADS, tq), F32),
                        pltpu.VMEM((IDX_HEADS, tq, LANES), BF16),
                        pltpu.VMEM((N_HEADS, tq, LANES), BF16),
                        pltpu.VMEM((1, tq), I32)],
        compiler_params=pltpu.CompilerParams(
            dimension_semantics=("arbitrary", "arbitrary"), vmem_limit_bytes=56 * 1024 * 1024),
        name="attn",
    )(q, qi, wt, kid, k, vt, attn_out_g.reshape(1, ATTN_WIDTH))


def _outproj_kernel(pool_ref, attn_ref, x_ref, wo_ref, g1_ref, sh_ref, sc_ref, ng_ref,
                    x1_ref, h2_ref):
    mixed = _dot(pool_ref[0], wo_ref[0:POOL_WIDTH, :]) + _dot(attn_ref[0], wo_ref[POOL_WIDTH:, :])
    x1 = x_ref[0] + g1_ref[0] * mixed
    x1_ref[0] = x1
    ms = jnp.mean(x1 * x1, axis=-1, keepdims=True)
    h2 = (x1 * lax.rsqrt(ms + EPS) * ng_ref[...]) * (1.0 + sc_ref[0]) + sh_ref[0]
    h2_ref[0] = h2.astype(BF16)


def _outproj(pool_out, attn_out, x, w_out, g1, sh2, sc2, norm_g, tm):
    b, s, d = x.shape
    tok = lambda w: pl.BlockSpec((1, tm, w), lambda bi, ti: (bi, ti, 0))
    per_b = pl.BlockSpec((1, 1, d), lambda bi, ti: (bi, 0, 0))
    return pl.pallas_call(
        _outproj_kernel,
        grid=(b, s // tm),
        in_specs=[tok(POOL_WIDTH), tok(ATTN_WIDTH), tok(d),
                  pl.BlockSpec((POOL_WIDTH + ATTN_WIDTH, d), lambda bi, ti: (0, 0)),
                  per_b, per_b, per_b, pl.BlockSpec((1, d), lambda bi, ti: (0, 0))],
        out_specs=(tok(d), tok(d)),
        out_shape=(jax.ShapeDtypeStruct((b, s, d), F32), jax.ShapeDtypeStruct((b, s, d), BF16)),
        compiler_params=pltpu.CompilerParams(dimension_semantics=("arbitrary", "arbitrary")),
        name="outproj",
    )(pool_out, attn_out, x, w_out.astype(BF16), g1, sh2, sc2, norm_g.reshape(1, d))


def _peer_kernel(h2_ref, x1_ref, g2_ref, wqt_ref, sk_ref, u_ref, vt_ref, o_ref,
                 s1_ref, s2_ref, e1_ref, e2_ref, thr_ref, top_ref, cand_ref, at_ref, p_ref, acc_ref,
                 *, tm, eb):
    e = pl.program_id(1)
    ne = pl.num_programs(1)
    nsub = eb // PEER_KEYS

    @pl.when(e == 0)
    def _prologue():
        h2 = h2_ref[...]
        ridx = lax.broadcasted_iota(I32, (PEER_KEYS, tm), 0)

        def scores(hh, carry):
            qt = _dot_nt(wqt_ref[hh], h2).astype(BF16)
            st = _dot(sk_ref[hh], qt)
            hd = hh // 2

            @pl.when(hh % 2 == 0)
            def _():
                s1_ref[hd] = st

            @pl.when(hh % 2 == 1)
            def _():
                s2_ref[hd] = st

            s = st
            for r in range(PEER_TOPK):
                mx = jnp.max(s, axis=0, keepdims=True)
                top_ref[hh, r:r + 1, :] = mx
                first = jnp.min(jnp.where(s == mx, ridx, PEER_KEYS), axis=0, keepdims=True)
                s = jnp.where(ridx == first, -jnp.inf, s)
            return carry

        lax.fori_loop(0, 2 * PEER_HEADS, scores, 0)

        def gates(hd, carry):
            a1 = top_ref[2 * hd]
            for r2 in range(PEER_TOPK):
                cand_ref[r2 * PEER_TOPK:(r2 + 1) * PEER_TOPK, :] = (
                    a1 + top_ref[2 * hd + 1, r2:r2 + 1, :])
            cand = cand_ref[...]
            c = cand
            thr = jnp.full((1, tm), -jnp.inf, F32)
            remaining = jnp.full((1, tm), PEER_TOPK, I32)
            for r in range(PEER_TOPK):
                mx = jnp.max(c, axis=0, keepdims=True)
                eq = c == mx
                thr = jnp.where(remaining > 0, mx, thr)
                remaining = remaining - jnp.sum(eq.astype(I32), axis=0, keepdims=True)
                c = jnp.where(eq, -jnp.inf, c)
            m1 = top_ref[2 * hd, 0:1, :]
            m2 = top_ref[2 * hd + 1, 0:1, :]
            z = jnp.sum(jnp.where(cand >= thr, jnp.exp(cand - (m1 + m2)), 0.0),
                        axis=0, keepdims=True)
            thr_ref[pl.ds(hd, 1), :] = thr
            e1_ref[hd] = jnp.exp(s1_ref[hd] - m1)
            e2_ref[hd] = jnp.exp(s2_ref[hd] - m2) / z
            return carry

        lax.fori_loop(0, PEER_HEADS, gates, 0)
        acc_ref[...] = jnp.zeros_like(acc_ref)

    at_ref[...] = _dot_nt(u_ref[...], h2_ref[...])

    def sub(ii, carry):
        i1 = e * nsub + ii
        r0 = pl.multiple_of(ii * PEER_KEYS, PEER_KEYS)
        w = jnp.zeros((PEER_KEYS, tm), F32)
        for hd in range(PEER_HEADS):
            s1row = s1_ref[hd, pl.ds(i1, 1), :]
            e1row = e1_ref[hd, pl.ds(i1, 1), :]
            sm = s1row + s2_ref[hd]
            w = w + jnp.where(sm >= thr_ref[hd:hd + 1, :], e2_ref[hd] * e1row, 0.0)
        a = at_ref[pl.ds(r0, PEER_KEYS), :]
        gl = 0.5 * a * (1.0 + lax.erf(a * (1.0 / math.sqrt(2.0))))
        p_ref[pl.ds(r0, PEER_KEYS), :] = (gl * w).astype(BF16)
        return carry

    lax.fori_loop(0, nsub, sub, 0)
    acc_ref[...] += _dot(vt_ref[...], p_ref[...])

    @pl.when(e == ne - 1)
    def _epilogue():
        o_ref[...] = x1_ref[...] + g2_ref[0] * acc_ref[...].T


def _peer(h2, x1, g2, wq, sub_keys, u_tab, v_tab, seq, tm, eb):
    t, d = h2.shape
    ne = u_tab.shape[0] // eb
    half = PEER_KEYS
    wqt = wq.T.reshape(2 * PEER_HEADS, half, d).astype(BF16)
    sk = sub_keys.reshape(2 * PEER_HEADS, PEER_KEYS, half).astype(BF16)
    ub = u_tab.astype(BF16)
    vtb = v_tab.T.astype(BF16)
    hs = (PEER_HEADS, PEER_KEYS, tm)
    return pl.pallas_call(
        functools.partial(_peer_kernel, tm=tm, eb=eb),
        grid=(t // tm, ne),
        in_specs=[pl.BlockSpec((tm, d), lambda i, e: (i, 0)),
                  pl.BlockSpec((tm, d), lambda i, e: (i, 0)),
                  pl.BlockSpec((1, 1, d), lambda i, e: ((i * tm) // seq, 0, 0)),
                  pl.BlockSpec((2 * PEER_HEADS, half, d), lambda i, e: (0, 0, 0)),
                  pl.BlockSpec((2 * PEER_HEADS, PEER_KEYS, half), lambda i, e: (0, 0, 0)),
                  pl.BlockSpec((eb, d), lambda i, e: (e, 0)),
                  pl.BlockSpec((d, eb), lambda i, e: (0, e))],
        out_specs=pl.BlockSpec((tm, d), lambda i, e: (i, 0)),
        out_shape=jax.ShapeDtypeStruct((t, d), F32),
        scratch_shapes=[pltpu.VMEM(hs, F32), pltpu.VMEM(hs, F32), pltpu.VMEM(hs, F32),
                        pltpu.VMEM(hs, F32),
                        pltpu.VMEM((PEER_HEADS, tm), F32),
                        pltpu.VMEM((2 * PEER_HEADS, PEER_TOPK, tm), F32),
                        pltpu.VMEM((PEER_TOPK * PEER_TOPK, tm), F32),
                        pltpu.VMEM((eb, tm), F32),
                        pltpu.VMEM((eb, tm), BF16),
                        pltpu.VMEM((d, tm), F32)],
        compiler_params=pltpu.CompilerParams(
            dimension_semantics=("arbitrary", "arbitrary"), vmem_limit_bytes=56 * 1024 * 1024),
        name="peer",
    )(h2, x1, g2, wqt, sk, ub, vtb)


def _tile(n, pref):
    t = min(pref, n)
    assert n % t == 0
    return t


def kernel(x, c, ada_w, ada_b, norm1_g, w_in, pool_w, pool_scale, pool_out_g, q_norm_g, k_norm_g,
           attn_out_g, w_out, norm2_g, peer_wq, peer_subkeys, peer_u, peer_v):
    b, s, d = x.shape
    depth = ada_w.shape[0]
    tm = _tile(s, 512)
    tq = _tile(s, 128)
    tp = _tile(s, 512)
    eb = 1024
    for l in range(depth):
        mod = _ada(c, ada_w[l], ada_b[l]).reshape(b, 1, 6 * d)
        sh1, sc1, g1, sh2, sc2, g2 = jnp.split(mod, 6, axis=-1)
        q, k, qi, kid, vt, wt, pool_out = _inproj(
            x, sh1, sc1, norm1_g[l], w_in[l], pool_w[l], pool_scale[l], pool_out_g[l],
            q_norm_g[l], k_norm_g[l], tm)
        attn_out = _attn(q, k, qi, kid, vt, wt, attn_out_g[l], tq, tm)
        x1, h2 = _outproj(pool_out, attn_out, x, w_out[l], g1, sh2, sc2, norm2_g[l], tm)
        x = _peer(h2.reshape(b * s, d), x1.reshape(b * s, d), g2, peer_wq[l], peer_subkeys[l],
                  peer_u[l], peer_v[l], s, tp, eb).reshape(b, s, d)
    return x
```

```python
import functools
import math

import ml_dtypes
import numpy as np

import jax
import jax.numpy as jnp
from jax import lax
from jax.experimental import pallas as pl
from jax.experimental.pallas import tpu as pltpu

F32 = jnp.float32
BF16 = jnp.bfloat16
I32 = jnp.int32

CHUNK = 64
EPS = 1e-6
POOL_WINDOWS = (2, 4, 8, 16)
POOL_GROUP = 128
POOL_WIDTH = 512
ATTN_WIDTH = 512
HEAD_DIM = 64
N_HEADS = 8
IDX_HEADS = 8
IDX_DIM = 64
TOPK_MAX = 256
PEER_HEADS = 8
PEER_KEYS = 128
PEER_TOPK = 16
NEG = -1e30

LANES = 128
POOL_HALO = 16
LOG2E = math.log2(math.e)
M_INIT = 0.5 * NEG
N_FEAT = 9

NT_DIMS = (((1,), (1,)), ((), ()))


def _dot(a, b):
    return jnp.dot(a, b, preferred_element_type=F32)


def _dot_nt(a, b):
    return lax.dot_general(a, b, NT_DIMS, preferred_element_type=F32)


def _f32_key(v):
    b = int(np.float32(v).view(np.int32))
    return b ^ 0x7FFFFFFF if b < 0 else b


def _bf16_split3(c):
    c = np.float32(c)
    p1 = np.float32(c.astype(ml_dtypes.bfloat16))
    r = np.float32(c - p1)
    p2 = np.float32(r.astype(ml_dtypes.bfloat16))
    p3 = np.float32(np.float32(r - p2).astype(ml_dtypes.bfloat16))
    return float(p1), float(p2), float(p3)


_SLOPE_L2E = tuple(float(np.float32(2.0 ** (-8.0 * (h + 1) / N_HEADS) * LOG2E)) for h in range(N_HEADS))


def _ada_kernel(c_ref, w_ref, b_ref, o_ref):
    c = c_ref[...]
    cond = c * jax.nn.sigmoid(c)
    o_ref[...] = jnp.dot(cond, w_ref[...], preferred_element_type=F32,
                         precision=lax.Precision.HIGHEST) + b_ref[...]


def _ada(c, w, b):
    bsz, d = c.shape
    n = w.shape[1]
    tn = 1024
    return pl.pallas_call(
        _ada_kernel,
        grid=(n // tn,),
        in_specs=[pl.BlockSpec((bsz, d), lambda j: (0, 0)),
                  pl.BlockSpec((d, tn), lambda j: (0, j)),
                  pl.BlockSpec((1, tn), lambda j: (0, j))],
        out_specs=pl.BlockSpec((bsz, tn), lambda j: (0, j)),
        out_shape=jax.ShapeDtypeStruct((bsz, n), F32),
        name="ada",
    )(c, w, b.reshape(1, n))


def _group_sumsq(t, ones_bd):
    sq = t * t
    hi = sq.astype(BF16)
    lo = (sq - hi.astype(F32)).astype(BF16)
    return _dot(hi, ones_bd) + _dot(lo, ones_bd)


def _inproj_kernel(x_ref, sh_ref, sc_ref, g_ref, wnat_ref, wki_ref, wvt_ref, wwt_ref, ones_ref,
                   qg_ref, kg_ref, poolw_ref, pscale_ref, pg_ref,
                   q_ref, k_ref, qi_ref, kid_ref, vt_ref, wt_ref, pool_ref,
                   ubuf, *, tm, kc):
    i = pl.program_id(1)
    x = x_ref[0]
    ms = jnp.mean(x * x, axis=-1, keepdims=True)
    h = (x * lax.rsqrt(ms + EPS) * g_ref[...]) * (1.0 + sc_ref[0]) + sh_ref[0]
    hb = h.astype(BF16)

    proj = _dot(hb, wnat_ref[...])
    u = proj[:, 0:512]
    q = proj[:, 512:1024]
    k = proj[:, 1024:1536]
    qi = proj[:, 1536:2048]

    ones_bd = ones_ref[...]
    qn = q * lax.rsqrt(_group_sumsq(q, ones_bd) * (1.0 / HEAD_DIM) + EPS) * qg_ref[...]
    kn = k * lax.rsqrt(_group_sumsq(k, ones_bd) * (1.0 / HEAD_DIM) + EPS) * kg_ref[...]
    q_ref[0] = qn.astype(BF16)
    k_ref[0] = kn.astype(BF16)
    qi_ref[0] = (qi * (IDX_DIM ** -0.5)).astype(BF16)
    kid_ref[0] = _dot(hb, wki_ref[...]).astype(BF16)
    vt = _dot_nt(wvt_ref[...], hb).astype(BF16)
    for ci in range(tm // kc):
        vt_ref[0, ci] = vt[:, ci * kc:(ci + 1) * kc]
    wt = _dot_nt(wwt_ref[...], hb)
    wt_ref[0] = wt[0:IDX_HEADS, :] * (IDX_HEADS ** -0.5)

    @pl.when(i == 0)
    def _():
        ubuf[0:POOL_HALO, :] = jnp.zeros((POOL_HALO, POOL_WIDTH), F32)

    ubuf[POOL_HALO:POOL_HALO + tm, :] = u
    tpos = i * tm + lax.broadcasted_iota(I32, (tm, POOL_GROUP), 0)
    parts = []
    for gi, win in enumerate(POOL_WINDOWS):
        lo_l, hi_l = gi * POOL_GROUP, (gi + 1) * POOL_GROUP
        ws = ubuf[POOL_HALO:POOL_HALO + tm, lo_l:hi_l]
        for j in range(1, win):
            ws = ws + ubuf[POOL_HALO - j:POOL_HALO - j + tm, lo_l:hi_l]
        cnt = jnp.minimum(tpos + 1, win).astype(F32)
        pooled = ws / cnt - u[:, lo_l:hi_l]
        parts.append(_dot(pooled.astype(BF16), poolw_ref[gi]))
    mixed = jnp.concatenate(parts, axis=-1) * pscale_ref[...]
    pms = jnp.mean(mixed * mixed, axis=-1, keepdims=True)
    pool_ref[0] = (mixed * lax.rsqrt(pms + EPS) * pg_ref[...]).astype(BF16)
    ubuf[0:POOL_HALO, :] = ubuf[tm:tm + POOL_HALO, :]


def _inproj(x, sh1, sc1, norm_g, w_in, pool_w, pool_scale, pool_out_g, q_norm_g, k_norm_g, tm, kc):
    b, s, d = x.shape
    nt = s // tm
    w_u, w_q, w_k, w_v, w_qi, w_ki, w_wi = jnp.split(
        w_in, (512, 1024, 1536, 2048, 2560, 2624), axis=-1)
    wnat = jnp.concatenate([w_u, w_q, w_k, w_qi], axis=-1).astype(BF16)
    wki = jnp.concatenate([w_ki, w_ki], axis=-1).astype(BF16)
    wvt = w_v.T.astype(BF16)
    wwt = jnp.concatenate([w_wi.T, jnp.zeros((16 - IDX_HEADS, d), F32)], axis=0).astype(BF16)
    gid = jnp.arange(ATTN_WIDTH) // HEAD_DIM
    ones_bd = (gid[:, None] == gid[None, :]).astype(BF16)
    qg = (jnp.tile(q_norm_g, N_HEADS) * (HEAD_DIM ** -0.5 * LOG2E)).reshape(1, ATTN_WIDTH)
    kg = jnp.tile(k_norm_g, N_HEADS).reshape(1, ATTN_WIDTH)

    full = lambda shape: pl.BlockSpec(shape, lambda bi, ti: (0,) * len(shape))
    tok = lambda w: pl.BlockSpec((1, tm, w), lambda bi, ti: (bi, ti, 0))
    per_b = pl.BlockSpec((1, 1, d), lambda bi, ti: (bi, 0, 0))
    out_shapes = (
        jax.ShapeDtypeStruct((b, s, ATTN_WIDTH), BF16),
        jax.ShapeDtypeStruct((b, s, ATTN_WIDTH), BF16),
        jax.ShapeDtypeStruct((b, s, IDX_HEADS * IDX_DIM), BF16),
        jax.ShapeDtypeStruct((b, s, 2 * IDX_DIM), BF16),
        jax.ShapeDtypeStruct((b, s // kc, ATTN_WIDTH, kc), BF16),
        jax.ShapeDtypeStruct((b, IDX_HEADS, s), F32),
        jax.ShapeDtypeStruct((b, s, POOL_WIDTH), BF16),
    )
    out_specs = (
        tok(ATTN_WIDTH), tok(ATTN_WIDTH), tok(IDX_HEADS * IDX_DIM), tok(2 * IDX_DIM),
        pl.BlockSpec((1, tm // kc, ATTN_WIDTH, kc), lambda bi, ti: (bi, ti, 0, 0)),
        pl.BlockSpec((1, IDX_HEADS, tm), lambda bi, ti: (bi, 0, ti)),
        tok(POOL_WIDTH),
    )
    return pl.pallas_call(
        functools.partial(_inproj_kernel, tm=tm, kc=kc),
        grid=(b, nt),
        in_specs=[tok(d), per_b, per_b, full((1, d)), full((d, 2048)), full((d, 2 * IDX_DIM)),
                  full((ATTN_WIDTH, d)), full((16, d)), full((ATTN_WIDTH, ATTN_WIDTH)),
                  full((1, ATTN_WIDTH)), full((1, ATTN_WIDTH)),
                  full((len(POOL_WINDOWS), POOL_GROUP, POOL_GROUP)),
                  full((1, POOL_WIDTH)), full((1, POOL_WIDTH))],
        out_specs=out_specs,
        out_shape=out_shapes,
        scratch_shapes=[pltpu.VMEM((tm + POOL_HALO, POOL_WIDTH), F32)],
        compiler_params=pltpu.CompilerParams(
            dimension_semantics=("arbitrary", "arbitrary"), vmem_limit_bytes=48 * 1024 * 1024),
        name="inproj",
    )(x, sh1, sc1, norm_g.reshape(1, d), wnat, wki, wvt, wwt, ones_bd, qg, kg,
      pool_w.astype(BF16), pool_scale.reshape(1, -1), pool_out_g.reshape(1, -1))


def _attn_kernel(q_ref, qi_ref, wt_ref, kid_ref, k_ref, kf_ref, vt_ref, aog_ref, o_ref,
                 key_ref, acc_ref, m_ref, l_ref, qim_ref, qmf_ref, cut_ref, lg_ref,
                 *, tq, ck, seq, topk):
    j = pl.program_id(1)
    nck = ((j + 1) * tq + ck - 1) // ck
    qpos = j * tq + lax.broadcasted_iota(I32, (1, tq), 1)
    qchunk = qpos // CHUNK
    row_i = lax.broadcasted_iota(I32, (ck, tq), 0)

    lane = lax.broadcasted_iota(I32, (tq, LANES), 1)
    tcol = (j * tq + lax.broadcasted_iota(I32, (tq, LANES), 0)).astype(F32)
    for h in range(N_HEADS):
        p = h // 2
        keep = (lane < HEAD_DIM) if h % 2 == 0 else (lane >= HEAD_DIM)
        qi_slab = qi_ref[0, :, p * LANES:(p + 1) * LANES]
        q_slab = q_ref[0, :, p * LANES:(p + 1) * LANES]
        qim_ref[h * tq:(h + 1) * tq, :] = jnp.where(keep, qi_slab, jnp.zeros_like(qi_slab))
        qmf_ref[h, :, 0:LANES] = jnp.where(keep, q_slab, jnp.zeros_like(q_slab))
        c1, c2, c3 = _bf16_split3(_SLOPE_L2E[h])
        u = -_SLOPE_L2E[h] * tcol
        u1 = u.astype(BF16).astype(F32)
        r = u - u1
        u2 = r.astype(BF16).astype(F32)
        u3 = (r - u2).astype(BF16).astype(F32)
        feat = jnp.zeros((tq, LANES), F32)
        for li, val in enumerate((64.0 * c1, 64.0 * c2, 64.0 * c3, c1, c2, c3, u1, u2, u3)):
            feat = jnp.where(lane == li, val, feat)
        qmf_ref[h, :, LANES:2 * LANES] = feat.astype(BF16)

    def scores(c, last):
        r0 = pl.multiple_of(c * ck, ck)
        d = _dot_nt(kid_ref[0, pl.ds(r0, ck), :], qim_ref[...])
        acc = jnp.zeros((ck, tq), F32)
        for h in range(IDX_HEADS):
            acc = acc + wt_ref[0, h:h + 1, :] * jnp.maximum(d[:, h * tq:(h + 1) * tq], 0.0)
        if last:
            acc = jnp.where(((r0 + row_i) // CHUNK) <= qchunk, acc, NEG)
        bits = pltpu.bitcast(acc, I32)
        key_ref[pl.ds(r0, ck), :] = jnp.where(bits < 0, bits ^ 0x7FFFFFFF, bits)

    def scores_body(c, carry):
        scores(c, False)
        return carry

    lax.fori_loop(0, nck - 1, scores_body, 0)
    scores(nck - 1, True)

    def count(pred):
        def body(c, c8):
            r0 = pl.multiple_of(c * ck, ck)
            m = pred(key_ref[pl.ds(r0, ck), :], r0).astype(I32)
            return c8 + m.reshape(ck // 8, 8, tq).sum(axis=0)
        c8 = lax.fori_loop(0, nck, body, jnp.zeros((8, tq), I32))
        return c8.sum(axis=0, keepdims=True)

    c_nonneg = count(lambda kk, r0: kk >= 0)
    lo0 = jnp.where(c_nonneg >= topk, 0, -(2 ** 31)).astype(I32)

    def bit_body(b, lo):
        trial = lo + jnp.left_shift(jnp.int32(1), 30 - b)
        c = count(lambda kk, r0: kk >= trial)
        return jnp.where(c >= topk, trial, lo)

    lo = lax.fori_loop(0, 31, bit_body, lo0)

    c_gt = count(lambda kk, r0: kk > lo)
    c_ge = count(lambda kk, r0: kk >= lo)
    need = topk - c_gt
    cut_ref[...] = jnp.full((1, tq), seq, I32)

    @pl.when(jnp.max(c_ge) > topk)
    def _():
        nbits = max(1, (seq - 1).bit_length())

        def idx_body(b, jv):
            trial = jv + jnp.left_shift(jnp.int32(1), nbits - 1 - b)
            c = count(lambda kk, r0: (kk == lo)
                      & ((r0 + lax.broadcasted_iota(I32, (ck, tq), 0)) < trial))
            return jnp.where(c < need, trial, jv)

        jv = lax.fori_loop(0, nbits, idx_body, jnp.zeros((1, tq), I32))
        cut_ref[...] = jv + 1

    cut = cut_ref[...]

    m_ref[...] = jnp.full((N_HEADS, 1, tq), M_INIT, F32)
    l_ref[...] = jnp.zeros((N_HEADS, 1, tq), F32)
    acc_ref[...] = jnp.zeros((ATTN_WIDTH, tq), F32)

    def attend(c, last):
        r0 = pl.multiple_of(c * ck, ck)
        kk = key_ref[pl.ds(r0, ck), :]
        kf = kf_ref[pl.ds(r0, ck), :]
        kch = k_ref[0, pl.ds(r0, ck), :]
        vtc = vt_ref[0, c]
        kpos = r0 + row_i
        sel = (kk > lo) | ((kk == lo) & (kpos < cut))
        if last:
            sel = sel & ((kpos // CHUNK) <= qchunk)
            ahead = jnp.maximum(kpos - qpos, 0).astype(F32)
        pen = jnp.where(sel, 0.0, NEG)
        for h in range(N_HEADS):
            p = h // 2
            ks = kch[:, p * LANES:(p + 1) * LANES]
            lg = _dot_nt(jnp.concatenate([ks, kf], axis=1), qmf_ref[h]) + pen
            if last:
                lg = lg - (2.0 * _SLOPE_L2E[h]) * ahead
            lg_ref[h] = lg
        for h in range(N_HEADS):
            m_old = m_ref[h]
            m_new = jnp.maximum(m_old, jnp.max(lg_ref[h], axis=0, keepdims=True))
            alpha = jnp.exp2(m_old - m_new)
            pr = jnp.exp2(lg_ref[h] - m_new)
            l_ref[h] = alpha * l_ref[h] + jnp.sum(pr, axis=0, keepdims=True)
            m_ref[h] = m_new
            rows = slice(h * HEAD_DIM, (h + 1) * HEAD_DIM)
            acc_ref[rows, :] = alpha * acc_ref[rows, :] + _dot(vtc[rows, :], pr.astype(BF16))

    def attend_body(c, carry):
        attend(c, False)
        return carry

    lax.fori_loop(0, nck - 1, attend_body, 0)
    attend(nck - 1, True)

    outs = []
    for h in range(N_HEADS):
        rows = slice(h * HEAD_DIM, (h + 1) * HEAD_DIM)
        o = acc_ref[rows, :] / l_ref[h]
        oms = jnp.mean(o * o, axis=0, keepdims=True)
        outs.append(o * lax.rsqrt(oms + EPS))
    ot = jnp.concatenate(outs, axis=0)
    o_ref[0] = (ot.T * aog_ref[...]).astype(BF16)


def _attn(q, k, qi, kid, vt, wt, attn_out_g, tq, ck):
    b, s, _ = q.shape
    topk = min(TOPK_MAX, s // 4)
    assert tq == LANES and ck % tq == 0 and s % ck == 0 and s // CHUNK <= 256
    pos = jnp.arange(s)
    feats = [pos // CHUNK] * 3 + [pos % CHUNK] * 3 + [jnp.ones_like(pos)] * 3
    kf = jnp.zeros((s, LANES), F32).at[:, 0:N_FEAT].set(
        jnp.stack(feats, axis=1).astype(F32)).astype(BF16)
    blk = lambda w: pl.BlockSpec((1, tq, w), lambda bi, ji: (bi, ji, 0))
    whole = lambda w: pl.BlockSpec((1, s, w), lambda bi, ji: (bi, 0, 0))
    return pl.pallas_call(
        functools.partial(_attn_kernel, tq=tq, ck=ck, seq=s, topk=topk),
        grid=(b, s // tq),
        in_specs=[blk(ATTN_WIDTH), blk(IDX_HEADS * IDX_DIM),
                  pl.BlockSpec((1, IDX_HEADS, tq), lambda bi, ji: (bi, 0, ji)),
                  whole(2 * IDX_DIM), whole(ATTN_WIDTH),
                  pl.BlockSpec((s, LANES), lambda bi, ji: (0, 0)),
                  pl.BlockSpec((1, s // ck, ATTN_WIDTH, ck), lambda bi, ji: (bi, 0, 0, 0)),
                  pl.BlockSpec((1, ATTN_WIDTH), lambda bi, ji: (0, 0))],
        out_specs=blk(ATTN_WIDTH),
        out_shape=jax.ShapeDtypeStruct((b, s, ATTN_WIDTH), BF16),
        scratch_shapes=[pltpu.VMEM((s, tq), I32),
                        pltpu.VMEM((ATTN_WIDTH, tq), F32),
                        pltpu.VMEM((N_HEADS, 1, tq), F32),
                        pltpu.VMEM((N_HEADS, 1, tq), F32),
                        pltpu.VMEM((IDX_HEADS * tq, LANES), BF16),
                        pltpu.VMEM((N_HEADS, tq, 2 * LANES), BF16),
                        pltpu.VMEM((1, tq), I32),
                        pltpu.VMEM((N_HEADS, ck, tq), F32)],
        compiler_params=pltpu.CompilerParams(
            dimension_semantics=("arbitrary", "arbitrary"), vmem_limit_bytes=56 * 1024 * 1024),
        name="attn",
    )(q, qi, wt, kid, k, kf, vt, attn_out_g.reshape(1, ATTN_WIDTH))


def _outproj_kernel(pool_ref, attn_ref, x_ref, wo_ref, g1_ref, sh_ref, sc_ref, ng_ref,
                    x1_ref, h2_ref):
    mixed = _dot(pool_ref[0], wo_ref[0:POOL_WIDTH, :]) + _dot(attn_ref[0], wo_ref[POOL_WIDTH:, :])
    x1 = x_ref[0] + g1_ref[0] * mixed
    x1_ref[0] = x1
    ms = jnp.mean(x1 * x1, axis=-1, keepdims=True)
    h2 = (x1 * lax.rsqrt(ms + EPS) * ng_ref[...]) * (1.0 + sc_ref[0]) + sh_ref[0]
    h2_ref[0] = h2.astype(BF16)


def _outproj(pool_out, attn_out, x, w_out, g1, sh2, sc2, norm_g, tm):
    b, s, d = x.shape
    tok = lambda w: pl.BlockSpec((1, tm, w), lambda bi, ti: (bi, ti, 0))
    per_b = pl.BlockSpec((1, 1, d), lambda bi, ti: (bi, 0, 0))
    return pl.pallas_call(
        _outproj_kernel,
        grid=(b, s // tm),
        in_specs=[tok(POOL_WIDTH), tok(ATTN_WIDTH), tok(d),
                  pl.BlockSpec((POOL_WIDTH + ATTN_WIDTH, d), lambda bi, ti: (0, 0)),
                  per_b, per_b, per_b, pl.BlockSpec((1, d), lambda bi, ti: (0, 0))],
        out_specs=(tok(d), tok(d)),
        out_shape=(jax.ShapeDtypeStruct((b, s, d), F32), jax.ShapeDtypeStruct((b, s, d), BF16)),
        compiler_params=pltpu.CompilerParams(dimension_semantics=("arbitrary", "arbitrary")),
        name="outproj",
    )(pool_out, attn_out, x, w_out.astype(BF16), g1, sh2, sc2, norm_g.reshape(1, d))


def _peer_kernel(h2_ref, x1_ref, g2_ref, wqt_ref, sk_ref, u_ref, vt_ref, o_ref,
                 s1_ref, s2_ref, e1_ref, e2_ref, thr_ref, top_ref, cand_ref,
                 p_ref, acc_ref, s1b_ref, e1b_ref, *, tm, eb, ne):
    e = pl.program_id(1)
    nsub = eb // PEER_KEYS

    @pl.when(e == 0)
    def _prologue():
        h2 = h2_ref[...]
        ridx = lax.broadcasted_iota(I32, (PEER_KEYS, tm), 0)

        def scores(hh, carry):
            qt = _dot_nt(wqt_ref[hh], h2).astype(BF16)
            st = _dot(sk_ref[hh], qt)
            hd = hh // 2

            @pl.when(hh % 2 == 0)
            def _():
                s1_ref[hd] = st

            @pl.when(hh % 2 == 1)
            def _():
                s2_ref[hd] = st

            s = st
            for r in range(PEER_TOPK):
                mx = jnp.max(s, axis=0, keepdims=True)
                top_ref[hh, r:r + 1, :] = mx
                first = jnp.min(jnp.where(s == mx, ridx, PEER_KEYS), axis=0, keepdims=True)
                s = jnp.where(ridx == first, -jnp.inf, s)
            return carry

        lax.fori_loop(0, 2 * PEER_HEADS, scores, 0)

        def gates(hd, carry):
            a1 = top_ref[2 * hd]
            for r2 in range(PEER_TOPK):
                cand_ref[r2 * PEER_TOPK:(r2 + 1) * PEER_TOPK, :] = (
                    a1 + top_ref[2 * hd + 1, r2:r2 + 1, :])
            cand = cand_ref[...]
            c = cand
            thr = jnp.full((1, tm), -jnp.inf, F32)
            remaining = jnp.full((1, tm), PEER_TOPK, I32)
            for r in range(PEER_TOPK):
                mx = jnp.max(c, axis=0, keepdims=True)
                eq = c == mx
                thr = jnp.where(remaining > 0, mx, thr)
                remaining = remaining - jnp.sum(eq.astype(I32), axis=0, keepdims=True)
                c = jnp.where(eq, -jnp.inf, c)
            m1 = top_ref[2 * hd, 0:1, :]
            m2 = top_ref[2 * hd + 1, 0:1, :]
            z = jnp.sum(jnp.where(cand >= thr, jnp.exp(cand - (m1 + m2)), 0.0),
                        axis=0, keepdims=True)
            thr_ref[pl.ds(hd, 1), :] = thr
            e1_ref[hd] = jnp.exp(s1_ref[hd] - m1)
            e2_ref[hd] = jnp.exp(s2_ref[hd] - m2) / z
            return carry

        lax.fori_loop(0, PEER_HEADS, gates, 0)
        acc_ref[...] = jnp.zeros_like(acc_ref)

    i0 = pl.multiple_of(e * nsub, nsub)
    s1rows = [s1_ref[hd, pl.ds(i0, nsub), :] for hd in range(PEER_HEADS)]
    e1rows = [e1_ref[hd, pl.ds(i0, nsub), :] for hd in range(PEER_HEADS)]
    for hd in range(PEER_HEADS):
        s1b_ref[hd] = s1rows[hd]
        e1b_ref[hd] = e1rows[hd]
    for ii in range(nsub):
        rows = slice(ii * PEER_KEYS, (ii + 1) * PEER_KEYS)
        a_rows = _dot_nt(u_ref[rows, :], h2_ref[...])
        for tl in range(tm // LANES):
            cols = slice(tl * LANES, (tl + 1) * LANES)
            w = jnp.zeros((PEER_KEYS, LANES), F32)
            for hd in range(PEER_HEADS):
                sm = s1b_ref[hd, ii:ii + 1, cols] + s2_ref[hd, :, cols]
                gate = e2_ref[hd, :, cols] * e1b_ref[hd, ii:ii + 1, cols]
                w = w + jnp.where(sm >= thr_ref[hd:hd + 1, cols], gate, 0.0)
            a = a_rows[:, cols]
            gl = 0.5 * a * (1.0 + lax.erf(a * (1.0 / math.sqrt(2.0))))
            p_ref[rows, cols] = (gl * w).astype(BF16)
    acc_ref[...] += _dot(vt_ref[...], p_ref[...])

    @pl.when(e == ne - 1)
    def _epilogue():
        o_ref[...] = x1_ref[...] + g2_ref[0] * acc_ref[...].T


def _peer(h2, x1, g2, wq, sub_keys, u_tab, v_tab, seq, tm, eb):
    t, d = h2.shape
    ne = u_tab.shape[0] // eb
    half = PEER_KEYS
    wqt = wq.T.reshape(2 * PEER_HEADS, half, d).astype(BF16)
    sk = sub_keys.reshape(2 * PEER_HEADS, PEER_KEYS, half).astype(BF16)
    ub = u_tab.astype(BF16)
    vtb = v_tab.T.astype(BF16)
    hs = (PEER_HEADS, PEER_KEYS, tm)
    return pl.pallas_call(
        functools.partial(_peer_kernel, tm=tm, eb=eb, ne=ne),
        grid=(t // tm, ne),
        in_specs=[pl.BlockSpec((tm, d), lambda i, e: (i, 0)),
                  pl.BlockSpec((tm, d), lambda i, e: (i, 0)),
                  pl.BlockSpec((1, 1, d), lambda i, e: ((i * tm) // seq, 0, 0)),
                  pl.BlockSpec((2 * PEER_HEADS, half, d), lambda i, e: (0, 0, 0)),
                  pl.BlockSpec((2 * PEER_HEADS, PEER_KEYS, half), lambda i, e: (0, 0, 0)),
                  pl.BlockSpec((eb, d), lambda i, e: (e, 0)),
                  pl.BlockSpec((d, eb), lambda i, e: (0, e))],
        out_specs=pl.BlockSpec((tm, d), lambda i, e: (i, 0)),
        out_shape=jax.ShapeDtypeStruct((t, d), F32),
        scratch_shapes=[pltpu.VMEM(hs, F32), pltpu.VMEM(hs, F32), pltpu.VMEM(hs, F32),
                        pltpu.VMEM(hs, F32),
                        pltpu.VMEM((PEER_HEADS, tm), F32),
                        pltpu.VMEM((2 * PEER_HEADS, PEER_TOPK, tm), F32),
                        pltpu.VMEM((PEER_TOPK * PEER_TOPK, tm), F32),
                        pltpu.VMEM((eb, tm), BF16),
                        pltpu.VMEM((d, tm), F32),
                        pltpu.VMEM((PEER_HEADS, eb // PEER_KEYS, tm), F32),
                        pltpu.VMEM((PEER_HEADS, eb // PEER_KEYS, tm), F32)],
        compiler_params=pltpu.CompilerParams(
            dimension_semantics=("arbitrary", "arbitrary"), vmem_limit_bytes=58 * 1024 * 1024),
        name="peer",
    )(h2, x1, g2, wqt, sk, ub, vtb)


def _tile(n, pref):
    t = min(pref, n)
    assert n % t == 0
    return t


def kernel(x, c, ada_w, ada_b, norm1_g, w_in, pool_w, pool_scale, pool_out_g, q_norm_g, k_norm_g,
           attn_out_g, w_out, norm2_g, peer_wq, peer_subkeys, peer_u, peer_v):
    b, s, d = x.shape
    depth = ada_w.shape[0]
    tm = _tile(s, 512)
    tq = LANES
    ck = _tile(tm, 512)
    tp = _tile(s, 512)
    eb = 1024
    for l in range(depth):
        mod = _ada(c, ada_w[l], ada_b[l]).reshape(b, 1, 6 * d)
        sh1, sc1, g1, sh2, sc2, g2 = jnp.split(mod, 6, axis=-1)
        q, k, qi, kid, vt, wt, pool_out = _inproj(
            x, sh1, sc1, norm1_g[l], w_in[l], pool_w[l], pool_scale[l], pool_out_g[l],
            q_norm_g[l], k_norm_g[l], tm, ck)
        attn_out = _attn(q, k, qi, kid, vt, wt, attn_out_g[l], tq, ck)
        x1, h2 = _outproj(pool_out, attn_out, x, w_out[l], g1, sh2, sc2, norm2_g[l], tm)
        x = _peer(h2.reshape(b * s, d), x1.reshape(b * s, d), g2, peer_wq[l], peer_subkeys[l],
                  peer_u[l], peer_v[l], s, tp, eb).reshape(b, s, d)
    return x
```

```python
import functools
import math

import ml_dtypes
import numpy as np

import jax
import jax.numpy as jnp
from jax import lax
from jax.experimental import pallas as pl
from jax.experimental.pallas import tpu as pltpu

F32 = jnp.float32
BF16 = jnp.bfloat16
I32 = jnp.int32

CHUNK = 64
EPS = 1e-6
POOL_WINDOWS = (2, 4, 8, 16)
POOL_GROUP = 128
POOL_WIDTH = 512
ATTN_WIDTH = 512
HEAD_DIM = 64
N_HEADS = 8
IDX_HEADS = 8
IDX_DIM = 64
TOPK_MAX = 256
PEER_HEADS = 8
PEER_KEYS = 128
PEER_TOPK = 16
NEG = -1e30

LANES = 128
POOL_HALO = 16
LOG2E = math.log2(math.e)
M_INIT = 0.5 * NEG
N_FEAT = 9

NT_DIMS = (((1,), (1,)), ((), ()))


def _dot(a, b):
    return jnp.dot(a, b, preferred_element_type=F32)


def _dot_nt(a, b):
    return lax.dot_general(a, b, NT_DIMS, preferred_element_type=F32)


def _f32_key(v):
    b = int(np.float32(v).view(np.int32))
    return b ^ 0x7FFFFFFF if b < 0 else b


def _bf16_split3(c):
    c = np.float32(c)
    p1 = np.float32(c.astype(ml_dtypes.bfloat16))
    r = np.float32(c - p1)
    p2 = np.float32(r.astype(ml_dtypes.bfloat16))
    p3 = np.float32(np.float32(r - p2).astype(ml_dtypes.bfloat16))
    return float(p1), float(p2), float(p3)


_SLOPE_L2E = tuple(float(np.float32(2.0 ** (-8.0 * (h + 1) / N_HEADS) * LOG2E)) for h in range(N_HEADS))


def _ada_kernel(c_ref, w_ref, b_ref, o_ref):
    c = c_ref[...]
    cond = c * jax.nn.sigmoid(c)
    o_ref[...] = jnp.dot(cond, w_ref[...], preferred_element_type=F32,
                         precision=lax.Precision.HIGHEST) + b_ref[...]


def _ada(c, w, b):
    bsz, d = c.shape
    n = w.shape[1]
    tn = 1024
    return pl.pallas_call(
        _ada_kernel,
        grid=(n // tn,),
        in_specs=[pl.BlockSpec((bsz, d), lambda j: (0, 0)),
                  pl.BlockSpec((d, tn), lambda j: (0, j)),
                  pl.BlockSpec((1, tn), lambda j: (0, j))],
        out_specs=pl.BlockSpec((bsz, tn), lambda j: (0, j)),
        out_shape=jax.ShapeDtypeStruct((bsz, n), F32),
        name="ada",
    )(c, w, b.reshape(1, n))


def _group_sumsq(t, ones_bd):
    sq = t * t
    hi = sq.astype(BF16)
    lo = (sq - hi.astype(F32)).astype(BF16)
    return _dot(hi, ones_bd) + _dot(lo, ones_bd)


def _inproj_kernel(x_ref, sh_ref, sc_ref, g_ref, wnat_ref, wki_ref, wvt_ref, wwt_ref, ones_ref,
                   qg_ref, kg_ref, poolw_ref, pscale_ref, pg_ref,
                   q_ref, k_ref, qi_ref, kid_ref, vt_ref, wt_ref, pool_ref,
                   ubuf, *, tm, kc):
    i = pl.program_id(1)
    x = x_ref[0]
    ms = jnp.mean(x * x, axis=-1, keepdims=True)
    h = (x * lax.rsqrt(ms + EPS) * g_ref[...]) * (1.0 + sc_ref[0]) + sh_ref[0]
    hb = h.astype(BF16)

    proj = _dot(hb, wnat_ref[...])
    u = proj[:, 0:512]
    q = proj[:, 512:1024]
    k = proj[:, 1024:1536]
    qi = proj[:, 1536:2048]

    ones_bd = ones_ref[...]
    qn = q * lax.rsqrt(_group_sumsq(q, ones_bd) * (1.0 / HEAD_DIM) + EPS) * qg_ref[...]
    kn = k * lax.rsqrt(_group_sumsq(k, ones_bd) * (1.0 / HEAD_DIM) + EPS) * kg_ref[...]
    q_ref[0] = qn.astype(BF16)
    k_ref[0] = kn.astype(BF16)
    qi_ref[0] = (qi * (IDX_DIM ** -0.5)).astype(BF16)
    kid_ref[0] = _dot(hb, wki_ref[...]).astype(BF16)
    vt = _dot_nt(wvt_ref[...], hb).astype(BF16)
    for ci in range(tm // kc):
        vt_ref[0, ci] = vt[:, ci * kc:(ci + 1) * kc]
    wt = _dot_nt(wwt_ref[...], hb)
    wt_ref[0] = wt[0:IDX_HEADS, :] * (IDX_HEADS ** -0.5)

    @pl.when(i == 0)
    def _():
        ubuf[0:POOL_HALO, :] = jnp.zeros((POOL_HALO, POOL_WIDTH), F32)

    ubuf[POOL_HALO:POOL_HALO + tm, :] = u
    tpos = i * tm + lax.broadcasted_iota(I32, (tm, POOL_GROUP), 0)
    parts = []
    for gi, win in enumerate(POOL_WINDOWS):
        lo_l, hi_l = gi * POOL_GROUP, (gi + 1) * POOL_GROUP
        ws = ubuf[POOL_HALO:POOL_HALO + tm, lo_l:hi_l]
        for j in range(1, win):
            ws = ws + ubuf[POOL_HALO - j:POOL_HALO - j + tm, lo_l:hi_l]
        cnt = jnp.minimum(tpos + 1, win).astype(F32)
        pooled = ws / cnt - u[:, lo_l:hi_l]
        parts.append(_dot(pooled.astype(BF16), poolw_ref[gi]))
    mixed = jnp.concatenate(parts, axis=-1) * pscale_ref[...]
    pms = jnp.mean(mixed * mixed, axis=-1, keepdims=True)
    pool_ref[0] = (mixed * lax.rsqrt(pms + EPS) * pg_ref[...]).astype(BF16)
    ubuf[0:POOL_HALO, :] = ubuf[tm:tm + POOL_HALO, :]


def _inproj(x, sh1, sc1, norm_g, w_in, pool_w, pool_scale, pool_out_g, q_norm_g, k_norm_g, tm, kc):
    b, s, d = x.shape
    nt = s // tm
    w_u, w_q, w_k, w_v, w_qi, w_ki, w_wi = jnp.split(
        w_in, (512, 1024, 1536, 2048, 2560, 2624), axis=-1)
    wnat = jnp.concatenate([w_u, w_q, w_k, w_qi], axis=-1).astype(BF16)
    wki = jnp.concatenate([w_ki, w_ki], axis=-1).astype(BF16)
    wvt = w_v.T.astype(BF16)
    wwt = jnp.concatenate([w_wi.T, jnp.zeros((16 - IDX_HEADS, d), F32)], axis=0).astype(BF16)
    gid = jnp.arange(ATTN_WIDTH) // HEAD_DIM
    ones_bd = (gid[:, None] == gid[None, :]).astype(BF16)
    qg = (jnp.tile(q_norm_g, N_HEADS) * (HEAD_DIM ** -0.5 * LOG2E)).reshape(1, ATTN_WIDTH)
    kg = jnp.tile(k_norm_g, N_HEADS).reshape(1, ATTN_WIDTH)

    full = lambda shape: pl.BlockSpec(shape, lambda bi, ti: (0,) * len(shape))
    tok = lambda w: pl.BlockSpec((1, tm, w), lambda bi, ti: (bi, ti, 0))
    per_b = pl.BlockSpec((1, 1, d), lambda bi, ti: (bi, 0, 0))
    out_shapes = (
        jax.ShapeDtypeStruct((b, s, ATTN_WIDTH), BF16),
        jax.ShapeDtypeStruct((b, s, ATTN_WIDTH), BF16),
        jax.ShapeDtypeStruct((b, s, IDX_HEADS * IDX_DIM), BF16),
        jax.ShapeDtypeStruct((b, s, 2 * IDX_DIM), BF16),
        jax.ShapeDtypeStruct((b, s // kc, ATTN_WIDTH, kc), BF16),
        jax.ShapeDtypeStruct((b, IDX_HEADS, s), F32),
        jax.ShapeDtypeStruct((b, s, POOL_WIDTH), BF16),
    )
    out_specs = (
        tok(ATTN_WIDTH), tok(ATTN_WIDTH), tok(IDX_HEADS * IDX_DIM), tok(2 * IDX_DIM),
        pl.BlockSpec((1, tm // kc, ATTN_WIDTH, kc), lambda bi, ti: (bi, ti, 0, 0)),
        pl.BlockSpec((1, IDX_HEADS, tm), lambda bi, ti: (bi, 0, ti)),
        tok(POOL_WIDTH),
    )
    return pl.pallas_call(
        functools.partial(_inproj_kernel, tm=tm, kc=kc),
        grid=(b, nt),
        in_specs=[tok(d), per_b, per_b, full((1, d)), full((d, 2048)), full((d, 2 * IDX_DIM)),
                  full((ATTN_WIDTH, d)), full((16, d)), full((ATTN_WIDTH, ATTN_WIDTH)),
                  full((1, ATTN_WIDTH)), full((1, ATTN_WIDTH)),
                  full((len(POOL_WINDOWS), POOL_GROUP, POOL_GROUP)),
                  full((1, POOL_WIDTH)), full((1, POOL_WIDTH))],
        out_specs=out_specs,
        out_shape=out_shapes,
        scratch_shapes=[pltpu.VMEM((tm + POOL_HALO, POOL_WIDTH), F32)],
        compiler_params=pltpu.CompilerParams(
            dimension_semantics=("arbitrary", "arbitrary"), vmem_limit_bytes=48 * 1024 * 1024),
        name="inproj",
    )(x, sh1, sc1, norm_g.reshape(1, d), wnat, wki, wvt, wwt, ones_bd, qg, kg,
      pool_w.astype(BF16), pool_scale.reshape(1, -1), pool_out_g.reshape(1, -1))


def _attn_kernel(q_ref, qi_ref, wt_ref, kid_ref, k_ref, kf_ref, vt_ref, aog_ref, o_ref,
                 key_ref, acc_ref, m_ref, l_ref, qim_ref, qmf_ref, cut_ref, lg_ref,
                 *, tq, ck, seq, topk):
    j = pl.program_id(1)
    nck = ((j + 1) * tq + ck - 1) // ck
    qpos = j * tq + lax.broadcasted_iota(I32, (1, tq), 1)
    qchunk = qpos // CHUNK
    row_i = lax.broadcasted_iota(I32, (ck, tq), 0)

    lane = lax.broadcasted_iota(I32, (tq, LANES), 1)
    tcol = (j * tq + lax.broadcasted_iota(I32, (tq, LANES), 0)).astype(F32)
    for h in range(N_HEADS):
        p = h // 2
        keep = (lane < HEAD_DIM) if h % 2 == 0 else (lane >= HEAD_DIM)
        qi_slab = qi_ref[0, :, p * LANES:(p + 1) * LANES]
        q_slab = q_ref[0, :, p * LANES:(p + 1) * LANES]
        qim_ref[h * tq:(h + 1) * tq, :] = jnp.where(keep, qi_slab, jnp.zeros_like(qi_slab))
        qmf_ref[h, :, 0:LANES] = jnp.where(keep, q_slab, jnp.zeros_like(q_slab))
        c1, c2, c3 = _bf16_split3(_SLOPE_L2E[h])
        u = -_SLOPE_L2E[h] * tcol
        u1 = u.astype(BF16).astype(F32)
        r = u - u1
        u2 = r.astype(BF16).astype(F32)
        u3 = (r - u2).astype(BF16).astype(F32)
        feat = jnp.zeros((tq, LANES), F32)
        for li, val in enumerate((64.0 * c1, 64.0 * c2, 64.0 * c3, c1, c2, c3, u1, u2, u3)):
            feat = jnp.where(lane == li, val, feat)
        qmf_ref[h, :, LANES:2 * LANES] = feat.astype(BF16)

    def scores(c, last):
        r0 = pl.multiple_of(c * ck, ck)
        d = _dot_nt(kid_ref[0, pl.ds(r0, ck), :], qim_ref[...])
        acc = jnp.zeros((ck, tq), F32)
        for h in range(IDX_HEADS):
            acc = acc + wt_ref[0, h:h + 1, :] * jnp.maximum(d[:, h * tq:(h + 1) * tq], 0.0)
        if last:
            acc = jnp.where(((r0 + row_i) // CHUNK) <= qchunk, acc, NEG)
        bits = pltpu.bitcast(acc, I32)
        key_ref[pl.ds(r0, ck), :] = jnp.where(bits < 0, bits ^ 0x7FFFFFFF, bits)

    def scores_body(c, carry):
        scores(c, False)
        return carry

    lax.fori_loop(0, nck - 1, scores_body, 0)
    scores(nck - 1, True)

    def count(pred):
        def body(c, c8):
            r0 = pl.multiple_of(c * ck, ck)
            m = pred(key_ref[pl.ds(r0, ck), :], r0).astype(I32)
            return c8 + m.reshape(ck // 8, 8, tq).sum(axis=0)
        c8 = lax.fori_loop(0, nck, body, jnp.zeros((8, tq), I32))
        return c8.sum(axis=0, keepdims=True)

    c_nonneg = count(lambda kk, r0: kk >= 0)
    lo0 = jnp.where(c_nonneg >= topk, 0, -(2 ** 31)).astype(I32)

    def bit_body(b, lo):
        trial = lo + jnp.left_shift(jnp.int32(1), 30 - b)
        c = count(lambda kk, r0: kk >= trial)
        return jnp.where(c >= topk, trial, lo)

    lo = lax.fori_loop(0, 31, bit_body, lo0)

    c_gt = count(lambda kk, r0: kk > lo)
    c_ge = count(lambda kk, r0: kk >= lo)
    need = topk - c_gt
    cut_ref[...] = jnp.full((1, tq), seq, I32)

    @pl.when(jnp.max(c_ge) > topk)
    def _():
        nbits = max(1, (seq - 1).bit_length())

        def idx_body(b, jv):
            trial = jv + jnp.left_shift(jnp.int32(1), nbits - 1 - b)
            c = count(lambda kk, r0: (kk == lo)
                      & ((r0 + lax.broadcasted_iota(I32, (ck, tq), 0)) < trial))
            return jnp.where(c < need, trial, jv)

        jv = lax.fori_loop(0, nbits, idx_body, jnp.zeros((1, tq), I32))
        cut_ref[...] = jv + 1

    cut = cut_ref[...]

    m_ref[...] = jnp.full((N_HEADS, 1, tq), M_INIT, F32)
    l_ref[...] = jnp.zeros((N_HEADS, 1, tq), F32)
    acc_ref[...] = jnp.zeros((ATTN_WIDTH, tq), F32)

    def attend(c, last):
        r0 = pl.multiple_of(c * ck, ck)
        kk = key_ref[pl.ds(r0, ck), :]
        kf = kf_ref[pl.ds(r0, ck), :]
        kch = k_ref[0, pl.ds(r0, ck), :]
        vtc = vt_ref[0, c]
        kpos = r0 + row_i
        sel = (kk > lo) | ((kk == lo) & (kpos < cut))
        if last:
            sel = sel & ((kpos // CHUNK) <= qchunk)
            ahead = jnp.maximum(kpos - qpos, 0).astype(F32)
        pen = jnp.where(sel, 0.0, NEG)
        for h in range(N_HEADS):
            p = h // 2
            ks = kch[:, p * LANES:(p + 1) * LANES]
            lg = _dot_nt(jnp.concatenate([ks, kf], axis=1), qmf_ref[h]) + pen
            if last:
                lg = lg - (2.0 * _SLOPE_L2E[h]) * ahead
            lg_ref[h] = lg
        for h in range(N_HEADS):
            m_old = m_ref[h]
            m_new = jnp.maximum(m_old, jnp.max(lg_ref[h], axis=0, keepdims=True))
            alpha = jnp.exp2(m_old - m_new)
            pr = jnp.exp2(lg_ref[h] - m_new)
            l_ref[h] = alpha * l_ref[h] + jnp.sum(pr, axis=0, keepdims=True)
            m_ref[h] = m_new
            rows = slice(h * HEAD_DIM, (h + 1) * HEAD_DIM)
            acc_ref[rows, :] = alpha * acc_ref[rows, :] + _dot(vtc[rows, :], pr.astype(BF16))

    def attend_body(c, carry):
        attend(c, False)
        return carry

    lax.fori_loop(0, nck - 1, attend_body, 0)
    attend(nck - 1, True)

    outs = []
    for h in range(N_HEADS):
        rows = slice(h * HEAD_DIM, (h + 1) * HEAD_DIM)
        o = acc_ref[rows, :] / l_ref[h]
        oms = jnp.mean(o * o, axis=0, keepdims=True)
        outs.append(o * lax.rsqrt(oms + EPS))
    ot = jnp.concatenate(outs, axis=0)
    o_ref[0] = (ot.T * aog_ref[...]).astype(BF16)


def _attn(q, k, qi, kid, vt, wt, attn_out_g, tq, ck):
    b, s, _ = q.shape
    topk = min(TOPK_MAX, s // 4)
    assert tq == LANES and ck % tq == 0 and s % ck == 0 and s // CHUNK <= 256
    pos = jnp.arange(s)
    feats = [pos // CHUNK] * 3 + [pos % CHUNK] * 3 + [jnp.ones_like(pos)] * 3
    kf = jnp.zeros((s, LANES), F32).at[:, 0:N_FEAT].set(
        jnp.stack(feats, axis=1).astype(F32)).astype(BF16)
    blk = lambda w: pl.BlockSpec((1, tq, w), lambda bi, ji: (bi, ji, 0))
    whole = lambda w: pl.BlockSpec((1, s, w), lambda bi, ji: (bi, 0, 0))
    return pl.pallas_call(
        functools.partial(_attn_kernel, tq=tq, ck=ck, seq=s, topk=topk),
        grid=(b, s // tq),
        in_specs=[blk(ATTN_WIDTH), blk(IDX_HEADS * IDX_DIM),
                  pl.BlockSpec((1, IDX_HEADS, tq), lambda bi, ji: (bi, 0, ji)),
                  whole(2 * IDX_DIM), whole(ATTN_WIDTH),
                  pl.BlockSpec((s, LANES), lambda bi, ji: (0, 0)),
                  pl.BlockSpec((1, s // ck, ATTN_WIDTH, ck), lambda bi, ji: (bi, 0, 0, 0)),
                  pl.BlockSpec((1, ATTN_WIDTH), lambda bi, ji: (0, 0))],
        out_specs=blk(ATTN_WIDTH),
        out_shape=jax.ShapeDtypeStruct((b, s, ATTN_WIDTH), BF16),
        scratch_shapes=[pltpu.VMEM((s, tq), I32),
                        pltpu.VMEM((ATTN_WIDTH, tq), F32),
                        pltpu.VMEM((N_HEADS, 1, tq), F32),
                        pltpu.VMEM((N_HEADS, 1, tq), F32),
                        pltpu.VMEM((IDX_HEADS * tq, LANES), BF16),
                        pltpu.VMEM((N_HEADS, tq, 2 * LANES), BF16),
                        pltpu.VMEM((1, tq), I32),
                        pltpu.VMEM((N_HEADS, ck, tq), F32)],
        compiler_params=pltpu.CompilerParams(
            dimension_semantics=("arbitrary", "arbitrary"), vmem_limit_bytes=56 * 1024 * 1024),
        name="attn",
    )(q, qi, wt, kid, k, kf, vt, attn_out_g.reshape(1, ATTN_WIDTH))


def _outproj_kernel(pool_ref, attn_ref, x_ref, wo_ref, g1_ref, sh_ref, sc_ref, ng_ref,
                    x1_ref, h2t_ref):
    mixed = _dot(pool_ref[0], wo_ref[0:POOL_WIDTH, :]) + _dot(attn_ref[0], wo_ref[POOL_WIDTH:, :])
    x1 = x_ref[0] + g1_ref[0] * mixed
    x1_ref[0] = x1
    ms = jnp.mean(x1 * x1, axis=-1, keepdims=True)
    h2 = (x1 * lax.rsqrt(ms + EPS) * ng_ref[...]) * (1.0 + sc_ref[0]) + sh_ref[0]
    h2t_ref[...] = h2.T.astype(BF16)


def _outproj(pool_out, attn_out, x, w_out, g1, sh2, sc2, norm_g, tm):
    b, s, d = x.shape
    nt = s // tm
    tok = lambda w: pl.BlockSpec((1, tm, w), lambda bi, ti: (bi, ti, 0))
    per_b = pl.BlockSpec((1, 1, d), lambda bi, ti: (bi, 0, 0))
    return pl.pallas_call(
        _outproj_kernel,
        grid=(b, nt),
        in_specs=[tok(POOL_WIDTH), tok(ATTN_WIDTH), tok(d),
                  pl.BlockSpec((POOL_WIDTH + ATTN_WIDTH, d), lambda bi, ti: (0, 0)),
                  per_b, per_b, per_b, pl.BlockSpec((1, d), lambda bi, ti: (0, 0))],
        out_specs=(tok(d), pl.BlockSpec((d, tm), lambda bi, ti: (0, bi * nt + ti))),
        out_shape=(jax.ShapeDtypeStruct((b, s, d), F32), jax.ShapeDtypeStruct((d, b * s), BF16)),
        compiler_params=pltpu.CompilerParams(dimension_semantics=("arbitrary", "arbitrary")),
        name="outproj",
    )(pool_out, attn_out, x, w_out.astype(BF16), g1, sh2, sc2, norm_g.reshape(1, d))


def _batcher_pairs(n):
    pairs = []

    def merge(lo, cnt, r):
        step = r * 2
        if step < cnt:
            merge(lo, cnt, step)
            merge(lo + r, cnt, step)
            for i in range(lo + r, lo + cnt - r, step):
                pairs.append((i, i + r))
        else:
            pairs.append((lo, lo + r))

    def sort(lo, cnt):
        if cnt > 1:
            m = cnt // 2
            sort(lo, m)
            sort(lo + m, m)
            merge(lo, cnt, 1)

    sort(0, n)
    return tuple(pairs)


_SORT16 = _batcher_pairs(PEER_TOPK)


def _top16_desc(x):
    x = list(x)
    for i, j in _SORT16:
        x[i], x[j] = jnp.maximum(x[i], x[j]), jnp.minimum(x[i], x[j])
    for shift in (4, 2, 1):
        y = [jnp.maximum(x[k], pltpu.roll(x[PEER_TOPK - 1 - k], shift, 0)) for k in range(PEER_TOPK)]
        for d in (8, 4, 2, 1):
            for k in range(PEER_TOPK):
                if k & d == 0:
                    y[k], y[k + d] = jnp.maximum(y[k], y[k + d]), jnp.minimum(y[k], y[k + d])
        x = y
    return x


def _peer_kernel(h2t_ref, x1_ref, g2_ref, wqt_ref, sk_ref, u_ref, vt_ref, o_ref,
                 s1_ref, s2_ref, e1_ref, e2_ref, thr_ref,
                 p_ref, acc_ref, s1b_ref, e1b_ref, *, tm, eb, ne):
    e = pl.program_id(1)
    nsub = eb // PEER_KEYS

    @pl.when(e == 0)
    def _prologue():
        h2t = h2t_ref[...]
        sub = lax.broadcasted_iota(I32, (8, tm), 0)
        ninf = jnp.full((8, tm), -jnp.inf, F32)

        def compact(t, base):
            out = t[base + 7]
            for r in range(6, -1, -1):
                out = jnp.where(sub == r, t[base + r], out)
            return out

        def split8(st):
            return [st[k * 8:(k + 1) * 8, :] for k in range(PEER_KEYS // 8)]

        def head(hd, carry):
            q1 = _dot(wqt_ref[2 * hd], h2t).astype(BF16)
            q2 = _dot(wqt_ref[2 * hd + 1], h2t).astype(BF16)
            st1 = _dot(sk_ref[2 * hd], q1)
            st2 = _dot(sk_ref[2 * hd + 1], q2)
            s1_ref[hd] = st1
            s2_ref[hd] = st2
            t1 = _top16_desc(split8(st1))
            t2 = _top16_desc(split8(st2))
            a2lo, a2hi, a1hi = compact(t2, 0), compact(t2, 8), compact(t1, 8)
            cands = [t1[0] + a2lo, t1[0] + a2hi, t1[1] + a2lo]
            for r1 in range(2, 8):
                cands.append(jnp.where(sub < PEER_TOPK // (r1 + 1), t1[r1] + a2lo, ninf))
            cands.append(a1hi + t2[0])
            thr = _top16_desc(cands + [ninf] * (PEER_TOPK - len(cands)))[PEER_TOPK - 1]
            m = t1[0] + t2[0]
            z8 = jnp.zeros((8, tm), F32)
            for c in cands:
                z8 = z8 + jnp.where(c >= thr, jnp.exp(c - m), 0.0)
            z = jnp.sum(z8, axis=0, keepdims=True)
            thr_ref[pl.ds(hd, 1), :] = thr[0:1, :]
            e1_ref[hd] = jnp.exp(st1 - t1[0][0:1, :])
            e2_ref[hd] = jnp.exp(st2 - t2[0][0:1, :]) / z
            return carry

        lax.fori_loop(0, PEER_HEADS, head, 0)
        acc_ref[...] = jnp.zeros_like(acc_ref)

    i0 = pl.multiple_of(e * nsub, nsub)
    s1rows = [s1_ref[hd, pl.ds(i0, nsub), :] for hd in range(PEER_HEADS)]
    e1rows = [e1_ref[hd, pl.ds(i0, nsub), :] for hd in range(PEER_HEADS)]
    for hd in range(PEER_HEADS):
        s1b_ref[hd] = s1rows[hd]
        e1b_ref[hd] = e1rows[hd]
    for ii in range(nsub):
        rows = slice(ii * PEER_KEYS, (ii + 1) * PEER_KEYS)
        a_rows = _dot(u_ref[rows, :], h2t_ref[...])
        for tl in range(tm // LANES):
            cols = slice(tl * LANES, (tl + 1) * LANES)
            w = jnp.zeros((PEER_KEYS, LANES), F32)
            for hd in range(PEER_HEADS):
                sm = s1b_ref[hd, ii:ii + 1, cols] + s2_ref[hd, :, cols]
                gate = e2_ref[hd, :, cols] * e1b_ref[hd, ii:ii + 1, cols]
                w = w + jnp.where(sm >= thr_ref[hd:hd + 1, cols], gate, 0.0)
            a = a_rows[:, cols]
            gl = 0.5 * a * (1.0 + lax.erf(a * (1.0 / math.sqrt(2.0))))
            p_ref[rows, cols] = (gl * w).astype(BF16)
    acc_ref[...] += _dot(vt_ref[...], p_ref[...])

    @pl.when(e == ne - 1)
    def _epilogue():
        o_ref[...] = x1_ref[...] + g2_ref[0] * acc_ref[...].T


def _peer(h2t, x1, g2, wq, sub_keys, u_tab, v_tab, seq, tm, eb):
    d, t = h2t.shape
    ne = u_tab.shape[0] // eb
    half = PEER_KEYS
    wqt = wq.T.reshape(2 * PEER_HEADS, half, d).astype(BF16)
    sk = sub_keys.reshape(2 * PEER_HEADS, PEER_KEYS, half).astype(BF16)
    ub = u_tab.astype(BF16)
    vtb = v_tab.T.astype(BF16)
    hs = (PEER_HEADS, PEER_KEYS, tm)
    return pl.pallas_call(
        functools.partial(_peer_kernel, tm=tm, eb=eb, ne=ne),
        grid=(t // tm, ne),
        in_specs=[pl.BlockSpec((d, tm), lambda i, e: (0, i)),
                  pl.BlockSpec((tm, d), lambda i, e: (i, 0)),
                  pl.BlockSpec((1, 1, d), lambda i, e: ((i * tm) // seq, 0, 0)),
                  pl.BlockSpec((2 * PEER_HEADS, half, d), lambda i, e: (0, 0, 0)),
                  pl.BlockSpec((2 * PEER_HEADS, PEER_KEYS, half), lambda i, e: (0, 0, 0)),
                  pl.BlockSpec((eb, d), lambda i, e: (e, 0)),
                  pl.BlockSpec((d, eb), lambda i, e: (0, e))],
        out_specs=pl.BlockSpec((tm, d), lambda i, e: (i, 0)),
        out_shape=jax.ShapeDtypeStruct((t, d), F32),
        scratch_shapes=[pltpu.VMEM(hs, F32), pltpu.VMEM(hs, F32), pltpu.VMEM(hs, F32),
                        pltpu.VMEM(hs, F32),
                        pltpu.VMEM((PEER_HEADS, tm), F32),
                        pltpu.VMEM((eb, tm), BF16),
                        pltpu.VMEM((d, tm), F32),
                        pltpu.VMEM((PEER_HEADS, eb // PEER_KEYS, tm), F32),
                        pltpu.VMEM((PEER_HEADS, eb // PEER_KEYS, tm), F32)],
        compiler_params=pltpu.CompilerParams(
            dimension_semantics=("arbitrary", "arbitrary"), vmem_limit_bytes=58 * 1024 * 1024),
        name="peer",
    )(h2t, x1, g2, wqt, sk, ub, vtb)


def _tile(n, pref):
    t = min(pref, n)
    assert n % t == 0
    return t


def kernel(x, c, ada_w, ada_b, norm1_g, w_in, pool_w, pool_scale, pool_out_g, q_norm_g, k_norm_g,
           attn_out_g, w_out, norm2_g, peer_wq, peer_subkeys, peer_u, peer_v):
    b, s, d = x.shape
    depth = ada_w.shape[0]
    tm = _tile(s, 512)
    tq = LANES
    ck = _tile(tm, 512)
    tp = _tile(s, 512)
    eb = 1024
    for l in range(depth):
        mod = _ada(c, ada_w[l], ada_b[l]).reshape(b, 1, 6 * d)
        sh1, sc1, g1, sh2, sc2, g2 = jnp.split(mod, 6, axis=-1)
        q, k, qi, kid, vt, wt, pool_out = _inproj(
            x, sh1, sc1, norm1_g[l], w_in[l], pool_w[l], pool_scale[l], pool_out_g[l],
            q_norm_g[l], k_norm_g[l], tm, ck)
        attn_out = _attn(q, k, qi, kid, vt, wt, attn_out_g[l], tq, ck)
        x1, h2t = _outproj(pool_out, attn_out, x, w_out[l], g1, sh2, sc2, norm2_g[l], tm)
        x = _peer(h2t, x1.reshape(b * s, d), g2, peer_wq[l], peer_subkeys[l],
                  peer_u[l], peer_v[l], s, tp, eb).reshape(b, s, d)
    return x
```

```python
import functools
import math

import ml_dtypes
import numpy as np

import jax
import jax.numpy as jnp
from jax import lax
from jax.experimental import pallas as pl
from jax.experimental.pallas import tpu as pltpu

F32 = jnp.float32
BF16 = jnp.bfloat16
I32 = jnp.int32

CHUNK = 64
EPS = 1e-6
POOL_WINDOWS = (2, 4, 8, 16)
POOL_GROUP = 128
POOL_WIDTH = 512
ATTN_WIDTH = 512
HEAD_DIM = 64
N_HEADS = 8
IDX_HEADS = 8
IDX_DIM = 64
TOPK_MAX = 256
PEER_HEADS = 8
PEER_KEYS = 128
PEER_TOPK = 16
NEG = -1e30

LANES = 128
POOL_HALO = 16
LOG2E = math.log2(math.e)
M_INIT = 0.5 * NEG
N_FEAT = 9

NT_DIMS = (((1,), (1,)), ((), ()))


def _dot(a, b):
    return jnp.dot(a, b, preferred_element_type=F32)


def _dot_nt(a, b):
    return lax.dot_general(a, b, NT_DIMS, preferred_element_type=F32)


def _f32_key(v):
    b = int(np.float32(v).view(np.int32))
    return b ^ 0x7FFFFFFF if b < 0 else b


def _bf16_split3(c):
    c = np.float32(c)
    p1 = np.float32(c.astype(ml_dtypes.bfloat16))
    r = np.float32(c - p1)
    p2 = np.float32(r.astype(ml_dtypes.bfloat16))
    p3 = np.float32(np.float32(r - p2).astype(ml_dtypes.bfloat16))
    return float(p1), float(p2), float(p3)


_SLOPE_L2E = tuple(float(np.float32(2.0 ** (-8.0 * (h + 1) / N_HEADS) * LOG2E)) for h in range(N_HEADS))


def _ada_kernel(c_ref, w_ref, b_ref, o_ref):
    c = c_ref[...]
    cond = c * jax.nn.sigmoid(c)
    o_ref[...] = jnp.dot(cond, w_ref[...], preferred_element_type=F32,
                         precision=lax.Precision.HIGHEST) + b_ref[...]


def _ada(c, w, b):
    bsz, d = c.shape
    n = w.shape[1]
    tn = 1024
    return pl.pallas_call(
        _ada_kernel,
        grid=(n // tn,),
        in_specs=[pl.BlockSpec((bsz, d), lambda j: (0, 0)),
                  pl.BlockSpec((d, tn), lambda j: (0, j)),
                  pl.BlockSpec((1, tn), lambda j: (0, j))],
        out_specs=pl.BlockSpec((bsz, tn), lambda j: (0, j)),
        out_shape=jax.ShapeDtypeStruct((bsz, n), F32),
        name="ada",
    )(c, w, b.reshape(1, n))


def _group_sumsq(t, ones_bd):
    sq = t * t
    hi = sq.astype(BF16)
    lo = (sq - hi.astype(F32)).astype(BF16)
    return _dot(hi, ones_bd) + _dot(lo, ones_bd)


def _inproj_kernel(x_ref, sh_ref, sc_ref, g_ref, wnat_ref, wki_ref, wvt_ref, wwt_ref, ones_ref,
                   qg_ref, kg_ref, poolw_ref, pscale_ref, pg_ref,
                   q_ref, k_ref, qi_ref, kid_ref, vt_ref, wt_ref, pool_ref,
                   ubuf, *, tm, kc):
    i = pl.program_id(1)
    x = x_ref[0]
    ms = jnp.mean(x * x, axis=-1, keepdims=True)
    h = (x * lax.rsqrt(ms + EPS) * g_ref[...]) * (1.0 + sc_ref[0]) + sh_ref[0]
    hb = h.astype(BF16)

    proj = _dot(hb, wnat_ref[...])
    u = proj[:, 0:512]
    q = proj[:, 512:1024]
    k = proj[:, 1024:1536]
    qi = proj[:, 1536:2048]

    ones_bd = ones_ref[...]
    qn = q * lax.rsqrt(_group_sumsq(q, ones_bd) * (1.0 / HEAD_DIM) + EPS) * qg_ref[...]
    kn = k * lax.rsqrt(_group_sumsq(k, ones_bd) * (1.0 / HEAD_DIM) + EPS) * kg_ref[...]
    q_ref[0] = qn.astype(BF16)
    k_ref[0] = kn.astype(BF16)
    qi_ref[0] = (qi * (IDX_DIM ** -0.5)).astype(BF16)
    kid_ref[0] = _dot(hb, wki_ref[...]).astype(BF16)
    vt = _dot_nt(wvt_ref[...], hb).astype(BF16)
    for ci in range(tm // kc):
        vt_ref[0, ci] = vt[:, ci * kc:(ci + 1) * kc]
    wt = _dot_nt(wwt_ref[...], hb)
    wt_ref[0] = wt[0:IDX_HEADS, :] * (IDX_HEADS ** -0.5)

    @pl.when(i == 0)
    def _():
        ubuf[0:POOL_HALO, :] = jnp.zeros((POOL_HALO, POOL_WIDTH), F32)

    ubuf[POOL_HALO:POOL_HALO + tm, :] = u
    tpos = i * tm + lax.broadcasted_iota(I32, (tm, POOL_GROUP), 0)
    parts = []
    for gi, win in enumerate(POOL_WINDOWS):
        lo_l, hi_l = gi * POOL_GROUP, (gi + 1) * POOL_GROUP
        ws = ubuf[POOL_HALO:POOL_HALO + tm, lo_l:hi_l]
        for j in range(1, win):
            ws = ws + ubuf[POOL_HALO - j:POOL_HALO - j + tm, lo_l:hi_l]
        cnt = jnp.minimum(tpos + 1, win).astype(F32)
        pooled = ws / cnt - u[:, lo_l:hi_l]
        parts.append(_dot(pooled.astype(BF16), poolw_ref[gi]))
    mixed = jnp.concatenate(parts, axis=-1) * pscale_ref[...]
    pms = jnp.mean(mixed * mixed, axis=-1, keepdims=True)
    pool_ref[0] = (mixed * lax.rsqrt(pms + EPS) * pg_ref[...]).astype(BF16)
    ubuf[0:POOL_HALO, :] = ubuf[tm:tm + POOL_HALO, :]


def _inproj(x, sh1, sc1, norm_g, w_in, pool_w, pool_scale, pool_out_g, q_norm_g, k_norm_g, tm, kc):
    b, s, d = x.shape
    nt = s // tm
    w_u, w_q, w_k, w_v, w_qi, w_ki, w_wi = jnp.split(
        w_in, (512, 1024, 1536, 2048, 2560, 2624), axis=-1)
    wnat = jnp.concatenate([w_u, w_q, w_k, w_qi], axis=-1).astype(BF16)
    wki = jnp.concatenate([w_ki, w_ki], axis=-1).astype(BF16)
    wvt = w_v.T.astype(BF16)
    wwt = jnp.concatenate([w_wi.T, jnp.zeros((16 - IDX_HEADS, d), F32)], axis=0).astype(BF16)
    gid = jnp.arange(ATTN_WIDTH) // HEAD_DIM
    ones_bd = (gid[:, None] == gid[None, :]).astype(BF16)
    qg = (jnp.tile(q_norm_g, N_HEADS) * (HEAD_DIM ** -0.5 * LOG2E)).reshape(1, ATTN_WIDTH)
    kg = jnp.tile(k_norm_g, N_HEADS).reshape(1, ATTN_WIDTH)

    full = lambda shape: pl.BlockSpec(shape, lambda bi, ti: (0,) * len(shape))
    tok = lambda w: pl.BlockSpec((1, tm, w), lambda bi, ti: (bi, ti, 0))
    per_b = pl.BlockSpec((1, 1, d), lambda bi, ti: (bi, 0, 0))
    out_shapes = (
        jax.ShapeDtypeStruct((b, s, ATTN_WIDTH), BF16),
        jax.ShapeDtypeStruct((b, s, ATTN_WIDTH), BF16),
        jax.ShapeDtypeStruct((b, s, IDX_HEADS * IDX_DIM), BF16),
        jax.ShapeDtypeStruct((b, s, 2 * IDX_DIM), BF16),
        jax.ShapeDtypeStruct((b, s // kc, ATTN_WIDTH, kc), BF16),
        jax.ShapeDtypeStruct((b, IDX_HEADS, s), F32),
        jax.ShapeDtypeStruct((b, s, POOL_WIDTH), BF16),
    )
    out_specs = (
        tok(ATTN_WIDTH), tok(ATTN_WIDTH), tok(IDX_HEADS * IDX_DIM), tok(2 * IDX_DIM),
        pl.BlockSpec((1, tm // kc, ATTN_WIDTH, kc), lambda bi, ti: (bi, ti, 0, 0)),
        pl.BlockSpec((1, IDX_HEADS, tm), lambda bi, ti: (bi, 0, ti)),
        tok(POOL_WIDTH),
    )
    return pl.pallas_call(
        functools.partial(_inproj_kernel, tm=tm, kc=kc),
        grid=(b, nt),
        in_specs=[tok(d), per_b, per_b, full((1, d)), full((d, 2048)), full((d, 2 * IDX_DIM)),
                  full((ATTN_WIDTH, d)), full((16, d)), full((ATTN_WIDTH, ATTN_WIDTH)),
                  full((1, ATTN_WIDTH)), full((1, ATTN_WIDTH)),
                  full((len(POOL_WINDOWS), POOL_GROUP, POOL_GROUP)),
                  full((1, POOL_WIDTH)), full((1, POOL_WIDTH))],
        out_specs=out_specs,
        out_shape=out_shapes,
        scratch_shapes=[pltpu.VMEM((tm + POOL_HALO, POOL_WIDTH), F32)],
        compiler_params=pltpu.CompilerParams(
            dimension_semantics=("arbitrary", "arbitrary"), vmem_limit_bytes=48 * 1024 * 1024),
        name="inproj",
    )(x, sh1, sc1, norm_g.reshape(1, d), wnat, wki, wvt, wwt, ones_bd, qg, kg,
      pool_w.astype(BF16), pool_scale.reshape(1, -1), pool_out_g.reshape(1, -1))


def _attn_kernel(q_ref, qi_ref, wt_ref, kid_ref, k_ref, kf_ref, vt_ref, aog_ref, o_ref,
                 key_ref, acc_ref, m_ref, l_ref, qim_ref, qmf_ref, cut_ref, lg_ref,
                 *, tq, ck, seq, topk):
    j = pl.program_id(1)
    nck = ((j + 1) * tq + ck - 1) // ck
    qpos = j * tq + lax.broadcasted_iota(I32, (1, tq), 1)
    qchunk = qpos // CHUNK
    row_i = lax.broadcasted_iota(I32, (ck, tq), 0)

    lane = lax.broadcasted_iota(I32, (tq, LANES), 1)
    tcol = (j * tq + lax.broadcasted_iota(I32, (tq, LANES), 0)).astype(F32)
    for h in range(N_HEADS):
        p = h // 2
        keep = (lane < HEAD_DIM) if h % 2 == 0 else (lane >= HEAD_DIM)
        qi_slab = qi_ref[0, :, p * LANES:(p + 1) * LANES]
        q_slab = q_ref[0, :, p * LANES:(p + 1) * LANES]
        qim_ref[h * tq:(h + 1) * tq, :] = jnp.where(keep, qi_slab, jnp.zeros_like(qi_slab))
        qmf_ref[h, :, 0:LANES] = jnp.where(keep, q_slab, jnp.zeros_like(q_slab))
        c1, c2, c3 = _bf16_split3(_SLOPE_L2E[h])
        u = -_SLOPE_L2E[h] * tcol
        u1 = u.astype(BF16).astype(F32)
        r = u - u1
        u2 = r.astype(BF16).astype(F32)
        u3 = (r - u2).astype(BF16).astype(F32)
        feat = jnp.zeros((tq, LANES), F32)
        for li, val in enumerate((64.0 * c1, 64.0 * c2, 64.0 * c3, c1, c2, c3, u1, u2, u3)):
            feat = jnp.where(lane == li, val, feat)
        qmf_ref[h, :, LANES:2 * LANES] = feat.astype(BF16)

    def scores(c, last):
        r0 = pl.multiple_of(c * ck, ck)
        d = _dot_nt(kid_ref[0, pl.ds(r0, ck), :], qim_ref[...])
        acc = jnp.zeros((ck, tq), F32)
        for h in range(IDX_HEADS):
            acc = acc + wt_ref[0, h:h + 1, :] * jnp.maximum(d[:, h * tq:(h + 1) * tq], 0.0)
        if last:
            acc = jnp.where(((r0 + row_i) // CHUNK) <= qchunk, acc, NEG)
        bits = pltpu.bitcast(acc, I32)
        key_ref[pl.ds(r0, ck), :] = jnp.where(bits < 0, bits ^ 0x7FFFFFFF, bits)

    def scores_body(c, carry):
        scores(c, False)
        return carry

    lax.fori_loop(0, nck - 1, scores_body, 0)
    scores(nck - 1, True)

    def count(pred):
        def body(c, c8):
            r0 = pl.multiple_of(c * ck, ck)
            m = pred(key_ref[pl.ds(r0, ck), :], r0).astype(I32)
            return c8 + m.reshape(ck // 8, 8, tq).sum(axis=0)
        c8 = lax.fori_loop(0, nck, body, jnp.zeros((8, tq), I32))
        return c8.sum(axis=0, keepdims=True)

    c_nonneg = count(lambda kk, r0: kk >= 0)
    lo0 = jnp.where(c_nonneg >= topk, 0, -(2 ** 31)).astype(I32)

    def bit_body(b, lo):
        trial = lo + jnp.left_shift(jnp.int32(1), 30 - b)
        c = count(lambda kk, r0: kk >= trial)
        return jnp.where(c >= topk, trial, lo)

    lo = lax.fori_loop(0, 31, bit_body, lo0)

    c_gt = count(lambda kk, r0: kk > lo)
    c_ge = count(lambda kk, r0: kk >= lo)
    need = topk - c_gt
    cut_ref[...] = jnp.full((1, tq), seq, I32)

    @pl.when(jnp.max(c_ge) > topk)
    def _():
        nbits = max(1, (seq - 1).bit_length())

        def idx_body(b, jv):
            trial = jv + jnp.left_shift(jnp.int32(1), nbits - 1 - b)
            c = count(lambda kk, r0: (kk == lo)
                      & ((r0 + lax.broadcasted_iota(I32, (ck, tq), 0)) < trial))
            return jnp.where(c < need, trial, jv)

        jv = lax.fori_loop(0, nbits, idx_body, jnp.zeros((1, tq), I32))
        cut_ref[...] = jv + 1

    cut = cut_ref[...]

    m_ref[...] = jnp.full((N_HEADS, 1, tq), M_INIT, F32)
    l_ref[...] = jnp.zeros((N_HEADS, 1, tq), F32)
    acc_ref[...] = jnp.zeros((ATTN_WIDTH, tq), F32)

    def attend(c, last):
        r0 = pl.multiple_of(c * ck, ck)
        kk = key_ref[pl.ds(r0, ck), :]
        kf = kf_ref[pl.ds(r0, ck), :]
        kch = k_ref[0, pl.ds(r0, ck), :]
        vtc = vt_ref[0, c]
        kpos = r0 + row_i
        sel = (kk > lo) | ((kk == lo) & (kpos < cut))
        if last:
            sel = sel & ((kpos // CHUNK) <= qchunk)
            ahead = jnp.maximum(kpos - qpos, 0).astype(F32)
        pen = jnp.where(sel, 0.0, NEG)
        for h in range(N_HEADS):
            p = h // 2
            ks = kch[:, p * LANES:(p + 1) * LANES]
            lg = _dot_nt(jnp.concatenate([ks, kf], axis=1), qmf_ref[h]) + pen
            if last:
                lg = lg - (2.0 * _SLOPE_L2E[h]) * ahead
            lg_ref[h] = lg
        for h in range(N_HEADS):
            m_old = m_ref[h]
            m_new = jnp.maximum(m_old, jnp.max(lg_ref[h], axis=0, keepdims=True))
            alpha = jnp.exp2(m_old - m_new)
            pr = jnp.exp2(lg_ref[h] - m_new)
            l_ref[h] = alpha * l_ref[h] + jnp.sum(pr, axis=0, keepdims=True)
            m_ref[h] = m_new
            rows = slice(h * HEAD_DIM, (h + 1) * HEAD_DIM)
            acc_ref[rows, :] = alpha * acc_ref[rows, :] + _dot(vtc[rows, :], pr.astype(BF16))

    def attend_body(c, carry):
        attend(c, False)
        return carry

    lax.fori_loop(0, nck - 1, attend_body, 0)
    attend(nck - 1, True)

    outs = []
    for h in range(N_HEADS):
        rows = slice(h * HEAD_DIM, (h + 1) * HEAD_DIM)
        o = acc_ref[rows, :] / l_ref[h]
        oms = jnp.mean(o * o, axis=0, keepdims=True)
        outs.append(o * lax.rsqrt(oms + EPS))
    ot = jnp.concatenate(outs, axis=0)
    o_ref[0] = (ot.T * aog_ref[...]).astype(BF16)


def _attn(q, k, qi, kid, vt, wt, attn_out_g, tq, ck):
    b, s, _ = q.shape
    topk = min(TOPK_MAX, s // 4)
    assert tq == LANES and ck % tq == 0 and s % ck == 0 and s // CHUNK <= 256
    pos = jnp.arange(s)
    feats = [pos // CHUNK] * 3 + [pos % CHUNK] * 3 + [jnp.ones_like(pos)] * 3
    kf = jnp.zeros((s, LANES), F32).at[:, 0:N_FEAT].set(
        jnp.stack(feats, axis=1).astype(F32)).astype(BF16)
    blk = lambda w: pl.BlockSpec((1, tq, w), lambda bi, ji: (bi, ji, 0))
    whole = lambda w: pl.BlockSpec((1, s, w), lambda bi, ji: (bi, 0, 0))
    return pl.pallas_call(
        functools.partial(_attn_kernel, tq=tq, ck=ck, seq=s, topk=topk),
        grid=(b, s // tq),
        in_specs=[blk(ATTN_WIDTH), blk(IDX_HEADS * IDX_DIM),
                  pl.BlockSpec((1, IDX_HEADS, tq), lambda bi, ji: (bi, 0, ji)),
                  whole(2 * IDX_DIM), whole(ATTN_WIDTH),
                  pl.BlockSpec((s, LANES), lambda bi, ji: (0, 0)),
                  pl.BlockSpec((1, s // ck, ATTN_WIDTH, ck), lambda bi, ji: (bi, 0, 0, 0)),
                  pl.BlockSpec((1, ATTN_WIDTH), lambda bi, ji: (0, 0))],
        out_specs=blk(ATTN_WIDTH),
        out_shape=jax.ShapeDtypeStruct((b, s, ATTN_WIDTH), BF16),
        scratch_shapes=[pltpu.VMEM((s, tq), I32),
                        pltpu.VMEM((ATTN_WIDTH, tq), F32),
                        pltpu.VMEM((N_HEADS, 1, tq), F32),
                        pltpu.VMEM((N_HEADS, 1, tq), F32),
                        pltpu.VMEM((IDX_HEADS * tq, LANES), BF16),
                        pltpu.VMEM((N_HEADS, tq, 2 * LANES), BF16),
                        pltpu.VMEM((1, tq), I32),
                        pltpu.VMEM((N_HEADS, ck, tq), F32)],
        compiler_params=pltpu.CompilerParams(
            dimension_semantics=("arbitrary", "arbitrary"), vmem_limit_bytes=56 * 1024 * 1024),
        name="attn",
    )(q, qi, wt, kid, k, kf, vt, attn_out_g.reshape(1, ATTN_WIDTH))


def _outproj_kernel(pool_ref, attn_ref, x_ref, wo_ref, g1_ref, sh_ref, sc_ref, ng_ref,
                    x1_ref, h2t_ref):
    mixed = _dot(pool_ref[0], wo_ref[0:POOL_WIDTH, :]) + _dot(attn_ref[0], wo_ref[POOL_WIDTH:, :])
    x1 = x_ref[0] + g1_ref[0] * mixed
    x1_ref[0] = x1
    ms = jnp.mean(x1 * x1, axis=-1, keepdims=True)
    h2 = (x1 * lax.rsqrt(ms + EPS) * ng_ref[...]) * (1.0 + sc_ref[0]) + sh_ref[0]
    h2t_ref[...] = h2.T.astype(BF16)


def _outproj(pool_out, attn_out, x, w_out, g1, sh2, sc2, norm_g, tm):
    b, s, d = x.shape
    nt = s // tm
    tok = lambda w: pl.BlockSpec((1, tm, w), lambda bi, ti: (bi, ti, 0))
    per_b = pl.BlockSpec((1, 1, d), lambda bi, ti: (bi, 0, 0))
    return pl.pallas_call(
        _outproj_kernel,
        grid=(b, nt),
        in_specs=[tok(POOL_WIDTH), tok(ATTN_WIDTH), tok(d),
                  pl.BlockSpec((POOL_WIDTH + ATTN_WIDTH, d), lambda bi, ti: (0, 0)),
                  per_b, per_b, per_b, pl.BlockSpec((1, d), lambda bi, ti: (0, 0))],
        out_specs=(tok(d), pl.BlockSpec((d, tm), lambda bi, ti: (0, bi * nt + ti))),
        out_shape=(jax.ShapeDtypeStruct((b, s, d), F32), jax.ShapeDtypeStruct((d, b * s), BF16)),
        compiler_params=pltpu.CompilerParams(dimension_semantics=("arbitrary", "arbitrary")),
        name="outproj",
    )(pool_out, attn_out, x, w_out.astype(BF16), g1, sh2, sc2, norm_g.reshape(1, d))


def _batcher_pairs(n):
    pairs = []

    def merge(lo, cnt, r):
        step = r * 2
        if step < cnt:
            merge(lo, cnt, step)
            merge(lo + r, cnt, step)
            for i in range(lo + r, lo + cnt - r, step):
                pairs.append((i, i + r))
        else:
            pairs.append((lo, lo + r))

    def sort(lo, cnt):
        if cnt > 1:
            m = cnt // 2
            sort(lo, m)
            sort(lo + m, m)
            merge(lo, cnt, 1)

    sort(0, n)
    return tuple(pairs)


_SORT16 = _batcher_pairs(PEER_TOPK)


def _top16_desc(x):
    x = list(x)
    for i, j in _SORT16:
        x[i], x[j] = jnp.maximum(x[i], x[j]), jnp.minimum(x[i], x[j])
    for shift in (4, 2, 1):
        y = [jnp.maximum(x[k], pltpu.roll(x[PEER_TOPK - 1 - k], shift, 0)) for k in range(PEER_TOPK)]
        for d in (8, 4, 2, 1):
            for k in range(PEER_TOPK):
                if k & d == 0:
                    y[k], y[k + d] = jnp.maximum(y[k], y[k + d]), jnp.minimum(y[k], y[k + d])
        x = y
    return x


def _peer_kernel(h2t_ref, x1_ref, g2_ref, wqt_ref, sk_ref, u_ref, vt_ref, o_ref,
                 s1_ref, s2_ref, thr_ref, mz_ref, p_ref, acc_ref, s1b_ref, *, tm, eb, ne):
    e = pl.program_id(1)
    nsub = eb // PEER_KEYS
    ntl = tm // LANES

    @pl.when(e == 0)
    def _prologue():
        h2t = h2t_ref[...]
        sub = lax.broadcasted_iota(I32, (8, tm), 0)
        ninf = jnp.full((8, tm), -jnp.inf, F32)

        def compact(t, base):
            out = t[base + 7]
            for r in range(6, -1, -1):
                out = jnp.where(sub == r, t[base + r], out)
            return out

        def split8(st):
            return [st[k * 8:(k + 1) * 8, :] for k in range(PEER_KEYS // 8)]

        def head(hd, carry):
            q1 = _dot(wqt_ref[2 * hd], h2t).astype(BF16)
            q2 = _dot(wqt_ref[2 * hd + 1], h2t).astype(BF16)
            st1 = _dot(sk_ref[2 * hd], q1) * LOG2E
            st2 = _dot(sk_ref[2 * hd + 1], q2) * LOG2E
            s1_ref[hd] = st1
            for tl in range(ntl):
                s2_ref[hd, tl] = st2[:, tl * LANES:(tl + 1) * LANES]
            t1 = _top16_desc(split8(st1))
            t2 = _top16_desc(split8(st2))
            a2lo, a2hi, a1hi = compact(t2, 0), compact(t2, 8), compact(t1, 8)
            cands = [t1[0] + a2lo, t1[0] + a2hi, t1[1] + a2lo]
            for r1 in range(2, 8):
                cands.append(jnp.where(sub < PEER_TOPK // (r1 + 1), t1[r1] + a2lo, ninf))
            cands.append(a1hi + t2[0])
            thr = _top16_desc(cands + [ninf] * (PEER_TOPK - len(cands)))[PEER_TOPK - 1]
            m = t1[0] + t2[0]
            z8 = jnp.zeros((8, tm), F32)
            for c in cands:
                z8 = z8 + jnp.where(c >= thr, jnp.exp2(c - m), 0.0)
            z = jnp.sum(z8, axis=0, keepdims=True)
            thr_ref[pl.ds(hd, 1), :] = thr[0:1, :]
            mz_ref[pl.ds(hd, 1), :] = m[0:1, :] + jnp.log2(z)
            return carry

        lax.fori_loop(0, PEER_HEADS, head, 0)
        acc_ref[...] = jnp.zeros_like(acc_ref)

    i0 = pl.multiple_of(e * nsub, nsub)
    s1rows = [s1_ref[hd, pl.ds(i0, nsub), :] for hd in range(PEER_HEADS)]
    for hd in range(PEER_HEADS):
        s1b_ref[hd] = s1rows[hd]
    for ii in range(nsub):
        rows = slice(ii * PEER_KEYS, (ii + 1) * PEER_KEYS)
        a_rows = _dot(u_ref[rows, :], h2t_ref[...])
        for tl in range(ntl):
            cols = slice(tl * LANES, (tl + 1) * LANES)
            w = jnp.zeros((PEER_KEYS, LANES), F32)
            for hd in range(PEER_HEADS):
                sm = s1b_ref[hd, ii:ii + 1, cols] + s2_ref[hd, tl]
                gate = jnp.exp2(sm - mz_ref[hd:hd + 1, cols])
                w = w + jnp.where(sm >= thr_ref[hd:hd + 1, cols], gate, 0.0)
            a = a_rows[:, cols]
            gl = 0.5 * a * (1.0 + lax.erf(a * (1.0 / math.sqrt(2.0))))
            p_ref[rows, cols] = (gl * w).astype(BF16)
    acc_ref[...] += _dot(vt_ref[...], p_ref[...])

    @pl.when(e == ne - 1)
    def _epilogue():
        o_ref[...] = x1_ref[...] + g2_ref[0] * acc_ref[...].T


def _peer(h2t, x1, g2, wq, sub_keys, u_tab, v_tab, seq, tm, eb):
    d, t = h2t.shape
    ne = u_tab.shape[0] // eb
    half = PEER_KEYS
    wqt = wq.T.reshape(2 * PEER_HEADS, half, d).astype(BF16)
    sk = sub_keys.reshape(2 * PEER_HEADS, PEER_KEYS, half).astype(BF16)
    ub = u_tab.astype(BF16)
    vtb = v_tab.T.astype(BF16)
    hs = (PEER_HEADS, PEER_KEYS, tm)
    return pl.pallas_call(
        functools.partial(_peer_kernel, tm=tm, eb=eb, ne=ne),
        grid=(t // tm, ne),
        in_specs=[pl.BlockSpec((d, tm), lambda i, e: (0, i)),
                  pl.BlockSpec((tm, d), lambda i, e: (i, 0)),
                  pl.BlockSpec((1, 1, d), lambda i, e: ((i * tm) // seq, 0, 0)),
                  pl.BlockSpec((2 * PEER_HEADS, half, d), lambda i, e: (0, 0, 0)),
                  pl.BlockSpec((2 * PEER_HEADS, PEER_KEYS, half), lambda i, e: (0, 0, 0)),
                  pl.BlockSpec((eb, d), lambda i, e: (e, 0)),
                  pl.BlockSpec((d, eb), lambda i, e: (0, e))],
        out_specs=pl.BlockSpec((tm, d), lambda i, e: (i, 0)),
        out_shape=jax.ShapeDtypeStruct((t, d), F32),
        scratch_shapes=[pltpu.VMEM(hs, F32),
                        pltpu.VMEM((PEER_HEADS, tm // LANES, PEER_KEYS, LANES), F32),
                        pltpu.VMEM((PEER_HEADS, tm), F32),
                        pltpu.VMEM((PEER_HEADS, tm), F32),
                        pltpu.VMEM((eb, tm), BF16),
                        pltpu.VMEM((d, tm), F32),
                        pltpu.VMEM((PEER_HEADS, eb // PEER_KEYS, tm), F32)],
        compiler_params=pltpu.CompilerParams(
            dimension_semantics=("arbitrary", "arbitrary"), vmem_limit_bytes=58 * 1024 * 1024),
        name="peer",
    )(h2t, x1, g2, wqt, sk, ub, vtb)


def _tile(n, pref):
    t = min(pref, n)
    assert n % t == 0
    return t


def kernel(x, c, ada_w, ada_b, norm1_g, w_in, pool_w, pool_scale, pool_out_g, q_norm_g, k_norm_g,
           attn_out_g, w_out, norm2_g, peer_wq, peer_subkeys, peer_u, peer_v):
    b, s, d = x.shape
    depth = ada_w.shape[0]
    tm = _tile(s, 512)
    tq = LANES
    ck = _tile(tm, 512)
    tp = _tile(s, 512)
    eb = 1024
    for l in range(depth):
        mod = _ada(c, ada_w[l], ada_b[l]).reshape(b, 1, 6 * d)
        sh1, sc1, g1, sh2, sc2, g2 = jnp.split(mod, 6, axis=-1)
        q, k, qi, kid, vt, wt, pool_out = _inproj(
            x, sh1, sc1, norm1_g[l], w_in[l], pool_w[l], pool_scale[l], pool_out_g[l],
            q_norm_g[l], k_norm_g[l], tm, ck)
        attn_out = _attn(q, k, qi, kid, vt, wt, attn_out_g[l], tq, ck)
        x1, h2t = _outproj(pool_out, attn_out, x, w_out[l], g1, sh2, sc2, norm2_g[l], tm)
        x = _peer(h2t, x1.reshape(b * s, d), g2, peer_wq[l], peer_subkeys[l],
                  peer_u[l], peer_v[l], s, tp, eb).reshape(b, s, d)
    return x
```

```python
import functools
import math

import ml_dtypes
import numpy as np

import jax
import jax.numpy as jnp
from jax import lax
from jax.experimental import pallas as pl
from jax.experimental.pallas import tpu as pltpu

F32 = jnp.float32
BF16 = jnp.bfloat16
I32 = jnp.int32

CHUNK = 64
EPS = 1e-6
POOL_WINDOWS = (2, 4, 8, 16)
POOL_GROUP = 128
POOL_WIDTH = 512
ATTN_WIDTH = 512
HEAD_DIM = 64
N_HEADS = 8
IDX_HEADS = 8
IDX_DIM = 64
TOPK_MAX = 256
PEER_HEADS = 8
PEER_KEYS = 128
PEER_TOPK = 16
NEG = -1e30

LANES = 128
POOL_HALO = 16
LOG2E = math.log2(math.e)
M_INIT = 0.5 * NEG
N_FEAT = 9

NT_DIMS = (((1,), (1,)), ((), ()))


def _dot(a, b):
    return jnp.dot(a, b, preferred_element_type=F32)


def _dot_nt(a, b):
    return lax.dot_general(a, b, NT_DIMS, preferred_element_type=F32)


def _f32_key(v):
    b = int(np.float32(v).view(np.int32))
    return b ^ 0x7FFFFFFF if b < 0 else b


def _bf16_split3(c):
    c = np.float32(c)
    p1 = np.float32(c.astype(ml_dtypes.bfloat16))
    r = np.float32(c - p1)
    p2 = np.float32(r.astype(ml_dtypes.bfloat16))
    p3 = np.float32(np.float32(r - p2).astype(ml_dtypes.bfloat16))
    return float(p1), float(p2), float(p3)


_SLOPE_L2E = tuple(float(np.float32(2.0 ** (-8.0 * (h + 1) / N_HEADS) * LOG2E)) for h in range(N_HEADS))


def _ada_kernel(c_ref, w_ref, b_ref, o_ref):
    c = c_ref[...]
    cond = c * jax.nn.sigmoid(c)
    o_ref[...] = jnp.dot(cond, w_ref[...], preferred_element_type=F32,
                         precision=lax.Precision.HIGHEST) + b_ref[...]


def _ada(c, w, b):
    bsz, d = c.shape
    n = w.shape[1]
    tn = 1024
    return pl.pallas_call(
        _ada_kernel,
        grid=(n // tn,),
        in_specs=[pl.BlockSpec((bsz, d), lambda j: (0, 0)),
                  pl.BlockSpec((d, tn), lambda j: (0, j)),
                  pl.BlockSpec((1, tn), lambda j: (0, j))],
        out_specs=pl.BlockSpec((bsz, tn), lambda j: (0, j)),
        out_shape=jax.ShapeDtypeStruct((bsz, n), F32),
        name="ada",
    )(c, w, b.reshape(1, n))


def _group_sumsq(t, ones_bd):
    sq = t * t
    hi = sq.astype(BF16)
    lo = (sq - hi.astype(F32)).astype(BF16)
    return _dot(hi, ones_bd) + _dot(lo, ones_bd)


def _inproj_kernel(x_ref, sh_ref, sc_ref, g_ref, wnat_ref, wki_ref, wvt_ref, wwt_ref, ones_ref,
                   qg_ref, kg_ref, poolw_ref, pscale_ref, pg_ref,
                   q_ref, k_ref, qi_ref, kid_ref, vt_ref, wt_ref, pool_ref,
                   ubuf, *, tm, kc):
    i = pl.program_id(1)
    x = x_ref[0]
    ms = jnp.mean(x * x, axis=-1, keepdims=True)
    h = (x * lax.rsqrt(ms + EPS) * g_ref[...]) * (1.0 + sc_ref[0]) + sh_ref[0]
    hb = h.astype(BF16)

    proj = _dot(hb, wnat_ref[...])
    u = proj[:, 0:512]
    q = proj[:, 512:1024]
    k = proj[:, 1024:1536]
    qi = proj[:, 1536:2048]

    ones_bd = ones_ref[...]
    qn = q * lax.rsqrt(_group_sumsq(q, ones_bd) * (1.0 / HEAD_DIM) + EPS) * qg_ref[...]
    kn = k * lax.rsqrt(_group_sumsq(k, ones_bd) * (1.0 / HEAD_DIM) + EPS) * kg_ref[...]
    q_ref[0] = qn.astype(BF16)
    k_ref[0] = kn.astype(BF16)
    qi_ref[0] = (qi * (IDX_DIM ** -0.5)).astype(BF16)
    kid_ref[0] = _dot(hb, wki_ref[...]).astype(BF16)
    vt = _dot_nt(wvt_ref[...], hb).astype(BF16)
    for ci in range(tm // kc):
        vt_ref[0, ci] = vt[:, ci * kc:(ci + 1) * kc]
    wt = _dot_nt(wwt_ref[...], hb)
    wt_ref[0] = wt[0:IDX_HEADS, :] * (IDX_HEADS ** -0.5)

    @pl.when(i == 0)
    def _():
        ubuf[0:POOL_HALO, :] = jnp.zeros((POOL_HALO, POOL_WIDTH), F32)

    ubuf[POOL_HALO:POOL_HALO + tm, :] = u
    tpos = i * tm + lax.broadcasted_iota(I32, (tm, POOL_GROUP), 0)
    parts = []
    for gi, win in enumerate(POOL_WINDOWS):
        lo_l, hi_l = gi * POOL_GROUP, (gi + 1) * POOL_GROUP
        ws = ubuf[POOL_HALO:POOL_HALO + tm, lo_l:hi_l]
        for j in range(1, win):
            ws = ws + ubuf[POOL_HALO - j:POOL_HALO - j + tm, lo_l:hi_l]
        cnt = jnp.minimum(tpos + 1, win).astype(F32)
        pooled = ws / cnt - u[:, lo_l:hi_l]
        parts.append(_dot(pooled.astype(BF16), poolw_ref[gi]))
    mixed = jnp.concatenate(parts, axis=-1) * pscale_ref[...]
    pms = jnp.mean(mixed * mixed, axis=-1, keepdims=True)
    pool_ref[0] = (mixed * lax.rsqrt(pms + EPS) * pg_ref[...]).astype(BF16)
    ubuf[0:POOL_HALO, :] = ubuf[tm:tm + POOL_HALO, :]


def _inproj(x, sh1, sc1, norm_g, w_in, pool_w, pool_scale, pool_out_g, q_norm_g, k_norm_g, tm, kc):
    b, s, d = x.shape
    nt = s // tm
    w_u, w_q, w_k, w_v, w_qi, w_ki, w_wi = jnp.split(
        w_in, (512, 1024, 1536, 2048, 2560, 2624), axis=-1)
    wnat = jnp.concatenate([w_u, w_q, w_k, w_qi], axis=-1).astype(BF16)
    wki = jnp.concatenate([w_ki, w_ki], axis=-1).astype(BF16)
    wvt = w_v.T.astype(BF16)
    wwt = jnp.concatenate([w_wi.T, jnp.zeros((16 - IDX_HEADS, d), F32)], axis=0).astype(BF16)
    gid = jnp.arange(ATTN_WIDTH) // HEAD_DIM
    ones_bd = (gid[:, None] == gid[None, :]).astype(BF16)
    qg = (jnp.tile(q_norm_g, N_HEADS) * (HEAD_DIM ** -0.5 * LOG2E)).reshape(1, ATTN_WIDTH)
    kg = jnp.tile(k_norm_g, N_HEADS).reshape(1, ATTN_WIDTH)

    full = lambda shape: pl.BlockSpec(shape, lambda bi, ti: (0,) * len(shape))
    tok = lambda w: pl.BlockSpec((1, tm, w), lambda bi, ti: (bi, ti, 0))
    per_b = pl.BlockSpec((1, 1, d), lambda bi, ti: (bi, 0, 0))
    out_shapes = (
        jax.ShapeDtypeStruct((b, s, ATTN_WIDTH), BF16),
        jax.ShapeDtypeStruct((b, s, ATTN_WIDTH), BF16),
        jax.ShapeDtypeStruct((b, s, IDX_HEADS * IDX_DIM), BF16),
        jax.ShapeDtypeStruct((b, s, 2 * IDX_DIM), BF16),
        jax.ShapeDtypeStruct((b, s // kc, ATTN_WIDTH, kc), BF16),
        jax.ShapeDtypeStruct((b, IDX_HEADS, s), F32),
        jax.ShapeDtypeStruct((b, s, POOL_WIDTH), BF16),
    )
    out_specs = (
        tok(ATTN_WIDTH), tok(ATTN_WIDTH), tok(IDX_HEADS * IDX_DIM), tok(2 * IDX_DIM),
        pl.BlockSpec((1, tm // kc, ATTN_WIDTH, kc), lambda bi, ti: (bi, ti, 0, 0)),
        pl.BlockSpec((1, IDX_HEADS, tm), lambda bi, ti: (bi, 0, ti)),
        tok(POOL_WIDTH),
    )
    return pl.pallas_call(
        functools.partial(_inproj_kernel, tm=tm, kc=kc),
        grid=(b, nt),
        in_specs=[tok(d), per_b, per_b, full((1, d)), full((d, 2048)), full((d, 2 * IDX_DIM)),
                  full((ATTN_WIDTH, d)), full((16, d)), full((ATTN_WIDTH, ATTN_WIDTH)),
                  full((1, ATTN_WIDTH)), full((1, ATTN_WIDTH)),
                  full((len(POOL_WINDOWS), POOL_GROUP, POOL_GROUP)),
                  full((1, POOL_WIDTH)), full((1, POOL_WIDTH))],
        out_specs=out_specs,
        out_shape=out_shapes,
        scratch_shapes=[pltpu.VMEM((tm + POOL_HALO, POOL_WIDTH), F32)],
        compiler_params=pltpu.CompilerParams(
            dimension_semantics=("arbitrary", "arbitrary"), vmem_limit_bytes=48 * 1024 * 1024),
        name="inproj",
    )(x, sh1, sc1, norm_g.reshape(1, d), wnat, wki, wvt, wwt, ones_bd, qg, kg,
      pool_w.astype(BF16), pool_scale.reshape(1, -1), pool_out_g.reshape(1, -1))


def _attn_kernel(q_ref, qi_ref, wt_ref, kid_ref, k_ref, kf_ref, vt_ref, aog_ref, o_ref,
                 key_ref, acc_ref, m_ref, l_ref, qim_ref, qmf_ref, cut_ref, lg_ref,
                 *, tq, ck, seq, topk):
    j = pl.program_id(1)
    nck = ((j + 1) * tq + ck - 1) // ck
    qpos = j * tq + lax.broadcasted_iota(I32, (1, tq), 1)
    qchunk = qpos // CHUNK
    row_i = lax.broadcasted_iota(I32, (ck, tq), 0)

    lane = lax.broadcasted_iota(I32, (tq, LANES), 1)
    tcol = (j * tq + lax.broadcasted_iota(I32, (tq, LANES), 0)).astype(F32)
    for h in range(N_HEADS):
        p = h // 2
        keep = (lane < HEAD_DIM) if h % 2 == 0 else (lane >= HEAD_DIM)
        qi_slab = qi_ref[0, :, p * LANES:(p + 1) * LANES]
        q_slab = q_ref[0, :, p * LANES:(p + 1) * LANES]
        qim_ref[h * tq:(h + 1) * tq, :] = jnp.where(keep, qi_slab, jnp.zeros_like(qi_slab))
        qmf_ref[h, :, 0:LANES] = jnp.where(keep, q_slab, jnp.zeros_like(q_slab))
        c1, c2, c3 = _bf16_split3(_SLOPE_L2E[h])
        u = -_SLOPE_L2E[h] * tcol
        u1 = u.astype(BF16).astype(F32)
        r = u - u1
        u2 = r.astype(BF16).astype(F32)
        u3 = (r - u2).astype(BF16).astype(F32)
        feat = jnp.zeros((tq, LANES), F32)
        for li, val in enumerate((64.0 * c1, 64.0 * c2, 64.0 * c3, c1, c2, c3, u1, u2, u3)):
            feat = jnp.where(lane == li, val, feat)
        qmf_ref[h, :, LANES:2 * LANES] = feat.astype(BF16)

    def scores(c, last):
        r0 = pl.multiple_of(c * ck, ck)
        d = _dot_nt(kid_ref[0, pl.ds(r0, ck), :], qim_ref[...])
        acc = jnp.zeros((ck, tq), F32)
        for h in range(IDX_HEADS):
            acc = acc + wt_ref[0, h:h + 1, :] * jnp.maximum(d[:, h * tq:(h + 1) * tq], 0.0)
        if last:
            acc = jnp.where(((r0 + row_i) // CHUNK) <= qchunk, acc, NEG)
        bits = pltpu.bitcast(acc, I32)
        key_ref[pl.ds(r0, ck), :] = jnp.where(bits < 0, bits ^ 0x7FFFFFFF, bits)

    def scores_body(c, carry):
        scores(c, False)
        return carry

    lax.fori_loop(0, nck - 1, scores_body, 0)
    scores(nck - 1, True)

    def count(pred):
        def body(c, c8):
            r0 = pl.multiple_of(c * ck, ck)
            m = pred(key_ref[pl.ds(r0, ck), :], r0).astype(I32)
            return c8 + m.reshape(ck // 8, 8, tq).sum(axis=0)
        c8 = lax.fori_loop(0, nck, body, jnp.zeros((8, tq), I32))
        return c8.sum(axis=0, keepdims=True)

    c_nonneg = count(lambda kk, r0: kk >= 0)
    lo0 = jnp.where(c_nonneg >= topk, 0, -(2 ** 31)).astype(I32)

    def bit_body(b, lo):
        trial = lo + jnp.left_shift(jnp.int32(1), 30 - b)
        c = count(lambda kk, r0: kk >= trial)
        return jnp.where(c >= topk, trial, lo)

    lo = lax.fori_loop(0, 31, bit_body, lo0)

    c_gt = count(lambda kk, r0: kk > lo)
    c_ge = count(lambda kk, r0: kk >= lo)
    need = topk - c_gt
    cut_ref[...] = jnp.full((1, tq), seq, I32)

    @pl.when(jnp.max(c_ge) > topk)
    def _():
        nbits = max(1, (seq - 1).bit_length())

        def idx_body(b, jv):
            trial = jv + jnp.left_shift(jnp.int32(1), nbits - 1 - b)
            c = count(lambda kk, r0: (kk == lo)
                      & ((r0 + lax.broadcasted_iota(I32, (ck, tq), 0)) < trial))
            return jnp.where(c < need, trial, jv)

        jv = lax.fori_loop(0, nbits, idx_body, jnp.zeros((1, tq), I32))
        cut_ref[...] = jv + 1

    cut = cut_ref[...]

    m_ref[...] = jnp.full((N_HEADS, 1, tq), M_INIT, F32)
    l_ref[...] = jnp.zeros((N_HEADS, 1, tq), F32)
    acc_ref[...] = jnp.zeros((ATTN_WIDTH, tq), F32)

    def attend(c, last):
        r0 = pl.multiple_of(c * ck, ck)
        kk = key_ref[pl.ds(r0, ck), :]
        kf = kf_ref[pl.ds(r0, ck), :]
        kch = k_ref[0, pl.ds(r0, ck), :]
        vtc = vt_ref[0, c]
        kpos = r0 + row_i
        sel = (kk > lo) | ((kk == lo) & (kpos < cut))
        if last:
            sel = sel & ((kpos // CHUNK) <= qchunk)
            ahead = jnp.maximum(kpos - qpos, 0).astype(F32)
        pen = jnp.where(sel, 0.0, NEG)
        for h in range(N_HEADS):
            p = h // 2
            ks = kch[:, p * LANES:(p + 1) * LANES]
            lg = _dot_nt(jnp.concatenate([ks, kf], axis=1), qmf_ref[h]) + pen
            if last:
                lg = lg - (2.0 * _SLOPE_L2E[h]) * ahead
            lg_ref[h] = lg
        for h in range(N_HEADS):
            m_old = m_ref[h]
            m_new = jnp.maximum(m_old, jnp.max(lg_ref[h], axis=0, keepdims=True))
            alpha = jnp.exp2(m_old - m_new)
            pr = jnp.exp2(lg_ref[h] - m_new)
            l_ref[h] = alpha * l_ref[h] + jnp.sum(pr, axis=0, keepdims=True)
            m_ref[h] = m_new
            rows = slice(h * HEAD_DIM, (h + 1) * HEAD_DIM)
            acc_ref[rows, :] = alpha * acc_ref[rows, :] + _dot(vtc[rows, :], pr.astype(BF16))

    def attend_body(c, carry):
        attend(c, False)
        return carry

    lax.fori_loop(0, nck - 1, attend_body, 0)
    attend(nck - 1, True)

    outs = []
    for h in range(N_HEADS):
        rows = slice(h * HEAD_DIM, (h + 1) * HEAD_DIM)
        o = acc_ref[rows, :] / l_ref[h]
        oms = jnp.mean(o * o, axis=0, keepdims=True)
        outs.append(o * lax.rsqrt(oms + EPS))
    ot = jnp.concatenate(outs, axis=0)
    o_ref[0] = (ot.T * aog_ref[...]).astype(BF16)


def _attn(q, k, qi, kid, vt, wt, attn_out_g, tq, ck):
    b, s, _ = q.shape
    topk = min(TOPK_MAX, s // 4)
    assert tq % LANES == 0 and ck % tq == 0 and s % ck == 0 and s // CHUNK <= 256
    pos = jnp.arange(s)
    feats = [pos // CHUNK] * 3 + [pos % CHUNK] * 3 + [jnp.ones_like(pos)] * 3
    kf = jnp.zeros((s, LANES), F32).at[:, 0:N_FEAT].set(
        jnp.stack(feats, axis=1).astype(F32)).astype(BF16)
    blk = lambda w: pl.BlockSpec((1, tq, w), lambda bi, ji: (bi, ji, 0))
    once = pl.Buffered(1)
    whole = lambda w: pl.BlockSpec((1, s, w), lambda bi, ji: (bi, 0, 0), pipeline_mode=once)
    return pl.pallas_call(
        functools.partial(_attn_kernel, tq=tq, ck=ck, seq=s, topk=topk),
        grid=(b, s // tq),
        in_specs=[blk(ATTN_WIDTH), blk(IDX_HEADS * IDX_DIM),
                  pl.BlockSpec((1, IDX_HEADS, tq), lambda bi, ji: (bi, 0, ji)),
                  whole(2 * IDX_DIM), whole(ATTN_WIDTH),
                  pl.BlockSpec((s, LANES), lambda bi, ji: (0, 0), pipeline_mode=once),
                  pl.BlockSpec((1, s // ck, ATTN_WIDTH, ck), lambda bi, ji: (bi, 0, 0, 0),
                               pipeline_mode=once),
                  pl.BlockSpec((1, ATTN_WIDTH), lambda bi, ji: (0, 0))],
        out_specs=blk(ATTN_WIDTH),
        out_shape=jax.ShapeDtypeStruct((b, s, ATTN_WIDTH), BF16),
        scratch_shapes=[pltpu.VMEM((s, tq), I32),
                        pltpu.VMEM((ATTN_WIDTH, tq), F32),
                        pltpu.VMEM((N_HEADS, 1, tq), F32),
                        pltpu.VMEM((N_HEADS, 1, tq), F32),
                        pltpu.VMEM((IDX_HEADS * tq, LANES), BF16),
                        pltpu.VMEM((N_HEADS, tq, 2 * LANES), BF16),
                        pltpu.VMEM((1, tq), I32),
                        pltpu.VMEM((N_HEADS, ck, tq), F32)],
        compiler_params=pltpu.CompilerParams(
            dimension_semantics=("arbitrary", "arbitrary"), vmem_limit_bytes=56 * 1024 * 1024),
        name="attn",
    )(q, qi, wt, kid, k, kf, vt, attn_out_g.reshape(1, ATTN_WIDTH))


def _outproj_kernel(pool_ref, attn_ref, x_ref, wo_ref, g1_ref, sh_ref, sc_ref, ng_ref,
                    x1_ref, h2t_ref):
    mixed = _dot(pool_ref[0], wo_ref[0:POOL_WIDTH, :]) + _dot(attn_ref[0], wo_ref[POOL_WIDTH:, :])
    x1 = x_ref[0] + g1_ref[0] * mixed
    x1_ref[0] = x1
    ms = jnp.mean(x1 * x1, axis=-1, keepdims=True)
    h2 = (x1 * lax.rsqrt(ms + EPS) * ng_ref[...]) * (1.0 + sc_ref[0]) + sh_ref[0]
    h2t_ref[...] = h2.T.astype(BF16)


def _outproj(pool_out, attn_out, x, w_out, g1, sh2, sc2, norm_g, tm):
    b, s, d = x.shape
    nt = s // tm
    tok = lambda w: pl.BlockSpec((1, tm, w), lambda bi, ti: (bi, ti, 0))
    per_b = pl.BlockSpec((1, 1, d), lambda bi, ti: (bi, 0, 0))
    return pl.pallas_call(
        _outproj_kernel,
        grid=(b, nt),
        in_specs=[tok(POOL_WIDTH), tok(ATTN_WIDTH), tok(d),
                  pl.BlockSpec((POOL_WIDTH + ATTN_WIDTH, d), lambda bi, ti: (0, 0)),
                  per_b, per_b, per_b, pl.BlockSpec((1, d), lambda bi, ti: (0, 0))],
        out_specs=(tok(d), pl.BlockSpec((d, tm), lambda bi, ti: (0, bi * nt + ti))),
        out_shape=(jax.ShapeDtypeStruct((b, s, d), F32), jax.ShapeDtypeStruct((d, b * s), BF16)),
        compiler_params=pltpu.CompilerParams(dimension_semantics=("arbitrary", "arbitrary")),
        name="outproj",
    )(pool_out, attn_out, x, w_out.astype(BF16), g1, sh2, sc2, norm_g.reshape(1, d))


def _batcher_pairs(n):
    pairs = []

    def merge(lo, cnt, r):
        step = r * 2
        if step < cnt:
            merge(lo, cnt, step)
            merge(lo + r, cnt, step)
            for i in range(lo + r, lo + cnt - r, step):
                pairs.append((i, i + r))
        else:
            pairs.append((lo, lo + r))

    def sort(lo, cnt):
        if cnt > 1:
            m = cnt // 2
            sort(lo, m)
            sort(lo + m, m)
            merge(lo, cnt, 1)

    sort(0, n)
    return tuple(pairs)


_SORT16 = _batcher_pairs(PEER_TOPK)


def _top16_desc(x):
    x = list(x)
    for i, j in _SORT16:
        x[i], x[j] = jnp.maximum(x[i], x[j]), jnp.minimum(x[i], x[j])
    for shift in (4, 2, 1):
        y = [jnp.maximum(x[k], pltpu.roll(x[PEER_TOPK - 1 - k], shift, 0)) for k in range(PEER_TOPK)]
        for d in (8, 4, 2, 1):
            for k in range(PEER_TOPK):
                if k & d == 0:
                    y[k], y[k + d] = jnp.maximum(y[k], y[k + d]), jnp.minimum(y[k], y[k + d])
        x = y
    return x


def _peer_kernel(h2t_ref, x1_ref, g2_ref, wqt_ref, sk_ref, u_ref, vt_ref, o_ref,
                 s1_ref, s2_ref, thr_ref, mz_ref, p_ref, acc_ref, s1b_ref, *, tm, eb, ne):
    e = pl.program_id(1)
    nsub = eb // PEER_KEYS
    ntl = tm // LANES

    @pl.when(e == 0)
    def _prologue():
        h2t = h2t_ref[...]
        sub = lax.broadcasted_iota(I32, (8, tm), 0)
        ninf = jnp.full((8, tm), -jnp.inf, F32)

        def compact(t, base):
            out = t[base + 7]
            for r in range(6, -1, -1):
                out = jnp.where(sub == r, t[base + r], out)
            return out

        def split8(st):
            return [st[k * 8:(k + 1) * 8, :] for k in range(PEER_KEYS // 8)]

        def head(hd, carry):
            q1 = _dot(wqt_ref[2 * hd], h2t).astype(BF16)
            q2 = _dot(wqt_ref[2 * hd + 1], h2t).astype(BF16)
            st1 = _dot(sk_ref[2 * hd], q1) * LOG2E
            st2 = _dot(sk_ref[2 * hd + 1], q2) * LOG2E
            s1_ref[hd] = st1
            for tl in range(ntl):
                s2_ref[hd, tl] = st2[:, tl * LANES:(tl + 1) * LANES]
            t1 = _top16_desc(split8(st1))
            t2 = _top16_desc(split8(st2))
            a2lo, a2hi, a1hi = compact(t2, 0), compact(t2, 8), compact(t1, 8)
            cands = [t1[0] + a2lo, t1[0] + a2hi, t1[1] + a2lo]
            for r1 in range(2, 8):
                cands.append(jnp.where(sub < PEER_TOPK // (r1 + 1), t1[r1] + a2lo, ninf))
            cands.append(a1hi + t2[0])
            thr = _top16_desc(cands + [ninf] * (PEER_TOPK - len(cands)))[PEER_TOPK - 1]
            m = t1[0] + t2[0]
            z8 = jnp.zeros((8, tm), F32)
            for c in cands:
                z8 = z8 + jnp.where(c >= thr, jnp.exp2(c - m), 0.0)
            z = jnp.sum(z8, axis=0, keepdims=True)
            thr_ref[pl.ds(hd, 1), :] = thr[0:1, :]
            mz_ref[pl.ds(hd, 1), :] = m[0:1, :] + jnp.log2(z)
            return carry

        lax.fori_loop(0, PEER_HEADS, head, 0)
        acc_ref[...] = jnp.zeros_like(acc_ref)

    i0 = pl.multiple_of(e * nsub, nsub)
    s1rows = [s1_ref[hd, pl.ds(i0, nsub), :] for hd in range(PEER_HEADS)]
    for hd in range(PEER_HEADS):
        s1b_ref[hd] = s1rows[hd]
    for ii in range(nsub):
        rows = slice(ii * PEER_KEYS, (ii + 1) * PEER_KEYS)
        a_rows = _dot(u_ref[rows, :], h2t_ref[...])
        for tl in range(ntl):
            cols = slice(tl * LANES, (tl + 1) * LANES)
            w = jnp.zeros((PEER_KEYS, LANES), F32)
            for hd in range(PEER_HEADS):
                sm = s1b_ref[hd, ii:ii + 1, cols] + s2_ref[hd, tl]
                gate = jnp.exp2(sm - mz_ref[hd:hd + 1, cols])
                w = w + jnp.where(sm >= thr_ref[hd:hd + 1, cols], gate, 0.0)
            a = a_rows[:, cols]
            gl = 0.5 * a * (1.0 + lax.erf(a * (1.0 / math.sqrt(2.0))))
            p_ref[rows, cols] = (gl * w).astype(BF16)
    acc_ref[...] += _dot(vt_ref[...], p_ref[...])

    @pl.when(e == ne - 1)
    def _epilogue():
        o_ref[...] = x1_ref[...] + g2_ref[0] * acc_ref[...].T


def _peer(h2t, x1, g2, wq, sub_keys, u_tab, v_tab, seq, tm, eb):
    d, t = h2t.shape
    ne = u_tab.shape[0] // eb
    half = PEER_KEYS
    wqt = wq.T.reshape(2 * PEER_HEADS, half, d).astype(BF16)
    sk = sub_keys.reshape(2 * PEER_HEADS, PEER_KEYS, half).astype(BF16)
    ub = u_tab.astype(BF16)
    vtb = v_tab.T.astype(BF16)
    hs = (PEER_HEADS, PEER_KEYS, tm)
    return pl.pallas_call(
        functools.partial(_peer_kernel, tm=tm, eb=eb, ne=ne),
        grid=(t // tm, ne),
        in_specs=[pl.BlockSpec((d, tm), lambda i, e: (0, i)),
                  pl.BlockSpec((tm, d), lambda i, e: (i, 0)),
                  pl.BlockSpec((1, 1, d), lambda i, e: ((i * tm) // seq, 0, 0)),
                  pl.BlockSpec((2 * PEER_HEADS, half, d), lambda i, e: (0, 0, 0)),
                  pl.BlockSpec((2 * PEER_HEADS, PEER_KEYS, half), lambda i, e: (0, 0, 0)),
                  pl.BlockSpec((eb, d), lambda i, e: (e, 0)),
                  pl.BlockSpec((d, eb), lambda i, e: (0, e))],
        out_specs=pl.BlockSpec((tm, d), lambda i, e: (i, 0)),
        out_shape=jax.ShapeDtypeStruct((t, d), F32),
        scratch_shapes=[pltpu.VMEM(hs, F32),
                        pltpu.VMEM((PEER_HEADS, tm // LANES, PEER_KEYS, LANES), F32),
                        pltpu.VMEM((PEER_HEADS, tm), F32),
                        pltpu.VMEM((PEER_HEADS, tm), F32),
                        pltpu.VMEM((eb, tm), BF16),
                        pltpu.VMEM((d, tm), F32),
                        pltpu.VMEM((PEER_HEADS, eb // PEER_KEYS, tm), F32)],
        compiler_params=pltpu.CompilerParams(
            dimension_semantics=("arbitrary", "arbitrary"), vmem_limit_bytes=58 * 1024 * 1024),
        name="peer",
    )(h2t, x1, g2, wqt, sk, ub, vtb)


def _tile(n, pref):
    t = min(pref, n)
    assert n % t == 0
    return t


def kernel(x, c, ada_w, ada_b, norm1_g, w_in, pool_w, pool_scale, pool_out_g, q_norm_g, k_norm_g,
           attn_out_g, w_out, norm2_g, peer_wq, peer_subkeys, peer_u, peer_v):
    b, s, d = x.shape
    depth = ada_w.shape[0]
    tm = _tile(s, 512)
    tq = _tile(s, 2 * LANES)
    ck = _tile(tm, 512)
    tp = _tile(s, 512)
    eb = 1024
    for l in range(depth):
        mod = _ada(c, ada_w[l], ada_b[l]).reshape(b, 1, 6 * d)
        sh1, sc1, g1, sh2, sc2, g2 = jnp.split(mod, 6, axis=-1)
        q, k, qi, kid, vt, wt, pool_out = _inproj(
            x, sh1, sc1, norm1_g[l], w_in[l], pool_w[l], pool_scale[l], pool_out_g[l],
            q_norm_g[l], k_norm_g[l], tm, ck)
        attn_out = _attn(q, k, qi, kid, vt, wt, attn_out_g[l], tq, ck)
        x1, h2t = _outproj(pool_out, attn_out, x, w_out[l], g1, sh2, sc2, norm2_g[l], tm)
        x = _peer(h2t, x1.reshape(b * s, d), g2, peer_wq[l], peer_subkeys[l],
                  peer_u[l], peer_v[l], s, tp, eb).reshape(b, s, d)
    return x
```

```python
import functools
import math

import ml_dtypes
import numpy as np

import jax
import jax.numpy as jnp
from jax import lax
from jax.experimental import pallas as pl
from jax.experimental.pallas import tpu as pltpu

F32 = jnp.float32
BF16 = jnp.bfloat16
I32 = jnp.int32

CHUNK = 64
EPS = 1e-6
POOL_WINDOWS = (2, 4, 8, 16)
POOL_GROUP = 128
POOL_WIDTH = 512
ATTN_WIDTH = 512
HEAD_DIM = 64
N_HEADS = 8
IDX_HEADS = 8
IDX_DIM = 64
TOPK_MAX = 256
PEER_HEADS = 8
PEER_KEYS = 128
PEER_TOPK = 16
NEG = -1e30

LANES = 128
POOL_HALO = 16
LOG2E = math.log2(math.e)
M_INIT = 0.5 * NEG
N_FEAT = 9

NT_DIMS = (((1,), (1,)), ((), ()))


def _dot(a, b):
    return jnp.dot(a, b, preferred_element_type=F32)


def _dot_nt(a, b):
    return lax.dot_general(a, b, NT_DIMS, preferred_element_type=F32)


def _f32_key(v):
    b = int(np.float32(v).view(np.int32))
    return b ^ 0x7FFFFFFF if b < 0 else b


def _bf16_split3(c):
    c = np.float32(c)
    p1 = np.float32(c.astype(ml_dtypes.bfloat16))
    r = np.float32(c - p1)
    p2 = np.float32(r.astype(ml_dtypes.bfloat16))
    p3 = np.float32(np.float32(r - p2).astype(ml_dtypes.bfloat16))
    return float(p1), float(p2), float(p3)


_SLOPE_L2E = tuple(float(np.float32(2.0 ** (-8.0 * (h + 1) / N_HEADS) * LOG2E)) for h in range(N_HEADS))


def _ada_kernel(c_ref, w_ref, b_ref, o_ref):
    c = c_ref[...]
    cond = c * jax.nn.sigmoid(c)
    o_ref[...] = jnp.dot(cond, w_ref[...], preferred_element_type=F32,
                         precision=lax.Precision.HIGHEST) + b_ref[...]


def _ada(c, w, b):
    bsz, d = c.shape
    n = w.shape[1]
    tn = 1024
    return pl.pallas_call(
        _ada_kernel,
        grid=(n // tn,),
        in_specs=[pl.BlockSpec((bsz, d), lambda j: (0, 0)),
                  pl.BlockSpec((d, tn), lambda j: (0, j)),
                  pl.BlockSpec((1, tn), lambda j: (0, j))],
        out_specs=pl.BlockSpec((bsz, tn), lambda j: (0, j)),
        out_shape=jax.ShapeDtypeStruct((bsz, n), F32),
        name="ada",
    )(c, w, b.reshape(1, n))


def _group_sumsq(t, ones_bd):
    sq = t * t
    hi = sq.astype(BF16)
    lo = (sq - hi.astype(F32)).astype(BF16)
    return _dot(hi, ones_bd) + _dot(lo, ones_bd)


def _inproj_kernel(x_ref, sh_ref, sc_ref, g_ref, wnat_ref, wki_ref, wvt_ref, wwt_ref, ones_ref,
                   qg_ref, kg_ref, poolw_ref, pscale_ref, pg_ref,
                   q_ref, k_ref, qi_ref, kid_ref, vt_ref, wt_ref, pool_ref,
                   ubuf, *, tm, kc):
    i = pl.program_id(1)
    x = x_ref[0]
    ms = jnp.mean(x * x, axis=-1, keepdims=True)
    h = (x * lax.rsqrt(ms + EPS) * g_ref[...]) * (1.0 + sc_ref[0]) + sh_ref[0]
    hb = h.astype(BF16)

    proj = _dot(hb, wnat_ref[...])
    u = proj[:, 0:512]
    q = proj[:, 512:1024]
    k = proj[:, 1024:1536]
    qi = proj[:, 1536:2048]

    ones_bd = ones_ref[...]
    qn = q * lax.rsqrt(_group_sumsq(q, ones_bd) * (1.0 / HEAD_DIM) + EPS) * qg_ref[...]
    kn = k * lax.rsqrt(_group_sumsq(k, ones_bd) * (1.0 / HEAD_DIM) + EPS) * kg_ref[...]
    q_ref[0] = qn.astype(BF16)
    k_ref[0] = kn.astype(BF16)
    qi_ref[0] = (qi * (IDX_DIM ** -0.5)).astype(BF16)
    kid_ref[0] = _dot(hb, wki_ref[...]).astype(BF16)
    vt = _dot_nt(wvt_ref[...], hb).astype(BF16)
    for ci in range(tm // kc):
        vt_ref[0, ci] = vt[:, ci * kc:(ci + 1) * kc]
    wt = _dot_nt(wwt_ref[...], hb)
    wt_ref[0] = wt[0:IDX_HEADS, :] * (IDX_HEADS ** -0.5)

    @pl.when(i == 0)
    def _():
        ubuf[0:POOL_HALO, :] = jnp.zeros((POOL_HALO, POOL_WIDTH), F32)

    ubuf[POOL_HALO:POOL_HALO + tm, :] = u
    tpos = i * tm + lax.broadcasted_iota(I32, (tm, POOL_GROUP), 0)
    parts = []
    for gi, win in enumerate(POOL_WINDOWS):
        lo_l, hi_l = gi * POOL_GROUP, (gi + 1) * POOL_GROUP
        ws = ubuf[POOL_HALO:POOL_HALO + tm, lo_l:hi_l]
        for j in range(1, win):
            ws = ws + ubuf[POOL_HALO - j:POOL_HALO - j + tm, lo_l:hi_l]
        cnt = jnp.minimum(tpos + 1, win).astype(F32)
        pooled = ws / cnt - u[:, lo_l:hi_l]
        parts.append(_dot(pooled.astype(BF16), poolw_ref[gi]))
    mixed = jnp.concatenate(parts, axis=-1) * pscale_ref[...]
    pms = jnp.mean(mixed * mixed, axis=-1, keepdims=True)
    pool_ref[0] = (mixed * lax.rsqrt(pms + EPS) * pg_ref[...]).astype(BF16)
    ubuf[0:POOL_HALO, :] = ubuf[tm:tm + POOL_HALO, :]


def _inproj(x, sh1, sc1, norm_g, w_in, pool_w, pool_scale, pool_out_g, q_norm_g, k_norm_g, tm, kc):
    b, s, d = x.shape
    nt = s // tm
    w_u, w_q, w_k, w_v, w_qi, w_ki, w_wi = jnp.split(
        w_in, (512, 1024, 1536, 2048, 2560, 2624), axis=-1)
    wnat = jnp.concatenate([w_u, w_q, w_k, w_qi], axis=-1).astype(BF16)
    wki = jnp.concatenate([w_ki, w_ki], axis=-1).astype(BF16)
    wvt = w_v.T.astype(BF16)
    wwt = jnp.concatenate([w_wi.T, jnp.zeros((16 - IDX_HEADS, d), F32)], axis=0).astype(BF16)
    gid = jnp.arange(ATTN_WIDTH) // HEAD_DIM
    ones_bd = (gid[:, None] == gid[None, :]).astype(BF16)
    qg = (jnp.tile(q_norm_g, N_HEADS) * (HEAD_DIM ** -0.5 * LOG2E)).reshape(1, ATTN_WIDTH)
    kg = jnp.tile(k_norm_g, N_HEADS).reshape(1, ATTN_WIDTH)

    full = lambda shape: pl.BlockSpec(shape, lambda bi, ti: (0,) * len(shape))
    tok = lambda w: pl.BlockSpec((1, tm, w), lambda bi, ti: (bi, ti, 0))
    per_b = pl.BlockSpec((1, 1, d), lambda bi, ti: (bi, 0, 0))
    out_shapes = (
        jax.ShapeDtypeStruct((b, s, ATTN_WIDTH), BF16),
        jax.ShapeDtypeStruct((b, s, ATTN_WIDTH), BF16),
        jax.ShapeDtypeStruct((b, s, IDX_HEADS * IDX_DIM), BF16),
        jax.ShapeDtypeStruct((b, s, 2 * IDX_DIM), BF16),
        jax.ShapeDtypeStruct((b, s // kc, ATTN_WIDTH, kc), BF16),
        jax.ShapeDtypeStruct((b, IDX_HEADS, s), F32),
        jax.ShapeDtypeStruct((b, s, POOL_WIDTH), BF16),
    )
    out_specs = (
        tok(ATTN_WIDTH), tok(ATTN_WIDTH), tok(IDX_HEADS * IDX_DIM), tok(2 * IDX_DIM),
        pl.BlockSpec((1, tm // kc, ATTN_WIDTH, kc), lambda bi, ti: (bi, ti, 0, 0)),
        pl.BlockSpec((1, IDX_HEADS, tm), lambda bi, ti: (bi, 0, ti)),
        tok(POOL_WIDTH),
    )
    return pl.pallas_call(
        functools.partial(_inproj_kernel, tm=tm, kc=kc),
        grid=(b, nt),
        in_specs=[tok(d), per_b, per_b, full((1, d)), full((d, 2048)), full((d, 2 * IDX_DIM)),
                  full((ATTN_WIDTH, d)), full((16, d)), full((ATTN_WIDTH, ATTN_WIDTH)),
                  full((1, ATTN_WIDTH)), full((1, ATTN_WIDTH)),
                  full((len(POOL_WINDOWS), POOL_GROUP, POOL_GROUP)),
                  full((1, POOL_WIDTH)), full((1, POOL_WIDTH))],
        out_specs=out_specs,
        out_shape=out_shapes,
        scratch_shapes=[pltpu.VMEM((tm + POOL_HALO, POOL_WIDTH), F32)],
        compiler_params=pltpu.CompilerParams(
            dimension_semantics=("arbitrary", "arbitrary"), vmem_limit_bytes=48 * 1024 * 1024),
        name="inproj",
    )(x, sh1, sc1, norm_g.reshape(1, d), wnat, wki, wvt, wwt, ones_bd, qg, kg,
      pool_w.astype(BF16), pool_scale.reshape(1, -1), pool_out_g.reshape(1, -1))


def _attn_kernel(q_ref, qi_ref, wt_ref, kid_ref, k_ref, kf_ref, vt_ref, aog_ref, o_ref,
                 key_ref, acc_ref, m_ref, l_ref, qim_ref, qmf_ref, cut_ref, lg_ref,
                 *, tq, ck, seq, topk):
    j = pl.program_id(1)
    nck = ((j + 1) * tq + ck - 1) // ck
    qpos = j * tq + lax.broadcasted_iota(I32, (1, tq), 1)
    qchunk = qpos // CHUNK
    row_i = lax.broadcasted_iota(I32, (ck, tq), 0)

    lane = lax.broadcasted_iota(I32, (tq, LANES), 1)
    tcol = (j * tq + lax.broadcasted_iota(I32, (tq, LANES), 0)).astype(F32)
    for h in range(N_HEADS):
        p = h // 2
        keep = (lane < HEAD_DIM) if h % 2 == 0 else (lane >= HEAD_DIM)
        qi_slab = qi_ref[0, :, p * LANES:(p + 1) * LANES]
        q_slab = q_ref[0, :, p * LANES:(p + 1) * LANES]
        qim_ref[h * tq:(h + 1) * tq, :] = jnp.where(keep, qi_slab, jnp.zeros_like(qi_slab))
        qmf_ref[h, :, 0:LANES] = jnp.where(keep, q_slab, jnp.zeros_like(q_slab))
        c1, c2, c3 = _bf16_split3(_SLOPE_L2E[h])
        u = -_SLOPE_L2E[h] * tcol
        u1 = u.astype(BF16).astype(F32)
        r = u - u1
        u2 = r.astype(BF16).astype(F32)
        u3 = (r - u2).astype(BF16).astype(F32)
        feat = jnp.zeros((tq, LANES), F32)
        for li, val in enumerate((64.0 * c1, 64.0 * c2, 64.0 * c3, c1, c2, c3, u1, u2, u3)):
            feat = jnp.where(lane == li, val, feat)
        qmf_ref[h, :, LANES:2 * LANES] = feat.astype(BF16)

    def scores(c, last):
        r0 = pl.multiple_of(c * ck, ck)
        d = _dot_nt(kid_ref[0, pl.ds(r0, ck), :], qim_ref[...])
        acc = jnp.zeros((ck, tq), F32)
        for h in range(IDX_HEADS):
            acc = acc + wt_ref[0, h:h + 1, :] * jnp.maximum(d[:, h * tq:(h + 1) * tq], 0.0)
        if last:
            acc = jnp.where(((r0 + row_i) // CHUNK) <= qchunk, acc, NEG)
        bits = pltpu.bitcast(acc, I32)
        key_ref[pl.ds(r0, ck), :] = jnp.where(bits < 0, bits ^ 0x7FFFFFFF, bits)

    def scores_body(c, carry):
        scores(c, False)
        return carry

    lax.fori_loop(0, nck - 1, scores_body, 0)
    scores(nck - 1, True)

    def count(pred):
        def body(c, c8):
            r0 = pl.multiple_of(c * ck, ck)
            m = pred(key_ref[pl.ds(r0, ck), :], r0).astype(I32)
            return c8 + m.reshape(ck // 8, 8, tq).sum(axis=0)
        c8 = lax.fori_loop(0, nck, body, jnp.zeros((8, tq), I32))
        return c8.sum(axis=0, keepdims=True)

    c_nonneg = count(lambda kk, r0: kk >= 0)
    lo0 = jnp.where(c_nonneg >= topk, 0, -(2 ** 31)).astype(I32)

    def bit_body(b, lo):
        trial = lo + jnp.left_shift(jnp.int32(1), 30 - b)
        c = count(lambda kk, r0: kk >= trial)
        return jnp.where(c >= topk, trial, lo)

    lo = lax.fori_loop(0, 31, bit_body, lo0)

    c_gt = count(lambda kk, r0: kk > lo)
    c_ge = count(lambda kk, r0: kk >= lo)
    need = topk - c_gt
    cut_ref[...] = jnp.full((1, tq), seq, I32)

    @pl.when(jnp.max(c_ge) > topk)
    def _():
        nbits = max(1, (seq - 1).bit_length())

        def idx_body(b, jv):
            trial = jv + jnp.left_shift(jnp.int32(1), nbits - 1 - b)
            c = count(lambda kk, r0: (kk == lo)
                      & ((r0 + lax.broadcasted_iota(I32, (ck, tq), 0)) < trial))
            return jnp.where(c < need, trial, jv)

        jv = lax.fori_loop(0, nbits, idx_body, jnp.zeros((1, tq), I32))
        cut_ref[...] = jv + 1

    cut = cut_ref[...]

    m_ref[...] = jnp.full((N_HEADS, 1, tq), M_INIT, F32)
    l_ref[...] = jnp.zeros((N_HEADS, 1, tq), F32)
    acc_ref[...] = jnp.zeros((ATTN_WIDTH, tq), F32)

    def attend(c, last):
        r0 = pl.multiple_of(c * ck, ck)
        kk = key_ref[pl.ds(r0, ck), :]
        kf = kf_ref[pl.ds(r0, ck), :]
        kch = k_ref[0, pl.ds(r0, ck), :]
        vtc = vt_ref[0, c]
        kpos = r0 + row_i
        sel = (kk > lo) | ((kk == lo) & (kpos < cut))
        if last:
            sel = sel & ((kpos // CHUNK) <= qchunk)
            ahead = jnp.maximum(kpos - qpos, 0).astype(F32)
        pen = jnp.where(sel, 0.0, NEG)
        for h in range(N_HEADS):
            p = h // 2
            ks = kch[:, p * LANES:(p + 1) * LANES]
            lg = _dot_nt(jnp.concatenate([ks, kf], axis=1), qmf_ref[h]) + pen
            if last:
                lg = lg - (2.0 * _SLOPE_L2E[h]) * ahead
            lg_ref[h] = lg
        for h in range(N_HEADS):
            m_old = m_ref[h]
            m_new = jnp.maximum(m_old, jnp.max(lg_ref[h], axis=0, keepdims=True))
            alpha = jnp.exp2(m_old - m_new)
            pr = jnp.exp2(lg_ref[h] - m_new)
            l_ref[h] = alpha * l_ref[h] + jnp.sum(pr, axis=0, keepdims=True)
            m_ref[h] = m_new
            rows = slice(h * HEAD_DIM, (h + 1) * HEAD_DIM)
            acc_ref[rows, :] = alpha * acc_ref[rows, :] + _dot(vtc[rows, :], pr.astype(BF16))

    def attend_body(c, carry):
        attend(c, False)
        return carry

    lax.fori_loop(0, nck - 1, attend_body, 0)
    attend(nck - 1, True)

    outs = []
    for h in range(N_HEADS):
        rows = slice(h * HEAD_DIM, (h + 1) * HEAD_DIM)
        o = acc_ref[rows, :] / l_ref[h]
        oms = jnp.mean(o * o, axis=0, keepdims=True)
        outs.append(o * lax.rsqrt(oms + EPS))
    ot = jnp.concatenate(outs, axis=0)
    o_ref[0] = (ot.T * aog_ref[...]).astype(BF16)


def _attn(q, k, qi, kid, vt, wt, attn_out_g, tq, ck):
    b, s, _ = q.shape
    topk = min(TOPK_MAX, s // 4)
    assert tq % LANES == 0 and ck % tq == 0 and s % ck == 0 and s // CHUNK <= 256
    pos = jnp.arange(s)
    feats = [pos // CHUNK] * 3 + [pos % CHUNK] * 3 + [jnp.ones_like(pos)] * 3
    kf = jnp.zeros((s, LANES), F32).at[:, 0:N_FEAT].set(
        jnp.stack(feats, axis=1).astype(F32)).astype(BF16)
    blk = lambda w: pl.BlockSpec((1, tq, w), lambda bi, ji: (bi, ji, 0))
    once = pl.Buffered(1)
    whole = lambda w: pl.BlockSpec((1, s, w), lambda bi, ji: (bi, 0, 0), pipeline_mode=once)
    return pl.pallas_call(
        functools.partial(_attn_kernel, tq=tq, ck=ck, seq=s, topk=topk),
        grid=(b, s // tq),
        in_specs=[blk(ATTN_WIDTH), blk(IDX_HEADS * IDX_DIM),
                  pl.BlockSpec((1, IDX_HEADS, tq), lambda bi, ji: (bi, 0, ji)),
                  whole(2 * IDX_DIM), whole(ATTN_WIDTH),
                  pl.BlockSpec((s, LANES), lambda bi, ji: (0, 0), pipeline_mode=once),
                  pl.BlockSpec((1, s // ck, ATTN_WIDTH, ck), lambda bi, ji: (bi, 0, 0, 0),
                               pipeline_mode=once),
                  pl.BlockSpec((1, ATTN_WIDTH), lambda bi, ji: (0, 0))],
        out_specs=blk(ATTN_WIDTH),
        out_shape=jax.ShapeDtypeStruct((b, s, ATTN_WIDTH), BF16),
        scratch_shapes=[pltpu.VMEM((s, tq), I32),
                        pltpu.VMEM((ATTN_WIDTH, tq), F32),
                        pltpu.VMEM((N_HEADS, 1, tq), F32),
                        pltpu.VMEM((N_HEADS, 1, tq), F32),
                        pltpu.VMEM((IDX_HEADS * tq, LANES), BF16),
                        pltpu.VMEM((N_HEADS, tq, 2 * LANES), BF16),
                        pltpu.VMEM((1, tq), I32),
                        pltpu.VMEM((N_HEADS, ck, tq), F32)],
        compiler_params=pltpu.CompilerParams(
            dimension_semantics=("arbitrary", "arbitrary"), vmem_limit_bytes=56 * 1024 * 1024),
        name="attn",
    )(q, qi, wt, kid, k, kf, vt, attn_out_g.reshape(1, ATTN_WIDTH))


def _outproj_kernel(pool_ref, attn_ref, x_ref, wo_ref, g1_ref, sh_ref, sc_ref, ng_ref,
                    x1_ref, h2t_ref):
    mixed = _dot(pool_ref[0], wo_ref[0:POOL_WIDTH, :]) + _dot(attn_ref[0], wo_ref[POOL_WIDTH:, :])
    x1 = x_ref[0] + g1_ref[0] * mixed
    x1_ref[0] = x1
    ms = jnp.mean(x1 * x1, axis=-1, keepdims=True)
    h2 = (x1 * lax.rsqrt(ms + EPS) * ng_ref[...]) * (1.0 + sc_ref[0]) + sh_ref[0]
    h2t_ref[...] = h2.T.astype(BF16)


def _outproj(pool_out, attn_out, x, w_out, g1, sh2, sc2, norm_g, tm):
    b, s, d = x.shape
    nt = s // tm
    tok = lambda w: pl.BlockSpec((1, tm, w), lambda bi, ti: (bi, ti, 0))
    per_b = pl.BlockSpec((1, 1, d), lambda bi, ti: (bi, 0, 0))
    return pl.pallas_call(
        _outproj_kernel,
        grid=(b, nt),
        in_specs=[tok(POOL_WIDTH), tok(ATTN_WIDTH), tok(d),
                  pl.BlockSpec((POOL_WIDTH + ATTN_WIDTH, d), lambda bi, ti: (0, 0)),
                  per_b, per_b, per_b, pl.BlockSpec((1, d), lambda bi, ti: (0, 0))],
        out_specs=(tok(d), pl.BlockSpec((d, tm), lambda bi, ti: (0, bi * nt + ti))),
        out_shape=(jax.ShapeDtypeStruct((b, s, d), F32), jax.ShapeDtypeStruct((d, b * s), BF16)),
        compiler_params=pltpu.CompilerParams(dimension_semantics=("arbitrary", "arbitrary")),
        name="outproj",
    )(pool_out, attn_out, x, w_out.astype(BF16), g1, sh2, sc2, norm_g.reshape(1, d))


def _batcher_pairs(n):
    pairs = []

    def merge(lo, cnt, r):
        step = r * 2
        if step < cnt:
            merge(lo, cnt, step)
            merge(lo + r, cnt, step)
            for i in range(lo + r, lo + cnt - r, step):
                pairs.append((i, i + r))
        else:
            pairs.append((lo, lo + r))

    def sort(lo, cnt):
        if cnt > 1:
            m = cnt // 2
            sort(lo, m)
            sort(lo + m, m)
            merge(lo, cnt, 1)

    sort(0, n)
    return tuple(pairs)


_SORT16 = _batcher_pairs(PEER_TOPK)
A_PIECE = 4


def _top16_desc(x):
    x = list(x)
    for i, j in _SORT16:
        x[i], x[j] = jnp.maximum(x[i], x[j]), jnp.minimum(x[i], x[j])
    for shift in (4, 2, 1):
        y = [jnp.maximum(x[k], pltpu.roll(x[PEER_TOPK - 1 - k], shift, 0)) for k in range(PEER_TOPK)]
        for d in (8, 4, 2, 1):
            for k in range(PEER_TOPK):
                if k & d == 0:
                    y[k], y[k + d] = jnp.maximum(y[k], y[k + d]), jnp.minimum(y[k], y[k + d])
        x = y
    return x


def _peer_kernel(h2t_ref, x1_ref, g2_ref, wqt_ref, sk_ref, u_ref, vt_ref, o_ref,
                 s1_ref, s2_ref, thr_ref, mz_ref, p_ref, acc_ref, s1b_ref, *, tm, eb, ne):
    e = pl.program_id(1)
    nsub = eb // PEER_KEYS
    ntl = tm // LANES

    @pl.when(e == 0)
    def _prologue():
        h2t = h2t_ref[...]
        sub = lax.broadcasted_iota(I32, (8, tm), 0)
        ninf = jnp.full((8, tm), -jnp.inf, F32)

        def compact(t, base):
            out = t[base + 7]
            for r in range(6, -1, -1):
                out = jnp.where(sub == r, t[base + r], out)
            return out

        def split8(st):
            return [st[k * 8:(k + 1) * 8, :] for k in range(PEER_KEYS // 8)]

        def head(hd, carry):
            q1 = _dot(wqt_ref[2 * hd], h2t).astype(BF16)
            q2 = _dot(wqt_ref[2 * hd + 1], h2t).astype(BF16)
            st1 = _dot(sk_ref[2 * hd], q1) * LOG2E
            st2 = _dot(sk_ref[2 * hd + 1], q2) * LOG2E
            s1_ref[hd] = st1
            for tl in range(ntl):
                s2_ref[hd, tl] = st2[:, tl * LANES:(tl + 1) * LANES]
            t1 = _top16_desc(split8(st1))
            t2 = _top16_desc(split8(st2))
            a2lo, a2hi, a1hi = compact(t2, 0), compact(t2, 8), compact(t1, 8)
            cands = [t1[0] + a2lo, t1[0] + a2hi, t1[1] + a2lo]
            for r1 in range(2, 8):
                cands.append(jnp.where(sub < PEER_TOPK // (r1 + 1), t1[r1] + a2lo, ninf))
            cands.append(a1hi + t2[0])
            thr = _top16_desc(cands + [ninf] * (PEER_TOPK - len(cands)))[PEER_TOPK - 1]
            m = t1[0] + t2[0]
            z8 = jnp.zeros((8, tm), F32)
            for c in cands:
                z8 = z8 + jnp.where(c >= thr, jnp.exp2(c - m), 0.0)
            z = jnp.sum(z8, axis=0, keepdims=True)
            thr_ref[pl.ds(hd, 1), :] = thr[0:1, :]
            mz_ref[pl.ds(hd, 1), :] = m[0:1, :] + jnp.log2(z)
            return carry

        lax.fori_loop(0, PEER_HEADS, head, 0)
        acc_ref[...] = jnp.zeros_like(acc_ref)

    i0 = pl.multiple_of(e * nsub, nsub)
    s1rows = [s1_ref[hd, pl.ds(i0, nsub), :] for hd in range(PEER_HEADS)]
    for hd in range(PEER_HEADS):
        s1b_ref[hd] = s1rows[hd]
    for ii in range(nsub):
        rows = slice(ii * PEER_KEYS, (ii + 1) * PEER_KEYS)
        if ii % A_PIECE == 0:
            a_piece = _dot(u_ref[ii * PEER_KEYS:(ii + A_PIECE) * PEER_KEYS, :], h2t_ref[...])
        a_rows = a_piece[(ii % A_PIECE) * PEER_KEYS:(ii % A_PIECE + 1) * PEER_KEYS, :]
        for tl in range(ntl):
            cols = slice(tl * LANES, (tl + 1) * LANES)
            w = jnp.zeros((PEER_KEYS, LANES), F32)
            for hd in range(PEER_HEADS):
                sm = s1b_ref[hd, ii:ii + 1, cols] + s2_ref[hd, tl]
                gate = jnp.exp2(sm - mz_ref[hd:hd + 1, cols])
                w = w + jnp.where(sm >= thr_ref[hd:hd + 1, cols], gate, 0.0)
            a = a_rows[:, cols]
            gl = 0.5 * a * (1.0 + lax.erf(a * (1.0 / math.sqrt(2.0))))
            p_ref[rows, cols] = (gl * w).astype(BF16)
    acc_ref[...] += _dot(vt_ref[...], p_ref[...])

    @pl.when(e == ne - 1)
    def _epilogue():
        o_ref[...] = x1_ref[...] + g2_ref[0] * acc_ref[...].T


def _peer(h2t, x1, g2, wq, sub_keys, u_tab, v_tab, seq, tm, eb):
    d, t = h2t.shape
    ne = u_tab.shape[0] // eb
    half = PEER_KEYS
    wqt = wq.T.reshape(2 * PEER_HEADS, half, d).astype(BF16)
    sk = sub_keys.reshape(2 * PEER_HEADS, PEER_KEYS, half).astype(BF16)
    ub = u_tab.astype(BF16)
    vtb = v_tab.T.astype(BF16)
    hs = (PEER_HEADS, PEER_KEYS, tm)
    return pl.pallas_call(
        functools.partial(_peer_kernel, tm=tm, eb=eb, ne=ne),
        grid=(t // tm, ne),
        in_specs=[pl.BlockSpec((d, tm), lambda i, e: (0, i)),
                  pl.BlockSpec((tm, d), lambda i, e: (i, 0)),
                  pl.BlockSpec((1, 1, d), lambda i, e: ((i * tm) // seq, 0, 0)),
                  pl.BlockSpec((2 * PEER_HEADS, half, d), lambda i, e: (0, 0, 0)),
                  pl.BlockSpec((2 * PEER_HEADS, PEER_KEYS, half), lambda i, e: (0, 0, 0)),
                  pl.BlockSpec((eb, d), lambda i, e: (e, 0)),
                  pl.BlockSpec((d, eb), lambda i, e: (0, e))],
        out_specs=pl.BlockSpec((tm, d), lambda i, e: (i, 0)),
        out_shape=jax.ShapeDtypeStruct((t, d), F32),
        scratch_shapes=[pltpu.VMEM(hs, F32),
                        pltpu.VMEM((PEER_HEADS, tm // LANES, PEER_KEYS, LANES), F32),
                        pltpu.VMEM((PEER_HEADS, tm), F32),
                        pltpu.VMEM((PEER_HEADS, tm), F32),
                        pltpu.VMEM((eb, tm), BF16),
                        pltpu.VMEM((d, tm), F32),
                        pltpu.VMEM((PEER_HEADS, eb // PEER_KEYS, tm), F32)],
        compiler_params=pltpu.CompilerParams(
            dimension_semantics=("arbitrary", "arbitrary"), vmem_limit_bytes=58 * 1024 * 1024),
        name="peer",
    )(h2t, x1, g2, wqt, sk, ub, vtb)


def _tile(n, pref):
    t = min(pref, n)
    assert n % t == 0
    return t


def kernel(x, c, ada_w, ada_b, norm1_g, w_in, pool_w, pool_scale, pool_out_g, q_norm_g, k_norm_g,
           attn_out_g, w_out, norm2_g, peer_wq, peer_subkeys, peer_u, peer_v):
    b, s, d = x.shape
    depth = ada_w.shape[0]
    tm = _tile(s, 512)
    tq = _tile(s, 2 * LANES)
    ck = _tile(tm, 512)
    tp = _tile(s, 512)
    eb = 1024
    for l in range(depth):
        mod = _ada(c, ada_w[l], ada_b[l]).reshape(b, 1, 6 * d)
        sh1, sc1, g1, sh2, sc2, g2 = jnp.split(mod, 6, axis=-1)
        q, k, qi, kid, vt, wt, pool_out = _inproj(
            x, sh1, sc1, norm1_g[l], w_in[l], pool_w[l], pool_scale[l], pool_out_g[l],
            q_norm_g[l], k_norm_g[l], tm, ck)
        attn_out = _attn(q, k, qi, kid, vt, wt, attn_out_g[l], tq, ck)
        x1, h2t = _outproj(pool_out, attn_out, x, w_out[l], g1, sh2, sc2, norm2_g[l], tm)
        x = _peer(h2t, x1.reshape(b * s, d), g2, peer_wq[l], peer_subkeys[l],
                  peer_u[l], peer_v[l], s, tp, eb).reshape(b, s, d)
    return x
```

```python
import functools
import math

import ml_dtypes
import numpy as np

import jax
import jax.numpy as jnp
from jax import lax
from jax.experimental import pallas as pl
from jax.experimental.pallas import tpu as pltpu

F32 = jnp.float32
BF16 = jnp.bfloat16
I32 = jnp.int32

CHUNK = 64
EPS = 1e-6
POOL_WINDOWS = (2, 4, 8, 16)
POOL_GROUP = 128
POOL_WIDTH = 512
ATTN_WIDTH = 512
HEAD_DIM = 64
N_HEADS = 8
IDX_HEADS = 8
IDX_DIM = 64
TOPK_MAX = 256
PEER_HEADS = 8
PEER_KEYS = 128
PEER_TOPK = 16
NEG = -1e30

LANES = 128
POOL_HALO = 16
LOG2E = math.log2(math.e)
M_INIT = 0.5 * NEG
N_FEAT = 9

NT_DIMS = (((1,), (1,)), ((), ()))


def _dot(a, b):
    return jnp.dot(a, b, preferred_element_type=F32)


def _dot_nt(a, b):
    return lax.dot_general(a, b, NT_DIMS, preferred_element_type=F32)


def _f32_key(v):
    b = int(np.float32(v).view(np.int32))
    return b ^ 0x7FFFFFFF if b < 0 else b


def _bf16_split3(c):
    c = np.float32(c)
    p1 = np.float32(c.astype(ml_dtypes.bfloat16))
    r = np.float32(c - p1)
    p2 = np.float32(r.astype(ml_dtypes.bfloat16))
    p3 = np.float32(np.float32(r - p2).astype(ml_dtypes.bfloat16))
    return float(p1), float(p2), float(p3)


_SLOPE_L2E = tuple(float(np.float32(2.0 ** (-8.0 * (h + 1) / N_HEADS) * LOG2E)) for h in range(N_HEADS))


def _ada_kernel(c_ref, w_ref, b_ref, o_ref):
    c = c_ref[...]
    cond = c * jax.nn.sigmoid(c)
    o_ref[...] = jnp.dot(cond, w_ref[...], preferred_element_type=F32,
                         precision=lax.Precision.HIGHEST) + b_ref[...]


def _ada(c, w, b):
    bsz, d = c.shape
    n = w.shape[1]
    tn = 1024
    return pl.pallas_call(
        _ada_kernel,
        grid=(n // tn,),
        in_specs=[pl.BlockSpec((bsz, d), lambda j: (0, 0)),
                  pl.BlockSpec((d, tn), lambda j: (0, j)),
                  pl.BlockSpec((1, tn), lambda j: (0, j))],
        out_specs=pl.BlockSpec((bsz, tn), lambda j: (0, j)),
        out_shape=jax.ShapeDtypeStruct((bsz, n), F32),
        name="ada",
    )(c, w, b.reshape(1, n))


def _group_sumsq(t, ones_bd):
    sq = t * t
    hi = sq.astype(BF16)
    lo = (sq - hi.astype(F32)).astype(BF16)
    return _dot(hi, ones_bd) + _dot(lo, ones_bd)


def _inproj_kernel(x_ref, sh_ref, sc_ref, g_ref, wnat_ref, wki_ref, wvt_ref, wwt_ref, ones_ref,
                   qg_ref, kg_ref, poolw_ref, pscale_ref, pg_ref,
                   q_ref, k_ref, qi_ref, kid_ref, vt_ref, wt_ref, pool_ref,
                   ubuf, *, tm, kc):
    i = pl.program_id(1)
    x = x_ref[0]
    ms = jnp.mean(x * x, axis=-1, keepdims=True)
    h = (x * lax.rsqrt(ms + EPS) * g_ref[...]) * (1.0 + sc_ref[0]) + sh_ref[0]
    hb = h.astype(BF16)

    proj = _dot(hb, wnat_ref[...])
    u = proj[:, 0:512]
    q = proj[:, 512:1024]
    k = proj[:, 1024:1536]
    qi = proj[:, 1536:2048]

    ones_bd = ones_ref[...]
    qn = q * lax.rsqrt(_group_sumsq(q, ones_bd) * (1.0 / HEAD_DIM) + EPS) * qg_ref[...]
    kn = k * lax.rsqrt(_group_sumsq(k, ones_bd) * (1.0 / HEAD_DIM) + EPS) * kg_ref[...]
    q_ref[0] = qn.astype(BF16)
    k_ref[0] = kn.astype(BF16)
    qi_ref[0] = (qi * (IDX_DIM ** -0.5)).astype(BF16)
    kid_ref[0] = _dot(hb, wki_ref[...]).astype(BF16)
    vt = _dot_nt(wvt_ref[...], hb).astype(BF16)
    for ci in range(tm // kc):
        vt_ref[0, ci] = vt[:, ci * kc:(ci + 1) * kc]
    wt = _dot_nt(wwt_ref[...], hb)
    wt_ref[0] = wt[0:IDX_HEADS, :] * (IDX_HEADS ** -0.5)

    @pl.when(i == 0)
    def _():
        ubuf[0:POOL_HALO, :] = jnp.zeros((POOL_HALO, POOL_WIDTH), F32)

    ubuf[POOL_HALO:POOL_HALO + tm, :] = u
    tpos = i * tm + lax.broadcasted_iota(I32, (tm, POOL_GROUP), 0)
    parts = []
    for gi, win in enumerate(POOL_WINDOWS):
        lo_l, hi_l = gi * POOL_GROUP, (gi + 1) * POOL_GROUP
        ws = ubuf[POOL_HALO:POOL_HALO + tm, lo_l:hi_l]
        for j in range(1, win):
            ws = ws + ubuf[POOL_HALO - j:POOL_HALO - j + tm, lo_l:hi_l]
        cnt = jnp.minimum(tpos + 1, win).astype(F32)
        pooled = ws / cnt - u[:, lo_l:hi_l]
        parts.append(_dot(pooled.astype(BF16), poolw_ref[gi]))
    mixed = jnp.concatenate(parts, axis=-1) * pscale_ref[...]
    pms = jnp.mean(mixed * mixed, axis=-1, keepdims=True)
    pool_ref[0] = (mixed * lax.rsqrt(pms + EPS) * pg_ref[...]).astype(BF16)
    ubuf[0:POOL_HALO, :] = ubuf[tm:tm + POOL_HALO, :]


def _inproj(x, sh1, sc1, norm_g, w_in, pool_w, pool_scale, pool_out_g, q_norm_g, k_norm_g, tm, kc):
    b, s, d = x.shape
    nt = s // tm
    w_u, w_q, w_k, w_v, w_qi, w_ki, w_wi = jnp.split(
        w_in, (512, 1024, 1536, 2048, 2560, 2624), axis=-1)
    wnat = jnp.concatenate([w_u, w_q, w_k, w_qi], axis=-1).astype(BF16)
    wki = jnp.concatenate([w_ki, w_ki], axis=-1).astype(BF16)
    wvt = w_v.T.astype(BF16)
    wwt = jnp.concatenate([w_wi.T, jnp.zeros((16 - IDX_HEADS, d), F32)], axis=0).astype(BF16)
    gid = jnp.arange(ATTN_WIDTH) // HEAD_DIM
    ones_bd = (gid[:, None] == gid[None, :]).astype(BF16)
    qg = (jnp.tile(q_norm_g, N_HEADS) * (HEAD_DIM ** -0.5 * LOG2E)).reshape(1, ATTN_WIDTH)
    kg = jnp.tile(k_norm_g, N_HEADS).reshape(1, ATTN_WIDTH)

    full = lambda shape: pl.BlockSpec(shape, lambda bi, ti: (0,) * len(shape))
    tok = lambda w: pl.BlockSpec((1, tm, w), lambda bi, ti: (bi, ti, 0))
    per_b = pl.BlockSpec((1, 1, d), lambda bi, ti: (bi, 0, 0))
    out_shapes = (
        jax.ShapeDtypeStruct((b, s, ATTN_WIDTH), BF16),
        jax.ShapeDtypeStruct((b, s, ATTN_WIDTH), BF16),
        jax.ShapeDtypeStruct((b, s, IDX_HEADS * IDX_DIM), BF16),
        jax.ShapeDtypeStruct((b, s, 2 * IDX_DIM), BF16),
        jax.ShapeDtypeStruct((b, s // kc, ATTN_WIDTH, kc), BF16),
        jax.ShapeDtypeStruct((b, IDX_HEADS, s), F32),
        jax.ShapeDtypeStruct((b, s, POOL_WIDTH), BF16),
    )
    out_specs = (
        tok(ATTN_WIDTH), tok(ATTN_WIDTH), tok(IDX_HEADS * IDX_DIM), tok(2 * IDX_DIM),
        pl.BlockSpec((1, tm // kc, ATTN_WIDTH, kc), lambda bi, ti: (bi, ti, 0, 0)),
        pl.BlockSpec((1, IDX_HEADS, tm), lambda bi, ti: (bi, 0, ti)),
        tok(POOL_WIDTH),
    )
    return pl.pallas_call(
        functools.partial(_inproj_kernel, tm=tm, kc=kc),
        grid=(b, nt),
        in_specs=[tok(d), per_b, per_b, full((1, d)), full((d, 2048)), full((d, 2 * IDX_DIM)),
                  full((ATTN_WIDTH, d)), full((16, d)), full((ATTN_WIDTH, ATTN_WIDTH)),
                  full((1, ATTN_WIDTH)), full((1, ATTN_WIDTH)),
                  full((len(POOL_WINDOWS), POOL_GROUP, POOL_GROUP)),
                  full((1, POOL_WIDTH)), full((1, POOL_WIDTH))],
        out_specs=out_specs,
        out_shape=out_shapes,
        scratch_shapes=[pltpu.VMEM((tm + POOL_HALO, POOL_WIDTH), F32)],
        compiler_params=pltpu.CompilerParams(
            dimension_semantics=("arbitrary", "arbitrary"), vmem_limit_bytes=48 * 1024 * 1024),
        name="inproj",
    )(x, sh1, sc1, norm_g.reshape(1, d), wnat, wki, wvt, wwt, ones_bd, qg, kg,
      pool_w.astype(BF16), pool_scale.reshape(1, -1), pool_out_g.reshape(1, -1))


def _attn_kernel(q_ref, qi_ref, wt_ref, kid_ref, k_ref, kf_ref, vt_ref, aog_ref, o_ref,
                 key_ref, acc_ref, m_ref, l_ref, qim_ref, qmf_ref, cut_ref, lg_ref,
                 *, tq, ck, seq, topk):
    j = pl.program_id(1)
    nck = ((j + 1) * tq + ck - 1) // ck
    qpos = j * tq + lax.broadcasted_iota(I32, (1, tq), 1)
    qchunk = qpos // CHUNK
    row_i = lax.broadcasted_iota(I32, (ck, tq), 0)

    lane = lax.broadcasted_iota(I32, (tq, LANES), 1)
    tcol = (j * tq + lax.broadcasted_iota(I32, (tq, LANES), 0)).astype(F32)
    for h in range(N_HEADS):
        p = h // 2
        keep = (lane < HEAD_DIM) if h % 2 == 0 else (lane >= HEAD_DIM)
        qi_slab = qi_ref[0, :, p * LANES:(p + 1) * LANES]
        q_slab = q_ref[0, :, p * LANES:(p + 1) * LANES]
        qim_ref[h * tq:(h + 1) * tq, :] = jnp.where(keep, qi_slab, jnp.zeros_like(qi_slab))
        qmf_ref[h, :, 0:LANES] = jnp.where(keep, q_slab, jnp.zeros_like(q_slab))
        c1, c2, c3 = _bf16_split3(_SLOPE_L2E[h])
        u = -_SLOPE_L2E[h] * tcol
        u1 = u.astype(BF16).astype(F32)
        r = u - u1
        u2 = r.astype(BF16).astype(F32)
        u3 = (r - u2).astype(BF16).astype(F32)
        feat = jnp.zeros((tq, LANES), F32)
        for li, val in enumerate((64.0 * c1, 64.0 * c2, 64.0 * c3, c1, c2, c3, u1, u2, u3)):
            feat = jnp.where(lane == li, val, feat)
        qmf_ref[h, :, LANES:2 * LANES] = feat.astype(BF16)

    def scores(c, last):
        r0 = pl.multiple_of(c * ck, ck)
        d = _dot_nt(kid_ref[0, pl.ds(r0, ck), :], qim_ref[...])
        acc = jnp.zeros((ck, tq), F32)
        for h in range(IDX_HEADS):
            acc = acc + wt_ref[0, h:h + 1, :] * jnp.maximum(d[:, h * tq:(h + 1) * tq], 0.0)
        if last:
            acc = jnp.where(((r0 + row_i) // CHUNK) <= qchunk, acc, NEG)
        bits = pltpu.bitcast(acc, I32)
        key_ref[pl.ds(r0, ck), :] = jnp.where(bits < 0, bits ^ 0x7FFFFFFF, bits)

    def scores_body(c, carry):
        scores(c, False)
        return carry

    lax.fori_loop(0, nck - 1, scores_body, 0)
    scores(nck - 1, True)

    def count(pred):
        def body(c, c8):
            r0 = pl.multiple_of(c * ck, ck)
            m = pred(key_ref[pl.ds(r0, ck), :], r0).astype(I32)
            return c8 + m.reshape(ck // 8, 8, tq).sum(axis=0)
        c8 = lax.fori_loop(0, nck, body, jnp.zeros((8, tq), I32))
        return c8.sum(axis=0, keepdims=True)

    c_nonneg = count(lambda kk, r0: kk >= 0)
    lo0 = jnp.where(c_nonneg >= topk, 0, -(2 ** 31)).astype(I32)

    def bit_body(b, lo):
        trial = lo + jnp.left_shift(jnp.int32(1), 30 - b)
        c = count(lambda kk, r0: kk >= trial)
        return jnp.where(c >= topk, trial, lo)

    lo = lax.fori_loop(0, 31, bit_body, lo0)

    c_gt = count(lambda kk, r0: kk > lo)
    c_ge = count(lambda kk, r0: kk >= lo)
    need = topk - c_gt
    cut_ref[...] = jnp.full((1, tq), seq, I32)

    @pl.when(jnp.max(c_ge) > topk)
    def _():
        nbits = max(1, (seq - 1).bit_length())

        def idx_body(b, jv):
            trial = jv + jnp.left_shift(jnp.int32(1), nbits - 1 - b)
            c = count(lambda kk, r0: (kk == lo)
                      & ((r0 + lax.broadcasted_iota(I32, (ck, tq), 0)) < trial))
            return jnp.where(c < need, trial, jv)

        jv = lax.fori_loop(0, nbits, idx_body, jnp.zeros((1, tq), I32))
        cut_ref[...] = jv + 1

    cut = cut_ref[...]

    m_ref[...] = jnp.full((N_HEADS, 1, tq), M_INIT, F32)
    l_ref[...] = jnp.zeros((N_HEADS, 1, tq), F32)
    acc_ref[...] = jnp.zeros((ATTN_WIDTH, tq), F32)

    def attend(c, last):
        r0 = pl.multiple_of(c * ck, ck)
        kk = key_ref[pl.ds(r0, ck), :]
        kf = kf_ref[pl.ds(r0, ck), :]
        kch = k_ref[0, pl.ds(r0, ck), :]
        vtc = vt_ref[0, c]
        kpos = r0 + row_i
        sel = (kk > lo) | ((kk == lo) & (kpos < cut))
        if last:
            sel = sel & ((kpos // CHUNK) <= qchunk)
            ahead = jnp.maximum(kpos - qpos, 0).astype(F32)
        pen = jnp.where(sel, 0.0, NEG)
        for h in range(N_HEADS):
            p = h // 2
            ks = kch[:, p * LANES:(p + 1) * LANES]
            lg = _dot_nt(jnp.concatenate([ks, kf], axis=1), qmf_ref[h]) + pen
            if last:
                lg = lg - (2.0 * _SLOPE_L2E[h]) * ahead
            lg_ref[h] = lg
        for h in range(N_HEADS):
            m_old = m_ref[h]
            m_new = jnp.maximum(m_old, jnp.max(lg_ref[h], axis=0, keepdims=True))
            alpha = jnp.exp2(m_old - m_new)
            pr = jnp.exp2(lg_ref[h] - m_new)
            l_ref[h] = alpha * l_ref[h] + jnp.sum(pr, axis=0, keepdims=True)
            m_ref[h] = m_new
            rows = slice(h * HEAD_DIM, (h + 1) * HEAD_DIM)
            acc_ref[rows, :] = alpha * acc_ref[rows, :] + _dot(vtc[rows, :], pr.astype(BF16))

    def attend_body(c, carry):
        attend(c, False)
        return carry

    lax.fori_loop(0, nck - 1, attend_body, 0)
    attend(nck - 1, True)

    outs = []
    for h in range(N_HEADS):
        rows = slice(h * HEAD_DIM, (h + 1) * HEAD_DIM)
        o = acc_ref[rows, :] / l_ref[h]
        oms = jnp.mean(o * o, axis=0, keepdims=True)
        outs.append(o * lax.rsqrt(oms + EPS))
    ot = jnp.concatenate(outs, axis=0)
    o_ref[0] = (ot.T * aog_ref[...]).astype(BF16)


def _attn(q, k, qi, kid, vt, wt, attn_out_g, tq, ck):
    b, s, _ = q.shape
    topk = min(TOPK_MAX, s // 4)
    assert tq % LANES == 0 and ck % tq == 0 and s % ck == 0 and s // CHUNK <= 256
    pos = jnp.arange(s)
    feats = [pos // CHUNK] * 3 + [pos % CHUNK] * 3 + [jnp.ones_like(pos)] * 3
    kf = jnp.zeros((s, LANES), F32).at[:, 0:N_FEAT].set(
        jnp.stack(feats, axis=1).astype(F32)).astype(BF16)
    blk = lambda w: pl.BlockSpec((1, tq, w), lambda bi, ji: (bi, ji, 0))
    once = pl.Buffered(1)
    whole = lambda w: pl.BlockSpec((1, s, w), lambda bi, ji: (bi, 0, 0), pipeline_mode=once)
    return pl.pallas_call(
        functools.partial(_attn_kernel, tq=tq, ck=ck, seq=s, topk=topk),
        grid=(b, s // tq),
        in_specs=[blk(ATTN_WIDTH), blk(IDX_HEADS * IDX_DIM),
                  pl.BlockSpec((1, IDX_HEADS, tq), lambda bi, ji: (bi, 0, ji)),
                  whole(2 * IDX_DIM), whole(ATTN_WIDTH),
                  pl.BlockSpec((s, LANES), lambda bi, ji: (0, 0), pipeline_mode=once),
                  pl.BlockSpec((1, s // ck, ATTN_WIDTH, ck), lambda bi, ji: (bi, 0, 0, 0),
                               pipeline_mode=once),
                  pl.BlockSpec((1, ATTN_WIDTH), lambda bi, ji: (0, 0))],
        out_specs=blk(ATTN_WIDTH),
        out_shape=jax.ShapeDtypeStruct((b, s, ATTN_WIDTH), BF16),
        scratch_shapes=[pltpu.VMEM((s, tq), I32),
                        pltpu.VMEM((ATTN_WIDTH, tq), F32),
                        pltpu.VMEM((N_HEADS, 1, tq), F32),
                        pltpu.VMEM((N_HEADS, 1, tq), F32),
                        pltpu.VMEM((IDX_HEADS * tq, LANES), BF16),
                        pltpu.VMEM((N_HEADS, tq, 2 * LANES), BF16),
                        pltpu.VMEM((1, tq), I32),
                        pltpu.VMEM((N_HEADS, ck, tq), F32)],
        compiler_params=pltpu.CompilerParams(
            dimension_semantics=("arbitrary", "arbitrary"), vmem_limit_bytes=56 * 1024 * 1024),
        name="attn",
    )(q, qi, wt, kid, k, kf, vt, attn_out_g.reshape(1, ATTN_WIDTH))


def _outproj_kernel(pool_ref, attn_ref, x_ref, wo_ref, g1_ref, sh_ref, sc_ref, ng_ref,
                    x1_ref, h2t_ref):
    mixed = _dot(pool_ref[0], wo_ref[0:POOL_WIDTH, :]) + _dot(attn_ref[0], wo_ref[POOL_WIDTH:, :])
    x1 = x_ref[0] + g1_ref[0] * mixed
    x1_ref[0] = x1
    ms = jnp.mean(x1 * x1, axis=-1, keepdims=True)
    h2 = (x1 * lax.rsqrt(ms + EPS) * ng_ref[...]) * (1.0 + sc_ref[0]) + sh_ref[0]
    h2t_ref[...] = h2.T.astype(BF16)


def _outproj(pool_out, attn_out, x, w_out, g1, sh2, sc2, norm_g, tm):
    b, s, d = x.shape
    nt = s // tm
    tok = lambda w: pl.BlockSpec((1, tm, w), lambda bi, ti: (bi, ti, 0))
    per_b = pl.BlockSpec((1, 1, d), lambda bi, ti: (bi, 0, 0))
    return pl.pallas_call(
        _outproj_kernel,
        grid=(b, nt),
        in_specs=[tok(POOL_WIDTH), tok(ATTN_WIDTH), tok(d),
                  pl.BlockSpec((POOL_WIDTH + ATTN_WIDTH, d), lambda bi, ti: (0, 0)),
                  per_b, per_b, per_b, pl.BlockSpec((1, d), lambda bi, ti: (0, 0))],
        out_specs=(tok(d), pl.BlockSpec((d, tm), lambda bi, ti: (0, bi * nt + ti))),
        out_shape=(jax.ShapeDtypeStruct((b, s, d), F32), jax.ShapeDtypeStruct((d, b * s), BF16)),
        compiler_params=pltpu.CompilerParams(dimension_semantics=("arbitrary", "arbitrary")),
        name="outproj",
    )(pool_out, attn_out, x, w_out.astype(BF16), g1, sh2, sc2, norm_g.reshape(1, d))


def _batcher_pairs(n):
    pairs = []

    def merge(lo, cnt, r):
        step = r * 2
        if step < cnt:
            merge(lo, cnt, step)
            merge(lo + r, cnt, step)
            for i in range(lo + r, lo + cnt - r, step):
                pairs.append((i, i + r))
        else:
            pairs.append((lo, lo + r))

    def sort(lo, cnt):
        if cnt > 1:
            m = cnt // 2
            sort(lo, m)
            sort(lo + m, m)
            merge(lo, cnt, 1)

    sort(0, n)
    return tuple(pairs)


_SORT16 = _batcher_pairs(PEER_TOPK)
A_PIECE = 8


def _top16_desc(x):
    x = list(x)
    for i, j in _SORT16:
        x[i], x[j] = jnp.maximum(x[i], x[j]), jnp.minimum(x[i], x[j])
    for shift in (4, 2, 1):
        y = [jnp.maximum(x[k], pltpu.roll(x[PEER_TOPK - 1 - k], shift, 0)) for k in range(PEER_TOPK)]
        for d in (8, 4, 2, 1):
            for k in range(PEER_TOPK):
                if k & d == 0:
                    y[k], y[k + d] = jnp.maximum(y[k], y[k + d]), jnp.minimum(y[k], y[k + d])
        x = y
    return x


def _peer_kernel(h2t_ref, x1_ref, g2_ref, wqt_ref, sk_ref, u_ref, vt_ref, o_ref,
                 s1_ref, s2_ref, thr_ref, mz_ref, p_ref, acc_ref, s1b_ref, *, tm, eb, ne):
    e = pl.program_id(1)
    nsub = eb // PEER_KEYS
    ntl = tm // LANES

    @pl.when(e == 0)
    def _prologue():
        h2t = h2t_ref[...]
        sub = lax.broadcasted_iota(I32, (8, tm), 0)
        ninf = jnp.full((8, tm), -jnp.inf, F32)

        def compact(t, base):
            out = t[base + 7]
            for r in range(6, -1, -1):
                out = jnp.where(sub == r, t[base + r], out)
            return out

        def split8(st):
            return [st[k * 8:(k + 1) * 8, :] for k in range(PEER_KEYS // 8)]

        def head(hd, carry):
            q1 = _dot(wqt_ref[2 * hd], h2t).astype(BF16)
            q2 = _dot(wqt_ref[2 * hd + 1], h2t).astype(BF16)
            st1 = _dot(sk_ref[2 * hd], q1) * LOG2E
            st2 = _dot(sk_ref[2 * hd + 1], q2) * LOG2E
            s1_ref[hd] = st1
            for tl in range(ntl):
                s2_ref[hd, tl] = st2[:, tl * LANES:(tl + 1) * LANES]
            t1 = _top16_desc(split8(st1))
            t2 = _top16_desc(split8(st2))
            a2lo, a2hi, a1hi = compact(t2, 0), compact(t2, 8), compact(t1, 8)
            cands = [t1[0] + a2lo, t1[0] + a2hi, t1[1] + a2lo]
            for r1 in range(2, 8):
                cands.append(jnp.where(sub < PEER_TOPK // (r1 + 1), t1[r1] + a2lo, ninf))
            cands.append(a1hi + t2[0])
            thr = _top16_desc(cands + [ninf] * (PEER_TOPK - len(cands)))[PEER_TOPK - 1]
            m = t1[0] + t2[0]
            z8 = jnp.zeros((8, tm), F32)
            for c in cands:
                z8 = z8 + jnp.where(c >= thr, jnp.exp2(c - m), 0.0)
            z = jnp.sum(z8, axis=0, keepdims=True)
            thr_ref[pl.ds(hd, 1), :] = thr[0:1, :]
            mz_ref[pl.ds(hd, 1), :] = m[0:1, :] + jnp.log2(z)
            return carry

        lax.fori_loop(0, PEER_HEADS, head, 0)
        acc_ref[...] = jnp.zeros_like(acc_ref)

    i0 = pl.multiple_of(e * nsub, nsub)
    s1rows = [s1_ref[hd, pl.ds(i0, nsub), :] for hd in range(PEER_HEADS)]
    for hd in range(PEER_HEADS):
        s1b_ref[hd] = s1rows[hd]
    for ii in range(nsub):
        rows = slice(ii * PEER_KEYS, (ii + 1) * PEER_KEYS)
        if ii % A_PIECE == 0:
            a_piece = _dot(u_ref[ii * PEER_KEYS:(ii + A_PIECE) * PEER_KEYS, :], h2t_ref[...])
        a_rows = a_piece[(ii % A_PIECE) * PEER_KEYS:(ii % A_PIECE + 1) * PEER_KEYS, :]
        for tl in range(ntl):
            cols = slice(tl * LANES, (tl + 1) * LANES)
            w = jnp.zeros((PEER_KEYS, LANES), F32)
            for hd in range(PEER_HEADS):
                sm = s1b_ref[hd, ii:ii + 1, cols] + s2_ref[hd, tl]
                gate = jnp.exp2(sm - mz_ref[hd:hd + 1, cols])
                w = w + jnp.where(sm >= thr_ref[hd:hd + 1, cols], gate, 0.0)
            a = a_rows[:, cols]
            gl = 0.5 * a * (1.0 + lax.erf(a * (1.0 / math.sqrt(2.0))))
            p_ref[rows, cols] = (gl * w).astype(BF16)
    acc_ref[...] += _dot(vt_ref[...], p_ref[...])

    @pl.when(e == ne - 1)
    def _epilogue():
        o_ref[...] = x1_ref[...] + g2_ref[0] * acc_ref[...].T


def _peer(h2t, x1, g2, wq, sub_keys, u_tab, v_tab, seq, tm, eb):
    d, t = h2t.shape
    ne = u_tab.shape[0] // eb
    half = PEER_KEYS
    wqt = wq.T.reshape(2 * PEER_HEADS, half, d).astype(BF16)
    sk = sub_keys.reshape(2 * PEER_HEADS, PEER_KEYS, half).astype(BF16)
    ub = u_tab.astype(BF16)
    vtb = v_tab.T.astype(BF16)
    hs = (PEER_HEADS, PEER_KEYS, tm)
    return pl.pallas_call(
        functools.partial(_peer_kernel, tm=tm, eb=eb, ne=ne),
        grid=(t // tm, ne),
        in_specs=[pl.BlockSpec((d, tm), lambda i, e: (0, i)),
                  pl.BlockSpec((tm, d), lambda i, e: (i, 0)),
                  pl.BlockSpec((1, 1, d), lambda i, e: ((i * tm) // seq, 0, 0)),
                  pl.BlockSpec((2 * PEER_HEADS, half, d), lambda i, e: (0, 0, 0)),
                  pl.BlockSpec((2 * PEER_HEADS, PEER_KEYS, half), lambda i, e: (0, 0, 0)),
                  pl.BlockSpec((eb, d), lambda i, e: (e, 0)),
                  pl.BlockSpec((d, eb), lambda i, e: (0, e))],
        out_specs=pl.BlockSpec((tm, d), lambda i, e: (i, 0)),
        out_shape=jax.ShapeDtypeStruct((t, d), F32),
        scratch_shapes=[pltpu.VMEM(hs, F32),
                        pltpu.VMEM((PEER_HEADS, tm // LANES, PEER_KEYS, LANES), F32),
                        pltpu.VMEM((PEER_HEADS, tm), F32),
                        pltpu.VMEM((PEER_HEADS, tm), F32),
                        pltpu.VMEM((eb, tm), BF16),
                        pltpu.VMEM((d, tm), F32),
                        pltpu.VMEM((PEER_HEADS, eb // PEER_KEYS, tm), F32)],
        compiler_params=pltpu.CompilerParams(
            dimension_semantics=("arbitrary", "arbitrary"), vmem_limit_bytes=58 * 1024 * 1024),
        name="peer",
    )(h2t, x1, g2, wqt, sk, ub, vtb)


def _tile(n, pref):
    t = min(pref, n)
    assert n % t == 0
    return t


def kernel(x, c, ada_w, ada_b, norm1_g, w_in, pool_w, pool_scale, pool_out_g, q_norm_g, k_norm_g,
           attn_out_g, w_out, norm2_g, peer_wq, peer_subkeys, peer_u, peer_v):
    b, s, d = x.shape
    depth = ada_w.shape[0]
    tm = _tile(s, 512)
    tq = _tile(s, 2 * LANES)
    ck = _tile(tm, 512)
    tp = _tile(s, 512)
    eb = 1024
    for l in range(depth):
        mod = _ada(c, ada_w[l], ada_b[l]).reshape(b, 1, 6 * d)
        sh1, sc1, g1, sh2, sc2, g2 = jnp.split(mod, 6, axis=-1)
        q, k, qi, kid, vt, wt, pool_out = _inproj(
            x, sh1, sc1, norm1_g[l], w_in[l], pool_w[l], pool_scale[l], pool_out_g[l],
            q_norm_g[l], k_norm_g[l], tm, ck)
        attn_out = _attn(q, k, qi, kid, vt, wt, attn_out_g[l], tq, ck)
        x1, h2t = _outproj(pool_out, attn_out, x, w_out[l], g1, sh2, sc2, norm2_g[l], tm)
        x = _peer(h2t, x1.reshape(b * s, d), g2, peer_wq[l], peer_subkeys[l],
                  peer_u[l], peer_v[l], s, tp, eb).reshape(b, s, d)
    return x
```

```python
import functools
import math

import ml_dtypes
import numpy as np

import jax
import jax.numpy as jnp
from jax import lax
from jax.experimental import pallas as pl
from jax.experimental.pallas import tpu as pltpu

F32 = jnp.float32
BF16 = jnp.bfloat16
I32 = jnp.int32

CHUNK = 64
EPS = 1e-6
POOL_WINDOWS = (2, 4, 8, 16)
POOL_GROUP = 128
POOL_WIDTH = 512
ATTN_WIDTH = 512
HEAD_DIM = 64
N_HEADS = 8
IDX_HEADS = 8
IDX_DIM = 64
TOPK_MAX = 256
PEER_HEADS = 8
PEER_KEYS = 128
PEER_TOPK = 16
NEG = -1e30

LANES = 128
POOL_HALO = 16
LOG2E = math.log2(math.e)
M_INIT = 0.5 * NEG
N_FEAT = 9

NT_DIMS = (((1,), (1,)), ((), ()))


def _dot(a, b):
    return jnp.dot(a, b, preferred_element_type=F32)


def _dot_nt(a, b):
    return lax.dot_general(a, b, NT_DIMS, preferred_element_type=F32)


def _f32_key(v):
    b = int(np.float32(v).view(np.int32))
    return b ^ 0x7FFFFFFF if b < 0 else b


def _bf16_split3(c):
    c = np.float32(c)
    p1 = np.float32(c.astype(ml_dtypes.bfloat16))
    r = np.float32(c - p1)
    p2 = np.float32(r.astype(ml_dtypes.bfloat16))
    p3 = np.float32(np.float32(r - p2).astype(ml_dtypes.bfloat16))
    return float(p1), float(p2), float(p3)


_SLOPE_L2E = tuple(float(np.float32(2.0 ** (-8.0 * (h + 1) / N_HEADS) * LOG2E)) for h in range(N_HEADS))


def _ada_kernel(c_ref, w_ref, b_ref, o_ref):
    c = c_ref[...]
    cond = c * jax.nn.sigmoid(c)
    o_ref[...] = jnp.dot(cond, w_ref[...], preferred_element_type=F32,
                         precision=lax.Precision.HIGHEST) + b_ref[...]


def _ada(c, w, b):
    bsz, d = c.shape
    n = w.shape[1]
    tn = 1024
    return pl.pallas_call(
        _ada_kernel,
        grid=(n // tn,),
        in_specs=[pl.BlockSpec((bsz, d), lambda j: (0, 0)),
                  pl.BlockSpec((d, tn), lambda j: (0, j)),
                  pl.BlockSpec((1, tn), lambda j: (0, j))],
        out_specs=pl.BlockSpec((bsz, tn), lambda j: (0, j)),
        out_shape=jax.ShapeDtypeStruct((bsz, n), F32),
        name="ada",
    )(c, w, b.reshape(1, n))


def _group_sumsq(t, ones_bd):
    sq = t * t
    hi = sq.astype(BF16)
    lo = (sq - hi.astype(F32)).astype(BF16)
    return _dot(hi, ones_bd) + _dot(lo, ones_bd)


def _inproj_kernel(x_ref, sh_ref, sc_ref, g_ref, wnat_ref, wki_ref, wvt_ref, wwt_ref, ones_ref,
                   qg_ref, kg_ref, poolw_ref, pscale_ref, pg_ref,
                   q_ref, k_ref, qi_ref, kid_ref, vt_ref, wt_ref, pool_ref,
                   ubuf, *, tm, kc):
    i = pl.program_id(1)
    x = x_ref[0]
    ms = jnp.mean(x * x, axis=-1, keepdims=True)
    h = (x * lax.rsqrt(ms + EPS) * g_ref[...]) * (1.0 + sc_ref[0]) + sh_ref[0]
    hb = h.astype(BF16)

    proj = _dot(hb, wnat_ref[...])
    u = proj[:, 0:512]
    q = proj[:, 512:1024]
    k = proj[:, 1024:1536]
    qi = proj[:, 1536:2048]

    ones_bd = ones_ref[...]
    qn = q * lax.rsqrt(_group_sumsq(q, ones_bd) * (1.0 / HEAD_DIM) + EPS) * qg_ref[...]
    kn = k * lax.rsqrt(_group_sumsq(k, ones_bd) * (1.0 / HEAD_DIM) + EPS) * kg_ref[...]
    q_ref[0] = qn.astype(BF16)
    k_ref[0] = kn.astype(BF16)
    qi_ref[0] = (qi * (IDX_DIM ** -0.5)).astype(BF16)
    kid_ref[0] = _dot(hb, wki_ref[...]).astype(BF16)
    vt = _dot_nt(wvt_ref[...], hb).astype(BF16)
    for ci in range(tm // kc):
        vt_ref[0, ci] = vt[:, ci * kc:(ci + 1) * kc]
    wt = _dot_nt(wwt_ref[...], hb)
    wt_ref[0] = wt[0:IDX_HEADS, :] * (IDX_HEADS ** -0.5)

    @pl.when(i == 0)
    def _():
        ubuf[0:POOL_HALO, :] = jnp.zeros((POOL_HALO, POOL_WIDTH), F32)

    ubuf[POOL_HALO:POOL_HALO + tm, :] = u
    tpos = i * tm + lax.broadcasted_iota(I32, (tm, POOL_GROUP), 0)
    parts = []
    for gi, win in enumerate(POOL_WINDOWS):
        lo_l, hi_l = gi * POOL_GROUP, (gi + 1) * POOL_GROUP
        ws = ubuf[POOL_HALO:POOL_HALO + tm, lo_l:hi_l]
        for j in range(1, win):
            ws = ws + ubuf[POOL_HALO - j:POOL_HALO - j + tm, lo_l:hi_l]
        cnt = jnp.minimum(tpos + 1, win).astype(F32)
        pooled = ws / cnt - u[:, lo_l:hi_l]
        parts.append(_dot(pooled.astype(BF16), poolw_ref[gi]))
    mixed = jnp.concatenate(parts, axis=-1) * pscale_ref[...]
    pms = jnp.mean(mixed * mixed, axis=-1, keepdims=True)
    pool_ref[0] = (mixed * lax.rsqrt(pms + EPS) * pg_ref[...]).astype(BF16)
    ubuf[0:POOL_HALO, :] = ubuf[tm:tm + POOL_HALO, :]


def _inproj(x, sh1, sc1, norm_g, w_in, pool_w, pool_scale, pool_out_g, q_norm_g, k_norm_g, tm, kc):
    b, s, d = x.shape
    nt = s // tm
    w_u, w_q, w_k, w_v, w_qi, w_ki, w_wi = jnp.split(
        w_in, (512, 1024, 1536, 2048, 2560, 2624), axis=-1)
    wnat = jnp.concatenate([w_u, w_q, w_k, w_qi], axis=-1).astype(BF16)
    wki = jnp.concatenate([w_ki, w_ki], axis=-1).astype(BF16)
    wvt = w_v.T.astype(BF16)
    wwt = jnp.concatenate([w_wi.T, jnp.zeros((16 - IDX_HEADS, d), F32)], axis=0).astype(BF16)
    gid = jnp.arange(ATTN_WIDTH) // HEAD_DIM
    ones_bd = (gid[:, None] == gid[None, :]).astype(BF16)
    qg = (jnp.tile(q_norm_g, N_HEADS) * (HEAD_DIM ** -0.5 * LOG2E)).reshape(1, ATTN_WIDTH)
    kg = jnp.tile(k_norm_g, N_HEADS).reshape(1, ATTN_WIDTH)

    full = lambda shape: pl.BlockSpec(shape, lambda bi, ti: (0,) * len(shape))
    tok = lambda w: pl.BlockSpec((1, tm, w), lambda bi, ti: (bi, ti, 0))
    per_b = pl.BlockSpec((1, 1, d), lambda bi, ti: (bi, 0, 0))
    out_shapes = (
        jax.ShapeDtypeStruct((b, s, ATTN_WIDTH), BF16),
        jax.ShapeDtypeStruct((b, s, ATTN_WIDTH), BF16),
        jax.ShapeDtypeStruct((b, s, IDX_HEADS * IDX_DIM), BF16),
        jax.ShapeDtypeStruct((b, s, 2 * IDX_DIM), BF16),
        jax.ShapeDtypeStruct((b, s // kc, ATTN_WIDTH, kc), BF16),
        jax.ShapeDtypeStruct((b, IDX_HEADS, s), F32),
        jax.ShapeDtypeStruct((b, s, POOL_WIDTH), BF16),
    )
    out_specs = (
        tok(ATTN_WIDTH), tok(ATTN_WIDTH), tok(IDX_HEADS * IDX_DIM), tok(2 * IDX_DIM),
        pl.BlockSpec((1, tm // kc, ATTN_WIDTH, kc), lambda bi, ti: (bi, ti, 0, 0)),
        pl.BlockSpec((1, IDX_HEADS, tm), lambda bi, ti: (bi, 0, ti)),
        tok(POOL_WIDTH),
    )
    return pl.pallas_call(
        functools.partial(_inproj_kernel, tm=tm, kc=kc),
        grid=(b, nt),
        in_specs=[tok(d), per_b, per_b, full((1, d)), full((d, 2048)), full((d, 2 * IDX_DIM)),
                  full((ATTN_WIDTH, d)), full((16, d)), full((ATTN_WIDTH, ATTN_WIDTH)),
                  full((1, ATTN_WIDTH)), full((1, ATTN_WIDTH)),
                  full((len(POOL_WINDOWS), POOL_GROUP, POOL_GROUP)),
                  full((1, POOL_WIDTH)), full((1, POOL_WIDTH))],
        out_specs=out_specs,
        out_shape=out_shapes,
        scratch_shapes=[pltpu.VMEM((tm + POOL_HALO, POOL_WIDTH), F32)],
        compiler_params=pltpu.CompilerParams(
            dimension_semantics=("arbitrary", "arbitrary"), vmem_limit_bytes=48 * 1024 * 1024),
        name="inproj",
    )(x, sh1, sc1, norm_g.reshape(1, d), wnat, wki, wvt, wwt, ones_bd, qg, kg,
      pool_w.astype(BF16), pool_scale.reshape(1, -1), pool_out_g.reshape(1, -1))


def _attn_kernel(q_ref, qi_ref, wt_ref, kid_ref, k_ref, kf_ref, vt_ref, aog_ref, o_ref,
                 key_ref, acc_ref, m_ref, l_ref, qim_ref, qmf_ref, cut_ref, lg_ref,
                 *, tq, ck, seq, topk):
    j = pl.program_id(1)
    nck = ((j + 1) * tq + ck - 1) // ck
    qpos = j * tq + lax.broadcasted_iota(I32, (1, tq), 1)
    qchunk = qpos // CHUNK
    row_i = lax.broadcasted_iota(I32, (ck, tq), 0)

    lane = lax.broadcasted_iota(I32, (tq, LANES), 1)
    tcol = (j * tq + lax.broadcasted_iota(I32, (tq, LANES), 0)).astype(F32)
    for h in range(N_HEADS):
        p = h // 2
        keep = (lane < HEAD_DIM) if h % 2 == 0 else (lane >= HEAD_DIM)
        qi_slab = qi_ref[0, :, p * LANES:(p + 1) * LANES]
        q_slab = q_ref[0, :, p * LANES:(p + 1) * LANES]
        qim_ref[h * tq:(h + 1) * tq, :] = jnp.where(keep, qi_slab, jnp.zeros_like(qi_slab))
        qmf_ref[h, :, 0:LANES] = jnp.where(keep, q_slab, jnp.zeros_like(q_slab))
        c1, c2, c3 = _bf16_split3(_SLOPE_L2E[h])
        u = -_SLOPE_L2E[h] * tcol
        u1 = u.astype(BF16).astype(F32)
        r = u - u1
        u2 = r.astype(BF16).astype(F32)
        u3 = (r - u2).astype(BF16).astype(F32)
        feat = jnp.zeros((tq, LANES), F32)
        for li, val in enumerate((64.0 * c1, 64.0 * c2, 64.0 * c3, c1, c2, c3, u1, u2, u3)):
            feat = jnp.where(lane == li, val, feat)
        qmf_ref[h, :, LANES:2 * LANES] = feat.astype(BF16)

    def scores(c, last):
        r0 = pl.multiple_of(c * ck, ck)
        d = _dot_nt(kid_ref[0, pl.ds(r0, ck), :], qim_ref[...])
        acc = jnp.zeros((ck, tq), F32)
        for h in range(IDX_HEADS):
            acc = acc + wt_ref[0, h:h + 1, :] * jnp.maximum(d[:, h * tq:(h + 1) * tq], 0.0)
        if last:
            acc = jnp.where(((r0 + row_i) // CHUNK) <= qchunk, acc, NEG)
        bits = pltpu.bitcast(acc, I32)
        key_ref[pl.ds(r0, ck), :] = jnp.where(bits < 0, bits ^ 0x7FFFFFFF, bits)

    def scores_body(c, carry):
        scores(c, False)
        return carry

    lax.fori_loop(0, nck - 1, scores_body, 0)
    scores(nck - 1, True)

    def count(pred):
        def body(c, c8):
            r0 = pl.multiple_of(c * ck, ck)
            m = pred(key_ref[pl.ds(r0, ck), :], r0).astype(I32)
            return c8 + m.reshape(ck // 8, 8, tq).sum(axis=0)
        c8 = lax.fori_loop(0, nck, body, jnp.zeros((8, tq), I32))
        return c8.sum(axis=0, keepdims=True)

    c_nonneg = count(lambda kk, r0: kk >= 0)
    lo0 = jnp.where(c_nonneg >= topk, 0, -(2 ** 31)).astype(I32)

    def bit_body(b, lo):
        trial = lo + jnp.left_shift(jnp.int32(1), 30 - b)
        c = count(lambda kk, r0: kk >= trial)
        return jnp.where(c >= topk, trial, lo)

    lo = lax.fori_loop(0, 31, bit_body, lo0)

    c_gt = count(lambda kk, r0: kk > lo)
    c_ge = count(lambda kk, r0: kk >= lo)
    need = topk - c_gt
    cut_ref[...] = jnp.full((1, tq), seq, I32)

    @pl.when(jnp.max(c_ge) > topk)
    def _():
        nbits = max(1, (seq - 1).bit_length())

        def idx_body(b, jv):
            trial = jv + jnp.left_shift(jnp.int32(1), nbits - 1 - b)
            c = count(lambda kk, r0: (kk == lo)
                      & ((r0 + lax.broadcasted_iota(I32, (ck, tq), 0)) < trial))
            return jnp.where(c < need, trial, jv)

        jv = lax.fori_loop(0, nbits, idx_body, jnp.zeros((1, tq), I32))
        cut_ref[...] = jv + 1

    cut = cut_ref[...]

    m_ref[...] = jnp.full((N_HEADS, 1, tq), M_INIT, F32)
    l_ref[...] = jnp.zeros((N_HEADS, 1, tq), F32)
    acc_ref[...] = jnp.zeros((ATTN_WIDTH, tq), F32)

    def attend(c, last):
        r0 = pl.multiple_of(c * ck, ck)
        kk = key_ref[pl.ds(r0, ck), :]
        kf = kf_ref[pl.ds(r0, ck), :]
        kch = k_ref[0, pl.ds(r0, ck), :]
        vtc = vt_ref[0, c]
        kpos = r0 + row_i
        sel = (kk > lo) | ((kk == lo) & (kpos < cut))
        if last:
            sel = sel & ((kpos // CHUNK) <= qchunk)
            ahead = jnp.maximum(kpos - qpos, 0).astype(F32)
        pen = jnp.where(sel, 0.0, NEG)
        for h in range(N_HEADS):
            p = h // 2
            ks = kch[:, p * LANES:(p + 1) * LANES]
            lg = _dot_nt(jnp.concatenate([ks, kf], axis=1), qmf_ref[h]) + pen
            if last:
                lg = lg - (2.0 * _SLOPE_L2E[h]) * ahead
            lg_ref[h] = lg
        for h in range(N_HEADS):
            m_old = m_ref[h]
            m_new = jnp.maximum(m_old, jnp.max(lg_ref[h], axis=0, keepdims=True))
            alpha = jnp.exp2(m_old - m_new)
            pr = jnp.exp2(lg_ref[h] - m_new)
            l_ref[h] = alpha * l_ref[h] + jnp.sum(pr, axis=0, keepdims=True)
            m_ref[h] = m_new
            rows = slice(h * HEAD_DIM, (h + 1) * HEAD_DIM)
            acc_ref[rows, :] = alpha * acc_ref[rows, :] + _dot(vtc[rows, :], pr.astype(BF16))

    def attend_body(c, carry):
        attend(c, False)
        return carry

    lax.fori_loop(0, nck - 1, attend_body, 0)
    attend(nck - 1, True)

    outs = []
    for h in range(N_HEADS):
        rows = slice(h * HEAD_DIM, (h + 1) * HEAD_DIM)
        o = acc_ref[rows, :] / l_ref[h]
        oms = jnp.mean(o * o, axis=0, keepdims=True)
        outs.append(o * lax.rsqrt(oms + EPS))
    ot = jnp.concatenate(outs, axis=0)
    o_ref[0] = (ot.T * aog_ref[...]).astype(BF16)


def _attn(q, k, qi, kid, vt, wt, attn_out_g, tq, ck):
    b, s, _ = q.shape
    topk = min(TOPK_MAX, s // 4)
    assert tq % LANES == 0 and ck % tq == 0 and s % ck == 0 and s // CHUNK <= 256
    pos = jnp.arange(s)
    feats = [pos // CHUNK] * 3 + [pos % CHUNK] * 3 + [jnp.ones_like(pos)] * 3
    kf = jnp.zeros((s, LANES), F32).at[:, 0:N_FEAT].set(
        jnp.stack(feats, axis=1).astype(F32)).astype(BF16)
    blk = lambda w: pl.BlockSpec((1, tq, w), lambda bi, ji: (bi, ji, 0))
    once = pl.Buffered(1)
    whole = lambda w: pl.BlockSpec((1, s, w), lambda bi, ji: (bi, 0, 0), pipeline_mode=once)
    return pl.pallas_call(
        functools.partial(_attn_kernel, tq=tq, ck=ck, seq=s, topk=topk),
        grid=(b, s // tq),
        in_specs=[blk(ATTN_WIDTH), blk(IDX_HEADS * IDX_DIM),
                  pl.BlockSpec((1, IDX_HEADS, tq), lambda bi, ji: (bi, 0, ji)),
                  whole(2 * IDX_DIM), whole(ATTN_WIDTH),
                  pl.BlockSpec((s, LANES), lambda bi, ji: (0, 0), pipeline_mode=once),
                  pl.BlockSpec((1, s // ck, ATTN_WIDTH, ck), lambda bi, ji: (bi, 0, 0, 0),
                               pipeline_mode=once),
                  pl.BlockSpec((1, ATTN_WIDTH), lambda bi, ji: (0, 0))],
        out_specs=blk(ATTN_WIDTH),
        out_shape=jax.ShapeDtypeStruct((b, s, ATTN_WIDTH), BF16),
        scratch_shapes=[pltpu.VMEM((s, tq), I32),
                        pltpu.VMEM((ATTN_WIDTH, tq), F32),
                        pltpu.VMEM((N_HEADS, 1, tq), F32),
                        pltpu.VMEM((N_HEADS, 1, tq), F32),
                        pltpu.VMEM((IDX_HEADS * tq, LANES), BF16),
                        pltpu.VMEM((N_HEADS, tq, 2 * LANES), BF16),
                        pltpu.VMEM((1, tq), I32),
                        pltpu.VMEM((N_HEADS, ck, tq), F32)],
        compiler_params=pltpu.CompilerParams(
            dimension_semantics=("arbitrary", "arbitrary"), vmem_limit_bytes=56 * 1024 * 1024),
        name="attn",
    )(q, qi, wt, kid, k, kf, vt, attn_out_g.reshape(1, ATTN_WIDTH))


def _outproj_kernel(pool_ref, attn_ref, x_ref, wo_ref, g1_ref, sh_ref, sc_ref, ng_ref,
                    x1_ref, h2t_ref):
    mixed = _dot(pool_ref[0], wo_ref[0:POOL_WIDTH, :]) + _dot(attn_ref[0], wo_ref[POOL_WIDTH:, :])
    x1 = x_ref[0] + g1_ref[0] * mixed
    x1_ref[0] = x1
    ms = jnp.mean(x1 * x1, axis=-1, keepdims=True)
    h2 = (x1 * lax.rsqrt(ms + EPS) * ng_ref[...]) * (1.0 + sc_ref[0]) + sh_ref[0]
    h2t_ref[...] = h2.T.astype(BF16)


def _outproj(pool_out, attn_out, x, w_out, g1, sh2, sc2, norm_g, tm):
    b, s, d = x.shape
    nt = s // tm
    tok = lambda w: pl.BlockSpec((1, tm, w), lambda bi, ti: (bi, ti, 0))
    per_b = pl.BlockSpec((1, 1, d), lambda bi, ti: (bi, 0, 0))
    return pl.pallas_call(
        _outproj_kernel,
        grid=(b, nt),
        in_specs=[tok(POOL_WIDTH), tok(ATTN_WIDTH), tok(d),
                  pl.BlockSpec((POOL_WIDTH + ATTN_WIDTH, d), lambda bi, ti: (0, 0)),
                  per_b, per_b, per_b, pl.BlockSpec((1, d), lambda bi, ti: (0, 0))],
        out_specs=(tok(d), pl.BlockSpec((d, tm), lambda bi, ti: (0, bi * nt + ti))),
        out_shape=(jax.ShapeDtypeStruct((b, s, d), F32), jax.ShapeDtypeStruct((d, b * s), BF16)),
        compiler_params=pltpu.CompilerParams(dimension_semantics=("arbitrary", "arbitrary")),
        name="outproj",
    )(pool_out, attn_out, x, w_out.astype(BF16), g1, sh2, sc2, norm_g.reshape(1, d))


def _batcher_pairs(n):
    pairs = []

    def merge(lo, cnt, r):
        step = r * 2
        if step < cnt:
            merge(lo, cnt, step)
            merge(lo + r, cnt, step)
            for i in range(lo + r, lo + cnt - r, step):
                pairs.append((i, i + r))
        else:
            pairs.append((lo, lo + r))

    def sort(lo, cnt):
        if cnt > 1:
            m = cnt // 2
            sort(lo, m)
            sort(lo + m, m)
            merge(lo, cnt, 1)

    sort(0, n)
    return tuple(pairs)


_SORT16 = _batcher_pairs(PEER_TOPK)
MM_GROUP = 8


def _top16_desc(x):
    x = list(x)
    for i, j in _SORT16:
        x[i], x[j] = jnp.maximum(x[i], x[j]), jnp.minimum(x[i], x[j])
    for shift in (4, 2, 1):
        y = [jnp.maximum(x[k], pltpu.roll(x[PEER_TOPK - 1 - k], shift, 0)) for k in range(PEER_TOPK)]
        for d in (8, 4, 2, 1):
            for k in range(PEER_TOPK):
                if k & d == 0:
                    y[k], y[k + d] = jnp.maximum(y[k], y[k + d]), jnp.minimum(y[k], y[k + d])
        x = y
    return x


def _peer_kernel(h2t_ref, x1_ref, g2_ref, wqt_ref, sk_ref, u_ref, vt_ref, o_ref,
                 s1_ref, s2_ref, thr_ref, mz_ref, p_ref, acc_ref, s1b_ref, *, tm, eb, ne):
    e = pl.program_id(1)
    nsub = eb // PEER_KEYS
    ntl = tm // LANES

    @pl.when(e == 0)
    def _prologue():
        h2t = h2t_ref[...]
        sub = lax.broadcasted_iota(I32, (8, tm), 0)
        ninf = jnp.full((8, tm), -jnp.inf, F32)

        def compact(t, base):
            out = t[base + 7]
            for r in range(6, -1, -1):
                out = jnp.where(sub == r, t[base + r], out)
            return out

        def split8(st):
            return [st[k * 8:(k + 1) * 8, :] for k in range(PEER_KEYS // 8)]

        def head(hd, carry):
            q1 = _dot(wqt_ref[2 * hd], h2t).astype(BF16)
            q2 = _dot(wqt_ref[2 * hd + 1], h2t).astype(BF16)
            st1 = _dot(sk_ref[2 * hd], q1) * LOG2E
            st2 = _dot(sk_ref[2 * hd + 1], q2) * LOG2E
            s1_ref[hd] = st1
            for tl in range(ntl):
                s2_ref[hd, tl] = st2[:, tl * LANES:(tl + 1) * LANES]
            t1 = _top16_desc(split8(st1))
            t2 = _top16_desc(split8(st2))
            a2lo, a2hi, a1hi = compact(t2, 0), compact(t2, 8), compact(t1, 8)
            cands = [t1[0] + a2lo, t1[0] + a2hi, t1[1] + a2lo]
            for r1 in range(2, 8):
                cands.append(jnp.where(sub < PEER_TOPK // (r1 + 1), t1[r1] + a2lo, ninf))
            cands.append(a1hi + t2[0])
            thr = _top16_desc(cands + [ninf] * (PEER_TOPK - len(cands)))[PEER_TOPK - 1]
            m = t1[0] + t2[0]
            z8 = jnp.zeros((8, tm), F32)
            for c in cands:
                z8 = z8 + jnp.where(c >= thr, jnp.exp2(c - m), 0.0)
            z = jnp.sum(z8, axis=0, keepdims=True)
            thr_ref[pl.ds(hd, 1), :] = thr[0:1, :]
            mz_ref[pl.ds(hd, 1), :] = m[0:1, :] + jnp.log2(z)
            return carry

        lax.fori_loop(0, PEER_HEADS, head, 0)
        acc_ref[...] = jnp.zeros_like(acc_ref)

    i0 = pl.multiple_of(e * nsub, nsub)
    s1rows = [s1_ref[hd, pl.ds(i0, nsub), :] for hd in range(PEER_HEADS)]
    for hd in range(PEER_HEADS):
        s1b_ref[hd] = s1rows[hd]
    gsz = MM_GROUP * PEER_KEYS
    for g in range(nsub // MM_GROUP):
        grows = slice(g * gsz, (g + 1) * gsz)
        a_grp = _dot(u_ref[grows, :], h2t_ref[...])
        for sg in range(MM_GROUP):
            ii = g * MM_GROUP + sg
            rows = slice(ii * PEER_KEYS, (ii + 1) * PEER_KEYS)
            for tl in range(ntl):
                cols = slice(tl * LANES, (tl + 1) * LANES)
                w = jnp.zeros((PEER_KEYS, LANES), F32)
                for hd in range(PEER_HEADS):
                    sm = s1b_ref[hd, ii:ii + 1, cols] + s2_ref[hd, tl]
                    gate = jnp.exp2(sm - mz_ref[hd:hd + 1, cols])
                    w = w + jnp.where(sm >= thr_ref[hd:hd + 1, cols], gate, 0.0)
                a = a_grp[sg * PEER_KEYS:(sg + 1) * PEER_KEYS, cols]
                gl = 0.5 * a * (1.0 + lax.erf(a * (1.0 / math.sqrt(2.0))))
                p_ref[rows, cols] = (gl * w).astype(BF16)
        acc_ref[...] += _dot(vt_ref[:, grows], p_ref[grows, :])

    @pl.when(e == ne - 1)
    def _epilogue():
        o_ref[...] = x1_ref[...] + g2_ref[0] * acc_ref[...].T


def _peer(h2t, x1, g2, wq, sub_keys, u_tab, v_tab, seq, tm, eb):
    d, t = h2t.shape
    ne = u_tab.shape[0] // eb
    half = PEER_KEYS
    wqt = wq.T.reshape(2 * PEER_HEADS, half, d).astype(BF16)
    sk = sub_keys.reshape(2 * PEER_HEADS, PEER_KEYS, half).astype(BF16)
    ub = u_tab.astype(BF16)
    vtb = v_tab.T.astype(BF16)
    hs = (PEER_HEADS, PEER_KEYS, tm)
    return pl.pallas_call(
        functools.partial(_peer_kernel, tm=tm, eb=eb, ne=ne),
        grid=(t // tm, ne),
        in_specs=[pl.BlockSpec((d, tm), lambda i, e: (0, i)),
                  pl.BlockSpec((tm, d), lambda i, e: (i, 0)),
                  pl.BlockSpec((1, 1, d), lambda i, e: ((i * tm) // seq, 0, 0)),
                  pl.BlockSpec((2 * PEER_HEADS, half, d), lambda i, e: (0, 0, 0)),
                  pl.BlockSpec((2 * PEER_HEADS, PEER_KEYS, half), lambda i, e: (0, 0, 0)),
                  pl.BlockSpec((eb, d), lambda i, e: (e, 0)),
                  pl.BlockSpec((d, eb), lambda i, e: (0, e))],
        out_specs=pl.BlockSpec((tm, d), lambda i, e: (i, 0)),
        out_shape=jax.ShapeDtypeStruct((t, d), F32),
        scratch_shapes=[pltpu.VMEM(hs, F32),
                        pltpu.VMEM((PEER_HEADS, tm // LANES, PEER_KEYS, LANES), F32),
                        pltpu.VMEM((PEER_HEADS, tm), F32),
                        pltpu.VMEM((PEER_HEADS, tm), F32),
                        pltpu.VMEM((eb, tm), BF16),
                        pltpu.VMEM((d, tm), F32),
                        pltpu.VMEM((PEER_HEADS, eb // PEER_KEYS, tm), F32)],
        compiler_params=pltpu.CompilerParams(
            dimension_semantics=("arbitrary", "arbitrary"), vmem_limit_bytes=58 * 1024 * 1024),
        name="peer",
    )(h2t, x1, g2, wqt, sk, ub, vtb)


def _tile(n, pref):
    t = min(pref, n)
    assert n % t == 0
    return t


def kernel(x, c, ada_w, ada_b, norm1_g, w_in, pool_w, pool_scale, pool_out_g, q_norm_g, k_norm_g,
           attn_out_g, w_out, norm2_g, peer_wq, peer_subkeys, peer_u, peer_v):
    b, s, d = x.shape
    depth = ada_w.shape[0]
    tm = _tile(s, 512)
    tq = _tile(s, 2 * LANES)
    ck = _tile(tm, 512)
    tp = _tile(s, 512)
    eb = 2048
    for l in range(depth):
        mod = _ada(c, ada_w[l], ada_b[l]).reshape(b, 1, 6 * d)
        sh1, sc1, g1, sh2, sc2, g2 = jnp.split(mod, 6, axis=-1)
        q, k, qi, kid, vt, wt, pool_out = _inproj(
            x, sh1, sc1, norm1_g[l], w_in[l], pool_w[l], pool_scale[l], pool_out_g[l],
            q_norm_g[l], k_norm_g[l], tm, ck)
        attn_out = _attn(q, k, qi, kid, vt, wt, attn_out_g[l], tq, ck)
        x1, h2t = _outproj(pool_out, attn_out, x, w_out[l], g1, sh2, sc2, norm2_g[l], tm)
        x = _peer(h2t, x1.reshape(b * s, d), g2, peer_wq[l], peer_subkeys[l],
                  peer_u[l], peer_v[l], s, tp, eb).reshape(b, s, d)
    return x
```

```python
import functools
import math

import ml_dtypes
import numpy as np

import jax
import jax.numpy as jnp
from jax import lax
from jax.experimental import pallas as pl
from jax.experimental.pallas import tpu as pltpu

F32 = jnp.float32
BF16 = jnp.bfloat16
I32 = jnp.int32

CHUNK = 64
EPS = 1e-6
POOL_WINDOWS = (2, 4, 8, 16)
POOL_GROUP = 128
POOL_WIDTH = 512
ATTN_WIDTH = 512
HEAD_DIM = 64
N_HEADS = 8
IDX_HEADS = 8
IDX_DIM = 64
TOPK_MAX = 256
PEER_HEADS = 8
PEER_KEYS = 128
PEER_TOPK = 16
NEG = -1e30

LANES = 128
POOL_HALO = 16
LOG2E = math.log2(math.e)
M_INIT = 0.5 * NEG
N_FEAT = 9

NT_DIMS = (((1,), (1,)), ((), ()))


def _dot(a, b):
    return jnp.dot(a, b, preferred_element_type=F32)


def _dot_nt(a, b):
    return lax.dot_general(a, b, NT_DIMS, preferred_element_type=F32)


def _f32_key(v):
    b = int(np.float32(v).view(np.int32))
    return b ^ 0x7FFFFFFF if b < 0 else b


def _bf16_split3(c):
    c = np.float32(c)
    p1 = np.float32(c.astype(ml_dtypes.bfloat16))
    r = np.float32(c - p1)
    p2 = np.float32(r.astype(ml_dtypes.bfloat16))
    p3 = np.float32(np.float32(r - p2).astype(ml_dtypes.bfloat16))
    return float(p1), float(p2), float(p3)


_SLOPE_L2E = tuple(float(np.float32(2.0 ** (-8.0 * (h + 1) / N_HEADS) * LOG2E)) for h in range(N_HEADS))


def _ada_kernel(c_ref, w_ref, b_ref, o_ref):
    c = c_ref[...]
    cond = c * jax.nn.sigmoid(c)
    o_ref[...] = jnp.dot(cond, w_ref[...], preferred_element_type=F32,
                         precision=lax.Precision.HIGHEST) + b_ref[...]


def _ada(c, w, b):
    bsz, d = c.shape
    n = w.shape[1]
    tn = 1024
    return pl.pallas_call(
        _ada_kernel,
        grid=(n // tn,),
        in_specs=[pl.BlockSpec((bsz, d), lambda j: (0, 0)),
                  pl.BlockSpec((d, tn), lambda j: (0, j)),
                  pl.BlockSpec((1, tn), lambda j: (0, j))],
        out_specs=pl.BlockSpec((bsz, tn), lambda j: (0, j)),
        out_shape=jax.ShapeDtypeStruct((bsz, n), F32),
        name="ada",
    )(c, w, b.reshape(1, n))


def _group_sumsq(t, ones_bd):
    sq = t * t
    hi = sq.astype(BF16)
    lo = (sq - hi.astype(F32)).astype(BF16)
    return _dot(hi, ones_bd) + _dot(lo, ones_bd)


def _inproj_kernel(x_ref, sh_ref, sc_ref, g_ref, wnat_ref, wki_ref, wvt_ref, wwt_ref, ones_ref,
                   qg_ref, kg_ref, poolw_ref, pscale_ref, pg_ref,
                   q_ref, k_ref, qi_ref, kid_ref, vt_ref, wt_ref, pool_ref,
                   ubuf, *, tm, kc):
    i = pl.program_id(1)
    x = x_ref[0]
    ms = jnp.mean(x * x, axis=-1, keepdims=True)
    h = (x * lax.rsqrt(ms + EPS) * g_ref[...]) * (1.0 + sc_ref[0]) + sh_ref[0]
    hb = h.astype(BF16)

    proj = _dot(hb, wnat_ref[...])
    u = proj[:, 0:512]
    q = proj[:, 512:1024]
    k = proj[:, 1024:1536]
    qi = proj[:, 1536:2048]

    ones_bd = ones_ref[...]
    qn = q * lax.rsqrt(_group_sumsq(q, ones_bd) * (1.0 / HEAD_DIM) + EPS) * qg_ref[...]
    kn = k * lax.rsqrt(_group_sumsq(k, ones_bd) * (1.0 / HEAD_DIM) + EPS) * kg_ref[...]
    q_ref[0] = qn.astype(BF16)
    k_ref[0] = kn.astype(BF16)
    qi_ref[0] = (qi * (IDX_DIM ** -0.5)).astype(BF16)
    kid_ref[0] = _dot(hb, wki_ref[...]).astype(BF16)
    vt = _dot_nt(wvt_ref[...], hb).astype(BF16)
    for ci in range(tm // kc):
        vt_ref[0, ci] = vt[:, ci * kc:(ci + 1) * kc]
    wt = _dot_nt(wwt_ref[...], hb)
    wt_ref[0] = wt[0:IDX_HEADS, :] * (IDX_HEADS ** -0.5)

    @pl.when(i == 0)
    def _():
        ubuf[0:POOL_HALO, :] = jnp.zeros((POOL_HALO, POOL_WIDTH), F32)

    ubuf[POOL_HALO:POOL_HALO + tm, :] = u
    tpos = i * tm + lax.broadcasted_iota(I32, (tm, POOL_GROUP), 0)
    parts = []
    for gi, win in enumerate(POOL_WINDOWS):
        lo_l, hi_l = gi * POOL_GROUP, (gi + 1) * POOL_GROUP
        ws = ubuf[POOL_HALO:POOL_HALO + tm, lo_l:hi_l]
        for j in range(1, win):
            ws = ws + ubuf[POOL_HALO - j:POOL_HALO - j + tm, lo_l:hi_l]
        cnt = jnp.minimum(tpos + 1, win).astype(F32)
        pooled = ws / cnt - u[:, lo_l:hi_l]
        parts.append(_dot(pooled.astype(BF16), poolw_ref[gi]))
    mixed = jnp.concatenate(parts, axis=-1) * pscale_ref[...]
    pms = jnp.mean(mixed * mixed, axis=-1, keepdims=True)
    pool_ref[0] = (mixed * lax.rsqrt(pms + EPS) * pg_ref[...]).astype(BF16)
    ubuf[0:POOL_HALO, :] = ubuf[tm:tm + POOL_HALO, :]


def _inproj(x, sh1, sc1, norm_g, w_in, pool_w, pool_scale, pool_out_g, q_norm_g, k_norm_g, tm, kc):
    b, s, d = x.shape
    nt = s // tm
    w_u, w_q, w_k, w_v, w_qi, w_ki, w_wi = jnp.split(
        w_in, (512, 1024, 1536, 2048, 2560, 2624), axis=-1)
    wnat = jnp.concatenate([w_u, w_q, w_k, w_qi], axis=-1).astype(BF16)
    wki = jnp.concatenate([w_ki, w_ki], axis=-1).astype(BF16)
    wvt = w_v.T.astype(BF16)
    wwt = jnp.concatenate([w_wi.T, jnp.zeros((16 - IDX_HEADS, d), F32)], axis=0).astype(BF16)
    gid = jnp.arange(ATTN_WIDTH) // HEAD_DIM
    ones_bd = (gid[:, None] == gid[None, :]).astype(BF16)
    qg = (jnp.tile(q_norm_g, N_HEADS) * (HEAD_DIM ** -0.5 * LOG2E)).reshape(1, ATTN_WIDTH)
    kg = jnp.tile(k_norm_g, N_HEADS).reshape(1, ATTN_WIDTH)

    full = lambda shape: pl.BlockSpec(shape, lambda bi, ti: (0,) * len(shape))
    tok = lambda w: pl.BlockSpec((1, tm, w), lambda bi, ti: (bi, ti, 0))
    per_b = pl.BlockSpec((1, 1, d), lambda bi, ti: (bi, 0, 0))
    out_shapes = (
        jax.ShapeDtypeStruct((b, s, ATTN_WIDTH), BF16),
        jax.ShapeDtypeStruct((b, s, ATTN_WIDTH), BF16),
        jax.ShapeDtypeStruct((b, s, IDX_HEADS * IDX_DIM), BF16),
        jax.ShapeDtypeStruct((b, s, 2 * IDX_DIM), BF16),
        jax.ShapeDtypeStruct((b, s // kc, ATTN_WIDTH, kc), BF16),
        jax.ShapeDtypeStruct((b, IDX_HEADS, s), F32),
        jax.ShapeDtypeStruct((b, s, POOL_WIDTH), BF16),
    )
    out_specs = (
        tok(ATTN_WIDTH), tok(ATTN_WIDTH), tok(IDX_HEADS * IDX_DIM), tok(2 * IDX_DIM),
        pl.BlockSpec((1, tm // kc, ATTN_WIDTH, kc), lambda bi, ti: (bi, ti, 0, 0)),
        pl.BlockSpec((1, IDX_HEADS, tm), lambda bi, ti: (bi, 0, ti)),
        tok(POOL_WIDTH),
    )
    return pl.pallas_call(
        functools.partial(_inproj_kernel, tm=tm, kc=kc),
        grid=(b, nt),
        in_specs=[tok(d), per_b, per_b, full((1, d)), full((d, 2048)), full((d, 2 * IDX_DIM)),
                  full((ATTN_WIDTH, d)), full((16, d)), full((ATTN_WIDTH, ATTN_WIDTH)),
                  full((1, ATTN_WIDTH)), full((1, ATTN_WIDTH)),
                  full((len(POOL_WINDOWS), POOL_GROUP, POOL_GROUP)),
                  full((1, POOL_WIDTH)), full((1, POOL_WIDTH))],
        out_specs=out_specs,
        out_shape=out_shapes,
        scratch_shapes=[pltpu.VMEM((tm + POOL_HALO, POOL_WIDTH), F32)],
        compiler_params=pltpu.CompilerParams(
            dimension_semantics=("arbitrary", "arbitrary"), vmem_limit_bytes=48 * 1024 * 1024),
        name="inproj",
    )(x, sh1, sc1, norm_g.reshape(1, d), wnat, wki, wvt, wwt, ones_bd, qg, kg,
      pool_w.astype(BF16), pool_scale.reshape(1, -1), pool_out_g.reshape(1, -1))


def _attn_kernel(q_ref, qi_ref, wt_ref, kid_ref, k_ref, kf_ref, vt_ref, aog_ref, o_ref,
                 key_ref, acc_ref, m_ref, l_ref, qim_ref, qmf_ref, cut_ref, lg_ref,
                 *, tq, ck, seq, topk):
    j = pl.program_id(1)
    nck = ((j + 1) * tq + ck - 1) // ck
    qpos = j * tq + lax.broadcasted_iota(I32, (1, tq), 1)
    qchunk = qpos // CHUNK
    row_i = lax.broadcasted_iota(I32, (ck, tq), 0)

    lane = lax.broadcasted_iota(I32, (tq, LANES), 1)
    tcol = (j * tq + lax.broadcasted_iota(I32, (tq, LANES), 0)).astype(F32)
    for h in range(N_HEADS):
        p = h // 2
        keep = (lane < HEAD_DIM) if h % 2 == 0 else (lane >= HEAD_DIM)
        qi_slab = qi_ref[0, :, p * LANES:(p + 1) * LANES]
        q_slab = q_ref[0, :, p * LANES:(p + 1) * LANES]
        qim_ref[h * tq:(h + 1) * tq, :] = jnp.where(keep, qi_slab, jnp.zeros_like(qi_slab))
        qmf_ref[h, :, 0:LANES] = jnp.where(keep, q_slab, jnp.zeros_like(q_slab))
        c1, c2, c3 = _bf16_split3(_SLOPE_L2E[h])
        u = -_SLOPE_L2E[h] * tcol
        u1 = u.astype(BF16).astype(F32)
        r = u - u1
        u2 = r.astype(BF16).astype(F32)
        u3 = (r - u2).astype(BF16).astype(F32)
        feat = jnp.zeros((tq, LANES), F32)
        for li, val in enumerate((64.0 * c1, 64.0 * c2, 64.0 * c3, c1, c2, c3, u1, u2, u3)):
            feat = jnp.where(lane == li, val, feat)
        qmf_ref[h, :, LANES:2 * LANES] = feat.astype(BF16)

    def scores(c, last):
        r0 = pl.multiple_of(c * ck, ck)
        d = _dot_nt(kid_ref[0, pl.ds(r0, ck), :], qim_ref[...])
        acc = jnp.zeros((ck, tq), F32)
        for h in range(IDX_HEADS):
            acc = acc + wt_ref[0, h:h + 1, :] * jnp.maximum(d[:, h * tq:(h + 1) * tq], 0.0)
        if last:
            acc = jnp.where(((r0 + row_i) // CHUNK) <= qchunk, acc, NEG)
        bits = pltpu.bitcast(acc, I32)
        key_ref[pl.ds(r0, ck), :] = jnp.where(bits < 0, bits ^ 0x7FFFFFFF, bits)

    def scores_body(c, carry):
        scores(c, False)
        return carry

    lax.fori_loop(0, nck - 1, scores_body, 0)
    scores(nck - 1, True)

    def count(pred):
        def body(c, c8):
            r0 = pl.multiple_of(c * ck, ck)
            m = pred(key_ref[pl.ds(r0, ck), :], r0).astype(I32)
            return c8 + m.reshape(ck // 8, 8, tq).sum(axis=0)
        c8 = lax.fori_loop(0, nck, body, jnp.zeros((8, tq), I32))
        return c8.sum(axis=0, keepdims=True)

    c_nonneg = count(lambda kk, r0: kk >= 0)
    ok0 = c_nonneg >= topk
    state0 = (jnp.where(ok0, 0, -(2 ** 31)).astype(I32),
              jnp.where(ok0, c_nonneg, nck * ck), jnp.where(ok0, 0, c_nonneg))

    def bit_body(b, state):
        lo, c_ge, c_gt = state
        trial = lo + jnp.left_shift(jnp.int32(1), 30 - b)
        c = count(lambda kk, r0: kk >= trial)
        ok = c >= topk
        return jnp.where(ok, trial, lo), jnp.where(ok, c, c_ge), jnp.where(ok, c_gt, c)

    lo, c_ge, c_gt = lax.fori_loop(0, 31, bit_body, state0)

    need = topk - c_gt
    cut_ref[...] = jnp.full((1, tq), seq, I32)

    @pl.when(jnp.max(c_ge) > topk)
    def _():
        nbits = max(1, (seq - 1).bit_length())

        def idx_body(b, jv):
            trial = jv + jnp.left_shift(jnp.int32(1), nbits - 1 - b)
            c = count(lambda kk, r0: (kk == lo)
                      & ((r0 + lax.broadcasted_iota(I32, (ck, tq), 0)) < trial))
            return jnp.where(c < need, trial, jv)

        jv = lax.fori_loop(0, nbits, idx_body, jnp.zeros((1, tq), I32))
        cut_ref[...] = jv + 1

    cut = cut_ref[...]

    m_ref[...] = jnp.full((N_HEADS, 1, tq), M_INIT, F32)
    l_ref[...] = jnp.zeros((N_HEADS, 1, tq), F32)
    acc_ref[...] = jnp.zeros((ATTN_WIDTH, tq), F32)

    def attend(c, last):
        r0 = pl.multiple_of(c * ck, ck)
        kk = key_ref[pl.ds(r0, ck), :]
        kf = kf_ref[pl.ds(r0, ck), :]
        kch = k_ref[0, pl.ds(r0, ck), :]
        vtc = vt_ref[0, c]
        kpos = r0 + row_i
        sel = (kk > lo) | ((kk == lo) & (kpos < cut))
        if last:
            sel = sel & ((kpos // CHUNK) <= qchunk)
            ahead = jnp.maximum(kpos - qpos, 0).astype(F32)
        pen = jnp.where(sel, 0.0, NEG)
        for h in range(N_HEADS):
            p = h // 2
            ks = kch[:, p * LANES:(p + 1) * LANES]
            lg = _dot_nt(jnp.concatenate([ks, kf], axis=1), qmf_ref[h]) + pen
            if last:
                lg = lg - (2.0 * _SLOPE_L2E[h]) * ahead
            lg_ref[h] = lg
        for h in range(N_HEADS):
            m_old = m_ref[h]
            m_new = jnp.maximum(m_old, jnp.max(lg_ref[h], axis=0, keepdims=True))
            alpha = jnp.exp2(m_old - m_new)
            pr = jnp.exp2(lg_ref[h] - m_new)
            l_ref[h] = alpha * l_ref[h] + jnp.sum(pr, axis=0, keepdims=True)
            m_ref[h] = m_new
            rows = slice(h * HEAD_DIM, (h + 1) * HEAD_DIM)
            acc_ref[rows, :] = alpha * acc_ref[rows, :] + _dot(vtc[rows, :], pr.astype(BF16))

    def attend_body(c, carry):
        attend(c, False)
        return carry

    lax.fori_loop(0, nck - 1, attend_body, 0)
    attend(nck - 1, True)

    outs = []
    for h in range(N_HEADS):
        rows = slice(h * HEAD_DIM, (h + 1) * HEAD_DIM)
        o = acc_ref[rows, :] / l_ref[h]
        oms = jnp.mean(o * o, axis=0, keepdims=True)
        outs.append(o * lax.rsqrt(oms + EPS))
    ot = jnp.concatenate(outs, axis=0)
    o_ref[0] = (ot.T * aog_ref[...]).astype(BF16)


def _attn(q, k, qi, kid, vt, wt, attn_out_g, tq, ck):
    b, s, _ = q.shape
    topk = min(TOPK_MAX, s // 4)
    assert tq % LANES == 0 and ck % tq == 0 and s % ck == 0 and s // CHUNK <= 256
    pos = jnp.arange(s)
    feats = [pos // CHUNK] * 3 + [pos % CHUNK] * 3 + [jnp.ones_like(pos)] * 3
    kf = jnp.zeros((s, LANES), F32).at[:, 0:N_FEAT].set(
        jnp.stack(feats, axis=1).astype(F32)).astype(BF16)
    blk = lambda w: pl.BlockSpec((1, tq, w), lambda bi, ji: (bi, ji, 0))
    once = pl.Buffered(1)
    whole = lambda w: pl.BlockSpec((1, s, w), lambda bi, ji: (bi, 0, 0), pipeline_mode=once)
    return pl.pallas_call(
        functools.partial(_attn_kernel, tq=tq, ck=ck, seq=s, topk=topk),
        grid=(b, s // tq),
        in_specs=[blk(ATTN_WIDTH), blk(IDX_HEADS * IDX_DIM),
                  pl.BlockSpec((1, IDX_HEADS, tq), lambda bi, ji: (bi, 0, ji)),
                  whole(2 * IDX_DIM), whole(ATTN_WIDTH),
                  pl.BlockSpec((s, LANES), lambda bi, ji: (0, 0), pipeline_mode=once),
                  pl.BlockSpec((1, s // ck, ATTN_WIDTH, ck), lambda bi, ji: (bi, 0, 0, 0),
                               pipeline_mode=once),
                  pl.BlockSpec((1, ATTN_WIDTH), lambda bi, ji: (0, 0))],
        out_specs=blk(ATTN_WIDTH),
        out_shape=jax.ShapeDtypeStruct((b, s, ATTN_WIDTH), BF16),
        scratch_shapes=[pltpu.VMEM((s, tq), I32),
                        pltpu.VMEM((ATTN_WIDTH, tq), F32),
                        pltpu.VMEM((N_HEADS, 1, tq), F32),
                        pltpu.VMEM((N_HEADS, 1, tq), F32),
                        pltpu.VMEM((IDX_HEADS * tq, LANES), BF16),
                        pltpu.VMEM((N_HEADS, tq, 2 * LANES), BF16),
                        pltpu.VMEM((1, tq), I32),
                        pltpu.VMEM((N_HEADS, ck, tq), F32)],
        compiler_params=pltpu.CompilerParams(
            dimension_semantics=("arbitrary", "arbitrary"), vmem_limit_bytes=56 * 1024 * 1024),
        name="attn",
    )(q, qi, wt, kid, k, kf, vt, attn_out_g.reshape(1, ATTN_WIDTH))


def _outproj_kernel(pool_ref, attn_ref, x_ref, wo_ref, g1_ref, sh_ref, sc_ref, ng_ref,
                    x1_ref, h2t_ref):
    mixed = _dot(pool_ref[0], wo_ref[0:POOL_WIDTH, :]) + _dot(attn_ref[0], wo_ref[POOL_WIDTH:, :])
    x1 = x_ref[0] + g1_ref[0] * mixed
    x1_ref[0] = x1
    ms = jnp.mean(x1 * x1, axis=-1, keepdims=True)
    h2 = (x1 * lax.rsqrt(ms + EPS) * ng_ref[...]) * (1.0 + sc_ref[0]) + sh_ref[0]
    h2t_ref[...] = h2.T.astype(BF16)


def _outproj(pool_out, attn_out, x, w_out, g1, sh2, sc2, norm_g, tm):
    b, s, d = x.shape
    nt = s // tm
    tok = lambda w: pl.BlockSpec((1, tm, w), lambda bi, ti: (bi, ti, 0))
    per_b = pl.BlockSpec((1, 1, d), lambda bi, ti: (bi, 0, 0))
    return pl.pallas_call(
        _outproj_kernel,
        grid=(b, nt),
        in_specs=[tok(POOL_WIDTH), tok(ATTN_WIDTH), tok(d),
                  pl.BlockSpec((POOL_WIDTH + ATTN_WIDTH, d), lambda bi, ti: (0, 0)),
                  per_b, per_b, per_b, pl.BlockSpec((1, d), lambda bi, ti: (0, 0))],
        out_specs=(tok(d), pl.BlockSpec((d, tm), lambda bi, ti: (0, bi * nt + ti))),
        out_shape=(jax.ShapeDtypeStruct((b, s, d), F32), jax.ShapeDtypeStruct((d, b * s), BF16)),
        compiler_params=pltpu.CompilerParams(dimension_semantics=("arbitrary", "arbitrary")),
        name="outproj",
    )(pool_out, attn_out, x, w_out.astype(BF16), g1, sh2, sc2, norm_g.reshape(1, d))


def _batcher_pairs(n):
    pairs = []

    def merge(lo, cnt, r):
        step = r * 2
        if step < cnt:
            merge(lo, cnt, step)
            merge(lo + r, cnt, step)
            for i in range(lo + r, lo + cnt - r, step):
                pairs.append((i, i + r))
        else:
            pairs.append((lo, lo + r))

    def sort(lo, cnt):
        if cnt > 1:
            m = cnt // 2
            sort(lo, m)
            sort(lo + m, m)
            merge(lo, cnt, 1)

    sort(0, n)
    return tuple(pairs)


_SORT16 = _batcher_pairs(PEER_TOPK)
MM_GROUP = 8
MM_COLS = 256


def _top16_desc(x):
    x = list(x)
    for i, j in _SORT16:
        x[i], x[j] = jnp.maximum(x[i], x[j]), jnp.minimum(x[i], x[j])
    for shift in (4, 2, 1):
        y = [jnp.maximum(x[k], pltpu.roll(x[PEER_TOPK - 1 - k], shift, 0)) for k in range(PEER_TOPK)]
        for d in (8, 4, 2, 1):
            for k in range(PEER_TOPK):
                if k & d == 0:
                    y[k], y[k + d] = jnp.maximum(y[k], y[k + d]), jnp.minimum(y[k], y[k + d])
        x = y
    return x


def _peer_kernel(h2t_ref, x1_ref, g2_ref, wqt_ref, sk_ref, u_ref, vt_ref, o_ref,
                 s1_ref, s2_ref, thr_ref, p_ref, acc_ref, s1b_ref, *, tm, eb, ne):
    e = pl.program_id(1)
    nsub = eb // PEER_KEYS
    ntl = tm // LANES

    @pl.when(e == 0)
    def _prologue():
        h2t = h2t_ref[...]
        sub = lax.broadcasted_iota(I32, (8, tm), 0)
        ninf = jnp.full((8, tm), -jnp.inf, F32)

        def compact(t, base):
            out = t[base + 7]
            for r in range(6, -1, -1):
                out = jnp.where(sub == r, t[base + r], out)
            return out

        def split8(st):
            return [st[k * 8:(k + 1) * 8, :] for k in range(PEER_KEYS // 8)]

        def head(hd, carry):
            q1 = _dot(wqt_ref[2 * hd], h2t).astype(BF16)
            q2 = _dot(wqt_ref[2 * hd + 1], h2t).astype(BF16)
            st1 = _dot(sk_ref[2 * hd], q1) * LOG2E
            st2 = _dot(sk_ref[2 * hd + 1], q2) * LOG2E
            for tl in range(ntl):
                s2_ref[hd, tl] = st2[:, tl * LANES:(tl + 1) * LANES]
            t1 = _top16_desc(split8(st1))
            t2 = _top16_desc(split8(st2))
            a2lo, a2hi = compact(t2, 0), compact(t2, 8)

            def candidates(t1x):
                cands = [t1x[0] + a2lo, t1x[0] + a2hi, t1x[1] + a2lo]
                for r1 in range(2, 8):
                    cands.append(jnp.where(sub < PEER_TOPK // (r1 + 1), t1x[r1] + a2lo, ninf))
                cands.append(compact(t1x, 8) + t2[0])
                return cands

            def kth(cands):
                return _top16_desc(cands + [ninf] * (PEER_TOPK - len(cands)))[PEER_TOPK - 1]

            cands = candidates(t1)
            thr = kth(cands)
            m = t1[0] + t2[0]
            z8 = jnp.zeros((8, tm), F32)
            for c in cands:
                z8 = z8 + jnp.where(c >= thr, jnp.exp2(c - m), 0.0)
            mz = m + jnp.log2(jnp.broadcast_to(jnp.sum(z8, axis=0, keepdims=True), (8, tm)))
            s1_ref[hd] = st1 - mz[0:1, :]
            thr_ref[pl.ds(hd, 1), :] = kth(candidates([t - mz for t in t1]))[0:1, :]
            return carry

        lax.fori_loop(0, PEER_HEADS, head, 0)
        acc_ref[...] = jnp.zeros_like(acc_ref)

    i0 = pl.multiple_of(e * nsub, nsub)
    s1rows = [s1_ref[hd, pl.ds(i0, nsub), :] for hd in range(PEER_HEADS)]
    for hd in range(PEER_HEADS):
        s1b_ref[hd] = s1rows[hd]
    gsz = MM_GROUP * PEER_KEYS
    for g in range(nsub // MM_GROUP):
        grows = slice(g * gsz, (g + 1) * gsz)
        a_grp = _dot(u_ref[grows, :], h2t_ref[...])
        for sg in range(MM_GROUP):
            ii = g * MM_GROUP + sg
            rows = slice(ii * PEER_KEYS, (ii + 1) * PEER_KEYS)
            for tl in range(ntl):
                cols = slice(tl * LANES, (tl + 1) * LANES)
                w = jnp.zeros((PEER_KEYS, LANES), F32)
                for hd in range(PEER_HEADS):
                    sm = s1b_ref[hd, ii:ii + 1, cols] + s2_ref[hd, tl]
                    w = w + jnp.where(sm >= thr_ref[hd:hd + 1, cols], jnp.exp2(sm), 0.0)
                a = a_grp[sg * PEER_KEYS:(sg + 1) * PEER_KEYS, cols]
                gl = 0.5 * a * (1.0 + lax.erf(a * (1.0 / math.sqrt(2.0))))
                p_ref[rows, cols] = (gl * w).astype(BF16)
        acc_ref[...] += _dot(vt_ref[:, grows], p_ref[grows, :])

    @pl.when(e == ne - 1)
    def _epilogue():
        o_ref[...] = x1_ref[...] + g2_ref[0] * acc_ref[...].T


def _peer(h2t, x1, g2, wq, sub_keys, u_tab, v_tab, seq, tm, eb):
    d, t = h2t.shape
    ne = u_tab.shape[0] // eb
    half = PEER_KEYS
    wqt = wq.T.reshape(2 * PEER_HEADS, half, d).astype(BF16)
    sk = sub_keys.reshape(2 * PEER_HEADS, PEER_KEYS, half).astype(BF16)
    ub = u_tab.astype(BF16)
    vtb = v_tab.T.astype(BF16)
    hs = (PEER_HEADS, PEER_KEYS, tm)
    return pl.pallas_call(
        functools.partial(_peer_kernel, tm=tm, eb=eb, ne=ne),
        grid=(t // tm, ne),
        in_specs=[pl.BlockSpec((d, tm), lambda i, e: (0, i)),
                  pl.BlockSpec((tm, d), lambda i, e: (i, 0)),
                  pl.BlockSpec((1, 1, d), lambda i, e: ((i * tm) // seq, 0, 0)),
                  pl.BlockSpec((2 * PEER_HEADS, half, d), lambda i, e: (0, 0, 0)),
                  pl.BlockSpec((2 * PEER_HEADS, PEER_KEYS, half), lambda i, e: (0, 0, 0)),
                  pl.BlockSpec((eb, d), lambda i, e: (e, 0)),
                  pl.BlockSpec((d, eb), lambda i, e: (0, e))],
        out_specs=pl.BlockSpec((tm, d), lambda i, e: (i, 0)),
        out_shape=jax.ShapeDtypeStruct((t, d), F32),
        scratch_shapes=[pltpu.VMEM(hs, F32),
                        pltpu.VMEM((PEER_HEADS, tm // LANES, PEER_KEYS, LANES), F32),
                        pltpu.VMEM((PEER_HEADS, tm), F32),
                        pltpu.VMEM((eb, tm), BF16),
                        pltpu.VMEM((d, tm), F32),
                        pltpu.VMEM((PEER_HEADS, eb // PEER_KEYS, tm), F32)],
        compiler_params=pltpu.CompilerParams(
            dimension_semantics=("arbitrary", "arbitrary"), vmem_limit_bytes=58 * 1024 * 1024),
        name="peer",
    )(h2t, x1, g2, wqt, sk, ub, vtb)


def _tile(n, pref):
    t = min(pref, n)
    assert n % t == 0
    return t


def kernel(x, c, ada_w, ada_b, norm1_g, w_in, pool_w, pool_scale, pool_out_g, q_norm_g, k_norm_g,
           attn_out_g, w_out, norm2_g, peer_wq, peer_subkeys, peer_u, peer_v):
    b, s, d = x.shape
    depth = ada_w.shape[0]
    tm = _tile(s, 512)
    tq = _tile(s, 2 * LANES)
    ck = _tile(tm, 512)
    tp = _tile(s, 512)
    eb = 2048
    for l in range(depth):
        mod = _ada(c, ada_w[l], ada_b[l]).reshape(b, 1, 6 * d)
        sh1, sc1, g1, sh2, sc2, g2 = jnp.split(mod, 6, axis=-1)
        q, k, qi, kid, vt, wt, pool_out = _inproj(
            x, sh1, sc1, norm1_g[l], w_in[l], pool_w[l], pool_scale[l], pool_out_g[l],
            q_norm_g[l], k_norm_g[l], tm, ck)
        attn_out = _attn(q, k, qi, kid, vt, wt, attn_out_g[l], tq, ck)
        x1, h2t = _outproj(pool_out, attn_out, x, w_out[l], g1, sh2, sc2, norm2_g[l], tm)
        x = _peer(h2t, x1.reshape(b * s, d), g2, peer_wq[l], peer_subkeys[l],
                  peer_u[l], peer_v[l], s, tp, eb).reshape(b, s, d)
    return x
```

```python
import functools
import math

import ml_dtypes
import numpy as np

import jax
import jax.numpy as jnp
from jax import lax
from jax.experimental import pallas as pl
from jax.experimental.pallas import tpu as pltpu

F32 = jnp.float32
BF16 = jnp.bfloat16
I32 = jnp.int32

CHUNK = 64
EPS = 1e-6
POOL_WINDOWS = (2, 4, 8, 16)
POOL_GROUP = 128
POOL_WIDTH = 512
ATTN_WIDTH = 512
HEAD_DIM = 64
N_HEADS = 8
IDX_HEADS = 8
IDX_DIM = 64
TOPK_MAX = 256
PEER_HEADS = 8
PEER_KEYS = 128
PEER_TOPK = 16
NEG = -1e30

LANES = 128
POOL_HALO = 16
LOG2E = math.log2(math.e)
M_INIT = 0.5 * NEG
N_FEAT = 9

NT_DIMS = (((1,), (1,)), ((), ()))


def _dot(a, b):
    return jnp.dot(a, b, preferred_element_type=F32)


def _dot_nt(a, b):
    return lax.dot_general(a, b, NT_DIMS, preferred_element_type=F32)


def _f32_key(v):
    b = int(np.float32(v).view(np.int32))
    return b ^ 0x7FFFFFFF if b < 0 else b


def _bf16_split3(c):
    c = np.float32(c)
    p1 = np.float32(c.astype(ml_dtypes.bfloat16))
    r = np.float32(c - p1)
    p2 = np.float32(r.astype(ml_dtypes.bfloat16))
    p3 = np.float32(np.float32(r - p2).astype(ml_dtypes.bfloat16))
    return float(p1), float(p2), float(p3)


_SLOPE_L2E = tuple(float(np.float32(2.0 ** (-8.0 * (h + 1) / N_HEADS) * LOG2E)) for h in range(N_HEADS))


def _ada_kernel(c_ref, w_ref, b_ref, o_ref):
    c = c_ref[...]
    cond = c * jax.nn.sigmoid(c)
    o_ref[...] = jnp.dot(cond, w_ref[...], preferred_element_type=F32,
                         precision=lax.Precision.HIGHEST) + b_ref[...]


def _ada(c, w, b):
    bsz, d = c.shape
    n = w.shape[1]
    tn = 1024
    return pl.pallas_call(
        _ada_kernel,
        grid=(n // tn,),
        in_specs=[pl.BlockSpec((bsz, d), lambda j: (0, 0)),
                  pl.BlockSpec((d, tn), lambda j: (0, j)),
                  pl.BlockSpec((1, tn), lambda j: (0, j))],
        out_specs=pl.BlockSpec((bsz, tn), lambda j: (0, j)),
        out_shape=jax.ShapeDtypeStruct((bsz, n), F32),
        name="ada",
    )(c, w, b.reshape(1, n))


def _group_sumsq(t, ones_bd):
    sq = t * t
    hi = sq.astype(BF16)
    lo = (sq - hi.astype(F32)).astype(BF16)
    return _dot(hi, ones_bd) + _dot(lo, ones_bd)


def _inproj_kernel(x_ref, sh_ref, sc_ref, g_ref, wnat_ref, wki_ref, wvt_ref, wwt_ref, ones_ref,
                   qg_ref, kg_ref, poolw_ref, pscale_ref, pg_ref,
                   q_ref, k_ref, qi_ref, kid_ref, vt_ref, wt_ref, pool_ref,
                   ubuf, *, tm, kc):
    i = pl.program_id(1)
    x = x_ref[0]
    ms = jnp.mean(x * x, axis=-1, keepdims=True)
    h = (x * lax.rsqrt(ms + EPS) * g_ref[...]) * (1.0 + sc_ref[0]) + sh_ref[0]
    hb = h.astype(BF16)

    proj = _dot(hb, wnat_ref[...])
    u = proj[:, 0:512]
    q = proj[:, 512:1024]
    k = proj[:, 1024:1536]
    qi = proj[:, 1536:2048]

    ones_bd = ones_ref[...]
    qn = q * lax.rsqrt(_group_sumsq(q, ones_bd) * (1.0 / HEAD_DIM) + EPS) * qg_ref[...]
    kn = k * lax.rsqrt(_group_sumsq(k, ones_bd) * (1.0 / HEAD_DIM) + EPS) * kg_ref[...]
    q_ref[0] = qn.astype(BF16)
    k_ref[0] = kn.astype(BF16)
    qi_ref[0] = (qi * (IDX_DIM ** -0.5)).astype(BF16)
    kid_ref[0] = _dot(hb, wki_ref[...]).astype(BF16)
    vt = _dot_nt(wvt_ref[...], hb).astype(BF16)
    for ci in range(tm // kc):
        vt_ref[0, ci] = vt[:, ci * kc:(ci + 1) * kc]
    wt = _dot_nt(wwt_ref[...], hb)
    wt_ref[0] = wt[0:IDX_HEADS, :] * (IDX_HEADS ** -0.5)

    @pl.when(i == 0)
    def _():
        ubuf[0:POOL_HALO, :] = jnp.zeros((POOL_HALO, POOL_WIDTH), F32)

    ubuf[POOL_HALO:POOL_HALO + tm, :] = u
    tpos = i * tm + lax.broadcasted_iota(I32, (tm, POOL_GROUP), 0)
    parts = []
    for gi, win in enumerate(POOL_WINDOWS):
        lo_l, hi_l = gi * POOL_GROUP, (gi + 1) * POOL_GROUP
        ws = ubuf[POOL_HALO:POOL_HALO + tm, lo_l:hi_l]
        for j in range(1, win):
            ws = ws + ubuf[POOL_HALO - j:POOL_HALO - j + tm, lo_l:hi_l]
        cnt = jnp.minimum(tpos + 1, win).astype(F32)
        pooled = ws / cnt - u[:, lo_l:hi_l]
        parts.append(_dot(pooled.astype(BF16), poolw_ref[gi]))
    mixed = jnp.concatenate(parts, axis=-1) * pscale_ref[...]
    pms = jnp.mean(mixed * mixed, axis=-1, keepdims=True)
    pool_ref[0] = (mixed * lax.rsqrt(pms + EPS) * pg_ref[...]).astype(BF16)
    ubuf[0:POOL_HALO, :] = ubuf[tm:tm + POOL_HALO, :]


def _inproj(x, sh1, sc1, norm_g, w_in, pool_w, pool_scale, pool_out_g, q_norm_g, k_norm_g, tm, kc):
    b, s, d = x.shape
    nt = s // tm
    w_u, w_q, w_k, w_v, w_qi, w_ki, w_wi = jnp.split(
        w_in, (512, 1024, 1536, 2048, 2560, 2624), axis=-1)
    wnat = jnp.concatenate([w_u, w_q, w_k, w_qi], axis=-1).astype(BF16)
    wki = jnp.concatenate([w_ki, w_ki], axis=-1).astype(BF16)
    wvt = w_v.T.astype(BF16)
    wwt = jnp.concatenate([w_wi.T, jnp.zeros((16 - IDX_HEADS, d), F32)], axis=0).astype(BF16)
    gid = jnp.arange(ATTN_WIDTH) // HEAD_DIM
    ones_bd = (gid[:, None] == gid[None, :]).astype(BF16)
    qg = (jnp.tile(q_norm_g, N_HEADS) * (HEAD_DIM ** -0.5 * LOG2E)).reshape(1, ATTN_WIDTH)
    kg = jnp.tile(k_norm_g, N_HEADS).reshape(1, ATTN_WIDTH)

    full = lambda shape: pl.BlockSpec(shape, lambda bi, ti: (0,) * len(shape))
    tok = lambda w: pl.BlockSpec((1, tm, w), lambda bi, ti: (bi, ti, 0))
    per_b = pl.BlockSpec((1, 1, d), lambda bi, ti: (bi, 0, 0))
    out_shapes = (
        jax.ShapeDtypeStruct((b, s, ATTN_WIDTH), BF16),
        jax.ShapeDtypeStruct((b, s, ATTN_WIDTH), BF16),
        jax.ShapeDtypeStruct((b, s, IDX_HEADS * IDX_DIM), BF16),
        jax.ShapeDtypeStruct((b, s, 2 * IDX_DIM), BF16),
        jax.ShapeDtypeStruct((b, s // kc, ATTN_WIDTH, kc), BF16),
        jax.ShapeDtypeStruct((b, IDX_HEADS, s), F32),
        jax.ShapeDtypeStruct((b, s, POOL_WIDTH), BF16),
    )
    out_specs = (
        tok(ATTN_WIDTH), tok(ATTN_WIDTH), tok(IDX_HEADS * IDX_DIM), tok(2 * IDX_DIM),
        pl.BlockSpec((1, tm // kc, ATTN_WIDTH, kc), lambda bi, ti: (bi, ti, 0, 0)),
        pl.BlockSpec((1, IDX_HEADS, tm), lambda bi, ti: (bi, 0, ti)),
        tok(POOL_WIDTH),
    )
    return pl.pallas_call(
        functools.partial(_inproj_kernel, tm=tm, kc=kc),
        grid=(b, nt),
        in_specs=[tok(d), per_b, per_b, full((1, d)), full((d, 2048)), full((d, 2 * IDX_DIM)),
                  full((ATTN_WIDTH, d)), full((16, d)), full((ATTN_WIDTH, ATTN_WIDTH)),
                  full((1, ATTN_WIDTH)), full((1, ATTN_WIDTH)),
                  full((len(POOL_WINDOWS), POOL_GROUP, POOL_GROUP)),
                  full((1, POOL_WIDTH)), full((1, POOL_WIDTH))],
        out_specs=out_specs,
        out_shape=out_shapes,
        scratch_shapes=[pltpu.VMEM((tm + POOL_HALO, POOL_WIDTH), F32)],
        compiler_params=pltpu.CompilerParams(
            dimension_semantics=("arbitrary", "arbitrary"), vmem_limit_bytes=48 * 1024 * 1024),
        name="inproj",
    )(x, sh1, sc1, norm_g.reshape(1, d), wnat, wki, wvt, wwt, ones_bd, qg, kg,
      pool_w.astype(BF16), pool_scale.reshape(1, -1), pool_out_g.reshape(1, -1))


def _attn_kernel(q_ref, qi_ref, wt_ref, kid_ref, k_ref, kf_ref, vt_ref, aog_ref, o_ref,
                 key_ref, acc_ref, m_ref, l_ref, qim_ref, qmf_ref, cut_ref, lg_ref, hi_ref,
                 *, tq, ck, seq, topk):
    j = pl.program_id(1)
    nck = ((j + 1) * tq + ck - 1) // ck
    qpos = j * tq + lax.broadcasted_iota(I32, (1, tq), 1)
    qchunk = qpos // CHUNK
    row_i = lax.broadcasted_iota(I32, (ck, tq), 0)

    lane = lax.broadcasted_iota(I32, (tq, LANES), 1)
    tcol = (j * tq + lax.broadcasted_iota(I32, (tq, LANES), 0)).astype(F32)
    for h in range(N_HEADS):
        p = h // 2
        keep = (lane < HEAD_DIM) if h % 2 == 0 else (lane >= HEAD_DIM)
        qi_slab = qi_ref[0, :, p * LANES:(p + 1) * LANES]
        q_slab = q_ref[0, :, p * LANES:(p + 1) * LANES]
        qim_ref[h * tq:(h + 1) * tq, :] = jnp.where(keep, qi_slab, jnp.zeros_like(qi_slab))
        qmf_ref[h, :, 0:LANES] = jnp.where(keep, q_slab, jnp.zeros_like(q_slab))
        c1, c2, c3 = _bf16_split3(_SLOPE_L2E[h])
        u = -_SLOPE_L2E[h] * tcol
        u1 = u.astype(BF16).astype(F32)
        r = u - u1
        u2 = r.astype(BF16).astype(F32)
        u3 = (r - u2).astype(BF16).astype(F32)
        feat = jnp.zeros((tq, LANES), F32)
        for li, val in enumerate((64.0 * c1, 64.0 * c2, 64.0 * c3, c1, c2, c3, u1, u2, u3)):
            feat = jnp.where(lane == li, val, feat)
        qmf_ref[h, :, LANES:2 * LANES] = feat.astype(BF16)

    def scores(c, last):
        r0 = pl.multiple_of(c * ck, ck)
        d = _dot_nt(kid_ref[0, pl.ds(r0, ck), :], qim_ref[...])
        acc = jnp.zeros((ck, tq), F32)
        for h in range(IDX_HEADS):
            acc = acc + wt_ref[0, h:h + 1, :] * jnp.maximum(d[:, h * tq:(h + 1) * tq], 0.0)
        if last:
            acc = jnp.where(((r0 + row_i) // CHUNK) <= qchunk, acc, NEG)
        bits = pltpu.bitcast(acc, I32)
        key_ref[pl.ds(r0, ck), :] = jnp.where(bits < 0, bits ^ 0x7FFFFFFF, bits)
        hi_ref[pl.ds(r0, ck), :] = pltpu.bitcast(bits & -65536, F32).astype(BF16)

    def scores_body(c, carry):
        scores(c, False)
        return carry

    lax.fori_loop(0, nck - 1, scores_body, 0)
    scores(nck - 1, True)

    def count(pred):
        def body(c, c8):
            r0 = pl.multiple_of(c * ck, ck)
            m = pred(key_ref[pl.ds(r0, ck), :], r0).astype(I32)
            return c8 + m.reshape(ck // 8, 8, tq).sum(axis=0)
        c8 = lax.fori_loop(0, nck, body, jnp.zeros((8, tq), I32))
        return c8.sum(axis=0, keepdims=True)

    def count_hi(trial16):
        pat = jnp.where(trial16 < 0, trial16 ^ 0x7FFF, trial16)
        tb = pltpu.bitcast(jnp.left_shift(pat, 16), F32).astype(BF16)

        def body(c, cacc):
            r0 = pl.multiple_of(c * ck, ck)
            m = hi_ref[pl.ds(r0, ck), :] >= tb
            ones = jnp.where(m, jnp.ones((ck, tq), BF16), jnp.zeros((ck, tq), BF16))
            parts = [ones[i * 16:(i + 1) * 16, :] for i in range(ck // 16)]
            while len(parts) > 1:
                parts = [parts[i] + parts[i + 1] for i in range(0, len(parts), 2)]
            return cacc + parts[0].astype(F32)

        cacc = lax.fori_loop(0, nck, body, jnp.zeros((16, tq), F32))
        return cacc.sum(axis=0, keepdims=True).astype(I32)

    c_nonneg = count_hi(jnp.zeros((1, tq), I32))
    ok0 = c_nonneg >= topk
    state0 = (jnp.where(ok0, 0, -(2 ** 15)).astype(I32),
              jnp.where(ok0, c_nonneg, nck * ck), jnp.where(ok0, 0, c_nonneg))

    def hi_body(b, state):
        lo16, c_ge, c_gt = state
        trial = lo16 + jnp.left_shift(jnp.int32(1), 14 - b)
        c = count_hi(trial)
        ok = c >= topk
        return jnp.where(ok, trial, lo16), jnp.where(ok, c, c_ge), jnp.where(ok, c_gt, c)

    lo16, c_ge, c_gt = lax.fori_loop(0, 15, hi_body, state0)

    def bit_body(b, state):
        lo, c_ge, c_gt = state
        trial = lo + jnp.left_shift(jnp.int32(1), 15 - b)
        c = count(lambda kk, r0: kk >= trial)
        ok = c >= topk
        return jnp.where(ok, trial, lo), jnp.where(ok, c, c_ge), jnp.where(ok, c_gt, c)

    lo, c_ge, c_gt = lax.fori_loop(0, 16, bit_body, (jnp.left_shift(lo16, 16), c_ge, c_gt))

    need = topk - c_gt
    cut_ref[...] = jnp.full((1, tq), seq, I32)

    @pl.when(jnp.max(c_ge) > topk)
    def _():
        nbits = max(1, (seq - 1).bit_length())

        def idx_body(b, jv):
            trial = jv + jnp.left_shift(jnp.int32(1), nbits - 1 - b)
            c = count(lambda kk, r0: (kk == lo)
                      & ((r0 + lax.broadcasted_iota(I32, (ck, tq), 0)) < trial))
            return jnp.where(c < need, trial, jv)

        jv = lax.fori_loop(0, nbits, idx_body, jnp.zeros((1, tq), I32))
        cut_ref[...] = jv + 1

    cut = cut_ref[...]

    m_ref[...] = jnp.full((N_HEADS, 1, tq), M_INIT, F32)
    l_ref[...] = jnp.zeros((N_HEADS, 1, tq), F32)
    acc_ref[...] = jnp.zeros((ATTN_WIDTH, tq), F32)

    def attend(c, last):
        r0 = pl.multiple_of(c * ck, ck)
        kk = key_ref[pl.ds(r0, ck), :]
        kf = kf_ref[pl.ds(r0, ck), :]
        kch = k_ref[0, pl.ds(r0, ck), :]
        vtc = vt_ref[0, c]
        kpos = r0 + row_i
        sel = (kk > lo) | ((kk == lo) & (kpos < cut))
        if last:
            sel = sel & ((kpos // CHUNK) <= qchunk)
            ahead = jnp.maximum(kpos - qpos, 0).astype(F32)
        pen = jnp.where(sel, 0.0, NEG)
        for h in range(N_HEADS):
            p = h // 2
            ks = kch[:, p * LANES:(p + 1) * LANES]
            lg = _dot_nt(jnp.concatenate([ks, kf], axis=1), qmf_ref[h]) + pen
            if last:
                lg = lg - (2.0 * _SLOPE_L2E[h]) * ahead
            lg_ref[h] = lg
        for h in range(N_HEADS):
            m_old = m_ref[h]
            m_new = jnp.maximum(m_old, jnp.max(lg_ref[h], axis=0, keepdims=True))
            alpha = jnp.exp2(m_old - m_new)
            pr = jnp.exp2(lg_ref[h] - m_new)
            l_ref[h] = alpha * l_ref[h] + jnp.sum(pr, axis=0, keepdims=True)
            m_ref[h] = m_new
            rows = slice(h * HEAD_DIM, (h + 1) * HEAD_DIM)
            acc_ref[rows, :] = alpha * acc_ref[rows, :] + _dot(vtc[rows, :], pr.astype(BF16))

    def attend_body(c, carry):
        attend(c, False)
        return carry

    lax.fori_loop(0, nck - 1, attend_body, 0)
    attend(nck - 1, True)

    outs = []
    for h in range(N_HEADS):
        rows = slice(h * HEAD_DIM, (h + 1) * HEAD_DIM)
        o = acc_ref[rows, :] / l_ref[h]
        oms = jnp.mean(o * o, axis=0, keepdims=True)
        outs.append(o * lax.rsqrt(oms + EPS))
    ot = jnp.concatenate(outs, axis=0)
    o_ref[0] = (ot.T * aog_ref[...]).astype(BF16)


def _attn(q, k, qi, kid, vt, wt, attn_out_g, tq, ck):
    b, s, _ = q.shape
    topk = min(TOPK_MAX, s // 4)
    assert tq % LANES == 0 and ck % tq == 0 and s % ck == 0 and s // CHUNK <= 256
    pos = jnp.arange(s)
    feats = [pos // CHUNK] * 3 + [pos % CHUNK] * 3 + [jnp.ones_like(pos)] * 3
    kf = jnp.zeros((s, LANES), F32).at[:, 0:N_FEAT].set(
        jnp.stack(feats, axis=1).astype(F32)).astype(BF16)
    blk = lambda w: pl.BlockSpec((1, tq, w), lambda bi, ji: (bi, ji, 0))
    once = pl.Buffered(1)
    whole = lambda w: pl.BlockSpec((1, s, w), lambda bi, ji: (bi, 0, 0), pipeline_mode=once)
    return pl.pallas_call(
        functools.partial(_attn_kernel, tq=tq, ck=ck, seq=s, topk=topk),
        grid=(b, s // tq),
        in_specs=[blk(ATTN_WIDTH), blk(IDX_HEADS * IDX_DIM),
                  pl.BlockSpec((1, IDX_HEADS, tq), lambda bi, ji: (bi, 0, ji)),
                  whole(2 * IDX_DIM), whole(ATTN_WIDTH),
                  pl.BlockSpec((s, LANES), lambda bi, ji: (0, 0), pipeline_mode=once),
                  pl.BlockSpec((1, s // ck, ATTN_WIDTH, ck), lambda bi, ji: (bi, 0, 0, 0),
                               pipeline_mode=once),
                  pl.BlockSpec((1, ATTN_WIDTH), lambda bi, ji: (0, 0))],
        out_specs=blk(ATTN_WIDTH),
        out_shape=jax.ShapeDtypeStruct((b, s, ATTN_WIDTH), BF16),
        scratch_shapes=[pltpu.VMEM((s, tq), I32),
                        pltpu.VMEM((ATTN_WIDTH, tq), F32),
                        pltpu.VMEM((N_HEADS, 1, tq), F32),
                        pltpu.VMEM((N_HEADS, 1, tq), F32),
                        pltpu.VMEM((IDX_HEADS * tq, LANES), BF16),
                        pltpu.VMEM((N_HEADS, tq, 2 * LANES), BF16),
                        pltpu.VMEM((1, tq), I32),
                        pltpu.VMEM((N_HEADS, ck, tq), F32),
                        pltpu.VMEM((s, tq), BF16)],
        compiler_params=pltpu.CompilerParams(
            dimension_semantics=("arbitrary", "arbitrary"), vmem_limit_bytes=56 * 1024 * 1024),
        name="attn",
    )(q, qi, wt, kid, k, kf, vt, attn_out_g.reshape(1, ATTN_WIDTH))


def _outproj_kernel(pool_ref, attn_ref, x_ref, wo_ref, g1_ref, sh_ref, sc_ref, ng_ref,
                    x1_ref, h2t_ref):
    mixed = _dot(pool_ref[0], wo_ref[0:POOL_WIDTH, :]) + _dot(attn_ref[0], wo_ref[POOL_WIDTH:, :])
    x1 = x_ref[0] + g1_ref[0] * mixed
    x1_ref[0] = x1
    ms = jnp.mean(x1 * x1, axis=-1, keepdims=True)
    h2 = (x1 * lax.rsqrt(ms + EPS) * ng_ref[...]) * (1.0 + sc_ref[0]) + sh_ref[0]
    h2t_ref[...] = h2.T.astype(BF16)


def _outproj(pool_out, attn_out, x, w_out, g1, sh2, sc2, norm_g, tm):
    b, s, d = x.shape
    nt = s // tm
    tok = lambda w: pl.BlockSpec((1, tm, w), lambda bi, ti: (bi, ti, 0))
    per_b = pl.BlockSpec((1, 1, d), lambda bi, ti: (bi, 0, 0))
    return pl.pallas_call(
        _outproj_kernel,
        grid=(b, nt),
        in_specs=[tok(POOL_WIDTH), tok(ATTN_WIDTH), tok(d),
                  pl.BlockSpec((POOL_WIDTH + ATTN_WIDTH, d), lambda bi, ti: (0, 0)),
                  per_b, per_b, per_b, pl.BlockSpec((1, d), lambda bi, ti: (0, 0))],
        out_specs=(tok(d), pl.BlockSpec((d, tm), lambda bi, ti: (0, bi * nt + ti))),
        out_shape=(jax.ShapeDtypeStruct((b, s, d), F32), jax.ShapeDtypeStruct((d, b * s), BF16)),
        compiler_params=pltpu.CompilerParams(dimension_semantics=("arbitrary", "arbitrary")),
        name="outproj",
    )(pool_out, attn_out, x, w_out.astype(BF16), g1, sh2, sc2, norm_g.reshape(1, d))


def _batcher_pairs(n):
    pairs = []

    def merge(lo, cnt, r):
        step = r * 2
        if step < cnt:
            merge(lo, cnt, step)
            merge(lo + r, cnt, step)
            for i in range(lo + r, lo + cnt - r, step):
                pairs.append((i, i + r))
        else:
            pairs.append((lo, lo + r))

    def sort(lo, cnt):
        if cnt > 1:
            m = cnt // 2
            sort(lo, m)
            sort(lo + m, m)
            merge(lo, cnt, 1)

    sort(0, n)
    return tuple(pairs)


_SORT16 = _batcher_pairs(PEER_TOPK)
MM_GROUP = 8
MM_COLS = 256


def _top16_desc(x):
    x = list(x)
    for i, j in _SORT16:
        x[i], x[j] = jnp.maximum(x[i], x[j]), jnp.minimum(x[i], x[j])
    for shift in (4, 2, 1):
        y = [jnp.maximum(x[k], pltpu.roll(x[PEER_TOPK - 1 - k], shift, 0)) for k in range(PEER_TOPK)]
        for d in (8, 4, 2, 1):
            for k in range(PEER_TOPK):
                if k & d == 0:
                    y[k], y[k + d] = jnp.maximum(y[k], y[k + d]), jnp.minimum(y[k], y[k + d])
        x = y
    return x


def _peer_kernel(h2t_ref, x1_ref, g2_ref, wqt_ref, sk_ref, u_ref, vt_ref, o_ref,
                 s1_ref, s2_ref, thr_ref, p_ref, acc_ref, s1b_ref, *, tm, eb, ne):
    e = pl.program_id(1)
    nsub = eb // PEER_KEYS
    ntl = tm // LANES

    @pl.when(e == 0)
    def _prologue():
        h2t = h2t_ref[...]
        sub = lax.broadcasted_iota(I32, (8, tm), 0)
        ninf = jnp.full((8, tm), -jnp.inf, F32)

        def compact(t, base):
            out = t[base + 7]
            for r in range(6, -1, -1):
                out = jnp.where(sub == r, t[base + r], out)
            return out

        def split8(st):
            return [st[k * 8:(k + 1) * 8, :] for k in range(PEER_KEYS // 8)]

        def head(hd, carry):
            q1 = _dot(wqt_ref[2 * hd], h2t).astype(BF16)
            q2 = _dot(wqt_ref[2 * hd + 1], h2t).astype(BF16)
            st1 = _dot(sk_ref[2 * hd], q1) * LOG2E
            st2 = _dot(sk_ref[2 * hd + 1], q2) * LOG2E
            for tl in range(ntl):
                s2_ref[hd, tl] = st2[:, tl * LANES:(tl + 1) * LANES]
            t1 = _top16_desc(split8(st1))
            t2 = _top16_desc(split8(st2))
            a2lo, a2hi = compact(t2, 0), compact(t2, 8)

            def candidates(t1x):
                cands = [t1x[0] + a2lo, t1x[0] + a2hi, t1x[1] + a2lo]
                for r1 in range(2, 8):
                    cands.append(jnp.where(sub < PEER_TOPK // (r1 + 1), t1x[r1] + a2lo, ninf))
                cands.append(compact(t1x, 8) + t2[0])
                return cands

            def kth(cands):
                return _top16_desc(cands + [ninf] * (PEER_TOPK - len(cands)))[PEER_TOPK - 1]

            cands = candidates(t1)
            thr = kth(cands)
            m = t1[0] + t2[0]
            z8 = jnp.zeros((8, tm), F32)
            for c in cands:
                z8 = z8 + jnp.where(c >= thr, jnp.exp2(c - m), 0.0)
            mz = m + jnp.log2(jnp.broadcast_to(jnp.sum(z8, axis=0, keepdims=True), (8, tm)))
            s1_ref[hd] = st1 - mz[0:1, :]
            thr_ref[pl.ds(hd, 1), :] = kth(candidates([t - mz for t in t1]))[0:1, :]
            return carry

        lax.fori_loop(0, PEER_HEADS, head, 0)
        acc_ref[...] = jnp.zeros_like(acc_ref)

    i0 = pl.multiple_of(e * nsub, nsub)
    s1rows = [s1_ref[hd, pl.ds(i0, nsub), :] for hd in range(PEER_HEADS)]
    for hd in range(PEER_HEADS):
        s1b_ref[hd] = s1rows[hd]
    gsz = MM_GROUP * PEER_KEYS
    for g in range(nsub // MM_GROUP):
        grows = slice(g * gsz, (g + 1) * gsz)
        a_grp = _dot(u_ref[grows, :], h2t_ref[...])
        for sg in range(MM_GROUP):
            ii = g * MM_GROUP + sg
            rows = slice(ii * PEER_KEYS, (ii + 1) * PEER_KEYS)
            for tl in range(ntl):
                cols = slice(tl * LANES, (tl + 1) * LANES)
                w = jnp.zeros((PEER_KEYS, LANES), F32)
                for hd in range(PEER_HEADS):
                    sm = s1b_ref[hd, ii:ii + 1, cols] + s2_ref[hd, tl]
                    w = w + jnp.where(sm >= thr_ref[hd:hd + 1, cols], jnp.exp2(sm), 0.0)
                a = a_grp[sg * PEER_KEYS:(sg + 1) * PEER_KEYS, cols]
                gl = 0.5 * a * (1.0 + lax.erf(a * (1.0 / math.sqrt(2.0))))
                p_ref[rows, cols] = (gl * w).astype(BF16)
        acc_ref[...] += _dot(vt_ref[:, grows], p_ref[grows, :])

    @pl.when(e == ne - 1)
    def _epilogue():
        o_ref[...] = x1_ref[...] + g2_ref[0] * acc_ref[...].T


def _peer(h2t, x1, g2, wq, sub_keys, u_tab, v_tab, seq, tm, eb):
    d, t = h2t.shape
    ne = u_tab.shape[0] // eb
    half = PEER_KEYS
    wqt = wq.T.reshape(2 * PEER_HEADS, half, d).astype(BF16)
    sk = sub_keys.reshape(2 * PEER_HEADS, PEER_KEYS, half).astype(BF16)
    ub = u_tab.astype(BF16)
    vtb = v_tab.T.astype(BF16)
    hs = (PEER_HEADS, PEER_KEYS, tm)
    return pl.pallas_call(
        functools.partial(_peer_kernel, tm=tm, eb=eb, ne=ne),
        grid=(t // tm, ne),
        in_specs=[pl.BlockSpec((d, tm), lambda i, e: (0, i)),
                  pl.BlockSpec((tm, d), lambda i, e: (i, 0)),
                  pl.BlockSpec((1, 1, d), lambda i, e: ((i * tm) // seq, 0, 0)),
                  pl.BlockSpec((2 * PEER_HEADS, half, d), lambda i, e: (0, 0, 0)),
                  pl.BlockSpec((2 * PEER_HEADS, PEER_KEYS, half), lambda i, e: (0, 0, 0)),
                  pl.BlockSpec((eb, d), lambda i, e: (e, 0)),
                  pl.BlockSpec((d, eb), lambda i, e: (0, e))],
        out_specs=pl.BlockSpec((tm, d), lambda i, e: (i, 0)),
        out_shape=jax.ShapeDtypeStruct((t, d), F32),
        scratch_shapes=[pltpu.VMEM(hs, F32),
                        pltpu.VMEM((PEER_HEADS, tm // LANES, PEER_KEYS, LANES), F32),
                        pltpu.VMEM((PEER_HEADS, tm), F32),
                        pltpu.VMEM((eb, tm), BF16),
                        pltpu.VMEM((d, tm), F32),
                        pltpu.VMEM((PEER_HEADS, eb // PEER_KEYS, tm), F32)],
        compiler_params=pltpu.CompilerParams(
            dimension_semantics=("arbitrary", "arbitrary"), vmem_limit_bytes=58 * 1024 * 1024),
        name="peer",
    )(h2t, x1, g2, wqt, sk, ub, vtb)


def _tile(n, pref):
    t = min(pref, n)
    assert n % t == 0
    return t


def kernel(x, c, ada_w, ada_b, norm1_g, w_in, pool_w, pool_scale, pool_out_g, q_norm_g, k_norm_g,
           attn_out_g, w_out, norm2_g, peer_wq, peer_subkeys, peer_u, peer_v):
    b, s, d = x.shape
    depth = ada_w.shape[0]
    tm = _tile(s, 512)
    tq = _tile(s, 2 * LANES)
    ck = _tile(tm, 512)
    tp = _tile(s, 512)
    eb = 2048
    for l in range(depth):
        mod = _ada(c, ada_w[l], ada_b[l]).reshape(b, 1, 6 * d)
        sh1, sc1, g1, sh2, sc2, g2 = jnp.split(mod, 6, axis=-1)
        q, k, qi, kid, vt, wt, pool_out = _inproj(
            x, sh1, sc1, norm1_g[l], w_in[l], pool_w[l], pool_scale[l], pool_out_g[l],
            q_norm_g[l], k_norm_g[l], tm, ck)
        attn_out = _attn(q, k, qi, kid, vt, wt, attn_out_g[l], tq, ck)
        x1, h2t = _outproj(pool_out, attn_out, x, w_out[l], g1, sh2, sc2, norm2_g[l], tm)
        x = _peer(h2t, x1.reshape(b * s, d), g2, peer_wq[l], peer_subkeys[l],
                  peer_u[l], peer_v[l], s, tp, eb).reshape(b, s, d)
    return x
```

```python
import functools
import math

import ml_dtypes
import numpy as np

import jax
import jax.numpy as jnp
from jax import lax
from jax.experimental import pallas as pl
from jax.experimental.pallas import tpu as pltpu

F32 = jnp.float32
BF16 = jnp.bfloat16
I32 = jnp.int32

CHUNK = 64
EPS = 1e-6
POOL_WINDOWS = (2, 4, 8, 16)
POOL_GROUP = 128
POOL_WIDTH = 512
ATTN_WIDTH = 512
HEAD_DIM = 64
N_HEADS = 8
IDX_HEADS = 8
IDX_DIM = 64
TOPK_MAX = 256
PEER_HEADS = 8
PEER_KEYS = 128
PEER_TOPK = 16
NEG = -1e30

LANES = 128
POOL_HALO = 16
LOG2E = math.log2(math.e)
M_INIT = 0.5 * NEG
N_FEAT = 9

NT_DIMS = (((1,), (1,)), ((), ()))


def _dot(a, b):
    return jnp.dot(a, b, preferred_element_type=F32)


def _dot_nt(a, b):
    return lax.dot_general(a, b, NT_DIMS, preferred_element_type=F32)


def _f32_key(v):
    b = int(np.float32(v).view(np.int32))
    return b ^ 0x7FFFFFFF if b < 0 else b


def _bf16_split3(c):
    c = np.float32(c)
    p1 = np.float32(c.astype(ml_dtypes.bfloat16))
    r = np.float32(c - p1)
    p2 = np.float32(r.astype(ml_dtypes.bfloat16))
    p3 = np.float32(np.float32(r - p2).astype(ml_dtypes.bfloat16))
    return float(p1), float(p2), float(p3)


_SLOPE_L2E = tuple(float(np.float32(2.0 ** (-8.0 * (h + 1) / N_HEADS) * LOG2E)) for h in range(N_HEADS))


def _ada_kernel(c_ref, w_ref, b_ref, o_ref):
    c = c_ref[...]
    cond = c * jax.nn.sigmoid(c)
    o_ref[...] = jnp.dot(cond, w_ref[...], preferred_element_type=F32,
                         precision=lax.Precision.HIGHEST) + b_ref[...]


def _ada(c, w, b):
    bsz, d = c.shape
    n = w.shape[1]
    tn = 1024
    return pl.pallas_call(
        _ada_kernel,
        grid=(n // tn,),
        in_specs=[pl.BlockSpec((bsz, d), lambda j: (0, 0)),
                  pl.BlockSpec((d, tn), lambda j: (0, j)),
                  pl.BlockSpec((1, tn), lambda j: (0, j))],
        out_specs=pl.BlockSpec((bsz, tn), lambda j: (0, j)),
        out_shape=jax.ShapeDtypeStruct((bsz, n), F32),
        name="ada",
    )(c, w, b.reshape(1, n))


def _group_sumsq(t, ones_bd):
    sq = t * t
    hi = sq.astype(BF16)
    lo = (sq - hi.astype(F32)).astype(BF16)
    return _dot(hi, ones_bd) + _dot(lo, ones_bd)


def _inproj_kernel(x_ref, sh_ref, sc_ref, g_ref, wnat_ref, wki_ref, wvt_ref, wwt_ref, ones_ref,
                   qg_ref, kg_ref, poolw_ref, pscale_ref, pg_ref,
                   q_ref, k_ref, qi_ref, kid_ref, vt_ref, wt_ref, pool_ref,
                   ubuf, *, tm, kc):
    i = pl.program_id(1)
    x = x_ref[0]
    ms = jnp.mean(x * x, axis=-1, keepdims=True)
    h = (x * lax.rsqrt(ms + EPS) * g_ref[...]) * (1.0 + sc_ref[0]) + sh_ref[0]
    hb = h.astype(BF16)

    proj = _dot(hb, wnat_ref[...])
    u = proj[:, 0:512]
    q = proj[:, 512:1024]
    k = proj[:, 1024:1536]
    qi = proj[:, 1536:2048]

    ones_bd = ones_ref[...]
    qn = q * lax.rsqrt(_group_sumsq(q, ones_bd) * (1.0 / HEAD_DIM) + EPS) * qg_ref[...]
    kn = k * lax.rsqrt(_group_sumsq(k, ones_bd) * (1.0 / HEAD_DIM) + EPS) * kg_ref[...]
    q_ref[0] = qn.astype(BF16)
    k_ref[0] = kn.astype(BF16)
    qi_ref[0] = (qi * (IDX_DIM ** -0.5)).astype(BF16)
    kid_ref[0] = _dot(hb, wki_ref[...]).astype(BF16)
    vt = _dot_nt(wvt_ref[...], hb).astype(BF16)
    for ci in range(tm // kc):
        vt_ref[0, ci] = vt[:, ci * kc:(ci + 1) * kc]
    wt = _dot_nt(wwt_ref[...], hb)
    wt_ref[0] = wt[0:IDX_HEADS, :] * (IDX_HEADS ** -0.5)

    @pl.when(i == 0)
    def _():
        ubuf[0:POOL_HALO, :] = jnp.zeros((POOL_HALO, POOL_WIDTH), F32)

    ubuf[POOL_HALO:POOL_HALO + tm, :] = u
    tpos = i * tm + lax.broadcasted_iota(I32, (tm, POOL_GROUP), 0)
    parts = []
    for gi, win in enumerate(POOL_WINDOWS):
        lo_l, hi_l = gi * POOL_GROUP, (gi + 1) * POOL_GROUP
        ws = ubuf[POOL_HALO:POOL_HALO + tm, lo_l:hi_l]
        for j in range(1, win):
            ws = ws + ubuf[POOL_HALO - j:POOL_HALO - j + tm, lo_l:hi_l]
        cnt = jnp.minimum(tpos + 1, win).astype(F32)
        pooled = ws / cnt - u[:, lo_l:hi_l]
        parts.append(_dot(pooled.astype(BF16), poolw_ref[gi]))
    mixed = jnp.concatenate(parts, axis=-1) * pscale_ref[...]
    pms = jnp.mean(mixed * mixed, axis=-1, keepdims=True)
    pool_ref[0] = (mixed * lax.rsqrt(pms + EPS) * pg_ref[...]).astype(BF16)
    ubuf[0:POOL_HALO, :] = ubuf[tm:tm + POOL_HALO, :]


def _inproj(x, sh1, sc1, norm_g, w_in, pool_w, pool_scale, pool_out_g, q_norm_g, k_norm_g, tm, kc):
    b, s, d = x.shape
    nt = s // tm
    w_u, w_q, w_k, w_v, w_qi, w_ki, w_wi = jnp.split(
        w_in, (512, 1024, 1536, 2048, 2560, 2624), axis=-1)
    wnat = jnp.concatenate([w_u, w_q, w_k, w_qi], axis=-1).astype(BF16)
    wki = jnp.concatenate([w_ki, w_ki], axis=-1).astype(BF16)
    wvt = w_v.T.astype(BF16)
    wwt = jnp.concatenate([w_wi.T, jnp.zeros((16 - IDX_HEADS, d), F32)], axis=0).astype(BF16)
    gid = jnp.arange(ATTN_WIDTH) // HEAD_DIM
    ones_bd = (gid[:, None] == gid[None, :]).astype(BF16)
    qg = (jnp.tile(q_norm_g, N_HEADS) * (HEAD_DIM ** -0.5 * LOG2E)).reshape(1, ATTN_WIDTH)
    kg = jnp.tile(k_norm_g, N_HEADS).reshape(1, ATTN_WIDTH)

    full = lambda shape: pl.BlockSpec(shape, lambda bi, ti: (0,) * len(shape))
    tok = lambda w: pl.BlockSpec((1, tm, w), lambda bi, ti: (bi, ti, 0))
    per_b = pl.BlockSpec((1, 1, d), lambda bi, ti: (bi, 0, 0))
    out_shapes = (
        jax.ShapeDtypeStruct((b, s, ATTN_WIDTH), BF16),
        jax.ShapeDtypeStruct((b, s, ATTN_WIDTH), BF16),
        jax.ShapeDtypeStruct((b, s, IDX_HEADS * IDX_DIM), BF16),
        jax.ShapeDtypeStruct((b, s, 2 * IDX_DIM), BF16),
        jax.ShapeDtypeStruct((b, s // kc, ATTN_WIDTH, kc), BF16),
        jax.ShapeDtypeStruct((b, IDX_HEADS, s), F32),
        jax.ShapeDtypeStruct((b, s, POOL_WIDTH), BF16),
    )
    out_specs = (
        tok(ATTN_WIDTH), tok(ATTN_WIDTH), tok(IDX_HEADS * IDX_DIM), tok(2 * IDX_DIM),
        pl.BlockSpec((1, tm // kc, ATTN_WIDTH, kc), lambda bi, ti: (bi, ti, 0, 0)),
        pl.BlockSpec((1, IDX_HEADS, tm), lambda bi, ti: (bi, 0, ti)),
        tok(POOL_WIDTH),
    )
    return pl.pallas_call(
        functools.partial(_inproj_kernel, tm=tm, kc=kc),
        grid=(b, nt),
        in_specs=[tok(d), per_b, per_b, full((1, d)), full((d, 2048)), full((d, 2 * IDX_DIM)),
                  full((ATTN_WIDTH, d)), full((16, d)), full((ATTN_WIDTH, ATTN_WIDTH)),
                  full((1, ATTN_WIDTH)), full((1, ATTN_WIDTH)),
                  full((len(POOL_WINDOWS), POOL_GROUP, POOL_GROUP)),
                  full((1, POOL_WIDTH)), full((1, POOL_WIDTH))],
        out_specs=out_specs,
        out_shape=out_shapes,
        scratch_shapes=[pltpu.VMEM((tm + POOL_HALO, POOL_WIDTH), F32)],
        compiler_params=pltpu.CompilerParams(
            dimension_semantics=("arbitrary", "arbitrary"), vmem_limit_bytes=48 * 1024 * 1024),
        name="inproj",
    )(x, sh1, sc1, norm_g.reshape(1, d), wnat, wki, wvt, wwt, ones_bd, qg, kg,
      pool_w.astype(BF16), pool_scale.reshape(1, -1), pool_out_g.reshape(1, -1))


def _attn_kernel(q_ref, qi_ref, wt_ref, kid_ref, k_ref, kf_ref, vt_ref, aog_ref, o_ref,
                 key_ref, acc_ref, m_ref, l_ref, qim_ref, qmf_ref, cut_ref, lg_ref, hi_ref,
                 *, tq, ck, seq, topk):
    j = pl.program_id(1)
    nck = ((j + 1) * tq + ck - 1) // ck
    qpos = j * tq + lax.broadcasted_iota(I32, (1, tq), 1)
    qchunk = qpos // CHUNK
    row_i = lax.broadcasted_iota(I32, (ck, tq), 0)

    lane = lax.broadcasted_iota(I32, (tq, LANES), 1)
    tcol = (j * tq + lax.broadcasted_iota(I32, (tq, LANES), 0)).astype(F32)
    for h in range(N_HEADS):
        p = h // 2
        keep = (lane < HEAD_DIM) if h % 2 == 0 else (lane >= HEAD_DIM)
        qi_slab = qi_ref[0, :, p * LANES:(p + 1) * LANES]
        q_slab = q_ref[0, :, p * LANES:(p + 1) * LANES]
        qim_ref[h * tq:(h + 1) * tq, :] = jnp.where(keep, qi_slab, jnp.zeros_like(qi_slab))
        qmf_ref[h, :, 0:LANES] = jnp.where(keep, q_slab, jnp.zeros_like(q_slab))
        c1, c2, c3 = _bf16_split3(_SLOPE_L2E[h])
        u = -_SLOPE_L2E[h] * tcol
        u1 = u.astype(BF16).astype(F32)
        r = u - u1
        u2 = r.astype(BF16).astype(F32)
        u3 = (r - u2).astype(BF16).astype(F32)
        feat = jnp.zeros((tq, LANES), F32)
        for li, val in enumerate((64.0 * c1, 64.0 * c2, 64.0 * c3, c1, c2, c3, u1, u2, u3)):
            feat = jnp.where(lane == li, val, feat)
        qmf_ref[h, :, LANES:2 * LANES] = feat.astype(BF16)

    def scores(c, last):
        r0 = pl.multiple_of(c * ck, ck)
        d = _dot_nt(kid_ref[0, pl.ds(r0, ck), :], qim_ref[...])
        acc = jnp.zeros((ck, tq), F32)
        for h in range(IDX_HEADS):
            acc = acc + wt_ref[0, h:h + 1, :] * jnp.maximum(d[:, h * tq:(h + 1) * tq], 0.0)
        if last:
            acc = jnp.where(((r0 + row_i) // CHUNK) <= qchunk, acc, NEG)
        bits = pltpu.bitcast(acc, I32)
        key_ref[pl.ds(r0, ck), :] = jnp.where(bits < 0, bits ^ 0x7FFFFFFF, bits)
        hi_ref[pl.ds(r0, ck), :] = pltpu.bitcast(bits & -65536, F32).astype(BF16)

    def scores_body(c, carry):
        scores(c, False)
        return carry

    lax.fori_loop(0, nck - 1, scores_body, 0)
    scores(nck - 1, True)

    def sum_chunks(chunk_fn, init):
        def body(i, acc):
            r0 = pl.multiple_of(i * (2 * ck), 2 * ck)
            return acc + chunk_fn(r0) + chunk_fn(pl.multiple_of(r0 + ck, ck))
        acc = lax.fori_loop(0, nck // 2, body, init)
        return lax.cond(nck % 2 == 1,
                        lambda a: a + chunk_fn(pl.multiple_of((nck - 1) * ck, ck)),
                        lambda a: a, acc)

    def count(pred):
        def chunk(r0):
            m = pred(key_ref[pl.ds(r0, ck), :], r0).astype(I32)
            return m.reshape(ck // 8, 8, tq).sum(axis=0)
        return sum_chunks(chunk, jnp.zeros((8, tq), I32)).sum(axis=0, keepdims=True)

    def count_hi(trial16):
        pat = jnp.where(trial16 < 0, trial16 ^ 0x7FFF, trial16)
        tb = pltpu.bitcast(jnp.left_shift(pat, 16), F32).astype(BF16)

        def chunk(r0):
            m = hi_ref[pl.ds(r0, ck), :] >= tb
            ones = jnp.where(m, jnp.ones((ck, tq), BF16), jnp.zeros((ck, tq), BF16))
            parts = [ones[i * 16:(i + 1) * 16, :] for i in range(ck // 16)]
            while len(parts) > 1:
                parts = [parts[i] + parts[i + 1] for i in range(0, len(parts), 2)]
            return parts[0].astype(F32)

        cacc = sum_chunks(chunk, jnp.zeros((16, tq), F32))
        return cacc.sum(axis=0, keepdims=True).astype(I32)

    c_nonneg = count_hi(jnp.zeros((1, tq), I32))
    ok0 = c_nonneg >= topk
    state0 = (jnp.where(ok0, 0, -(2 ** 15)).astype(I32),
              jnp.where(ok0, c_nonneg, nck * ck), jnp.where(ok0, 0, c_nonneg))

    def hi_body(b, state):
        lo16, c_ge, c_gt = state
        trial = lo16 + jnp.left_shift(jnp.int32(1), 14 - b)
        c = count_hi(trial)
        ok = c >= topk
        return jnp.where(ok, trial, lo16), jnp.where(ok, c, c_ge), jnp.where(ok, c_gt, c)

    lo16, c_ge, c_gt = lax.fori_loop(0, 15, hi_body, state0)

    def bit_body(b, state):
        lo, c_ge, c_gt = state
        trial = lo + jnp.left_shift(jnp.int32(1), 15 - b)
        c = count(lambda kk, r0: kk >= trial)
        ok = c >= topk
        return jnp.where(ok, trial, lo), jnp.where(ok, c, c_ge), jnp.where(ok, c_gt, c)

    lo, c_ge, c_gt = lax.fori_loop(0, 16, bit_body, (jnp.left_shift(lo16, 16), c_ge, c_gt))

    need = topk - c_gt
    cut_ref[...] = jnp.full((1, tq), seq, I32)

    @pl.when(jnp.max(c_ge) > topk)
    def _():
        nbits = max(1, (seq - 1).bit_length())

        def idx_body(b, jv):
            trial = jv + jnp.left_shift(jnp.int32(1), nbits - 1 - b)
            c = count(lambda kk, r0: (kk == lo)
                      & ((r0 + lax.broadcasted_iota(I32, (ck, tq), 0)) < trial))
            return jnp.where(c < need, trial, jv)

        jv = lax.fori_loop(0, nbits, idx_body, jnp.zeros((1, tq), I32))
        cut_ref[...] = jv + 1

    cut = cut_ref[...]

    m_ref[...] = jnp.full((N_HEADS, 1, tq), M_INIT, F32)
    l_ref[...] = jnp.zeros((N_HEADS, 1, tq), F32)
    acc_ref[...] = jnp.zeros((ATTN_WIDTH, tq), F32)

    def attend(c, last):
        r0 = pl.multiple_of(c * ck, ck)
        kk = key_ref[pl.ds(r0, ck), :]
        kf = kf_ref[pl.ds(r0, ck), :]
        kch = k_ref[0, pl.ds(r0, ck), :]
        vtc = vt_ref[0, c]
        kpos = r0 + row_i
        sel = (kk > lo) | ((kk == lo) & (kpos < cut))
        if last:
            sel = sel & ((kpos // CHUNK) <= qchunk)
            ahead = jnp.maximum(kpos - qpos, 0).astype(F32)
        pen = jnp.where(sel, 0.0, NEG)
        for h in range(N_HEADS):
            p = h // 2
            ks = kch[:, p * LANES:(p + 1) * LANES]
            lg = _dot_nt(jnp.concatenate([ks, kf], axis=1), qmf_ref[h]) + pen
            if last:
                lg = lg - (2.0 * _SLOPE_L2E[h]) * ahead
            lg_ref[h] = lg
        for h in range(N_HEADS):
            m_old = m_ref[h]
            m_new = jnp.maximum(m_old, jnp.max(lg_ref[h], axis=0, keepdims=True))
            alpha = jnp.exp2(m_old - m_new)
            pr = jnp.exp2(lg_ref[h] - m_new)
            l_ref[h] = alpha * l_ref[h] + jnp.sum(pr, axis=0, keepdims=True)
            m_ref[h] = m_new
            rows = slice(h * HEAD_DIM, (h + 1) * HEAD_DIM)
            acc_ref[rows, :] = alpha * acc_ref[rows, :] + _dot(vtc[rows, :], pr.astype(BF16))

    def attend_body(c, carry):
        attend(c, False)
        return carry

    lax.fori_loop(0, nck - 1, attend_body, 0)
    attend(nck - 1, True)

    outs = []
    for h in range(N_HEADS):
        rows = slice(h * HEAD_DIM, (h + 1) * HEAD_DIM)
        o = acc_ref[rows, :] / l_ref[h]
        oms = jnp.mean(o * o, axis=0, keepdims=True)
        outs.append(o * lax.rsqrt(oms + EPS))
    ot = jnp.concatenate(outs, axis=0)
    o_ref[0] = (ot.T * aog_ref[...]).astype(BF16)


def _attn(q, k, qi, kid, vt, wt, attn_out_g, tq, ck):
    b, s, _ = q.shape
    topk = min(TOPK_MAX, s // 4)
    assert tq % LANES == 0 and ck % tq == 0 and s % ck == 0 and s // CHUNK <= 256
    pos = jnp.arange(s)
    feats = [pos // CHUNK] * 3 + [pos % CHUNK] * 3 + [jnp.ones_like(pos)] * 3
    kf = jnp.zeros((s, LANES), F32).at[:, 0:N_FEAT].set(
        jnp.stack(feats, axis=1).astype(F32)).astype(BF16)
    blk = lambda w: pl.BlockSpec((1, tq, w), lambda bi, ji: (bi, ji, 0))
    once = pl.Buffered(1)
    whole = lambda w: pl.BlockSpec((1, s, w), lambda bi, ji: (bi, 0, 0), pipeline_mode=once)
    return pl.pallas_call(
        functools.partial(_attn_kernel, tq=tq, ck=ck, seq=s, topk=topk),
        grid=(b, s // tq),
        in_specs=[blk(ATTN_WIDTH), blk(IDX_HEADS * IDX_DIM),
                  pl.BlockSpec((1, IDX_HEADS, tq), lambda bi, ji: (bi, 0, ji)),
                  whole(2 * IDX_DIM), whole(ATTN_WIDTH),
                  pl.BlockSpec((s, LANES), lambda bi, ji: (0, 0), pipeline_mode=once),
                  pl.BlockSpec((1, s // ck, ATTN_WIDTH, ck), lambda bi, ji: (bi, 0, 0, 0),
                               pipeline_mode=once),
                  pl.BlockSpec((1, ATTN_WIDTH), lambda bi, ji: (0, 0))],
        out_specs=blk(ATTN_WIDTH),
        out_shape=jax.ShapeDtypeStruct((b, s, ATTN_WIDTH), BF16),
        scratch_shapes=[pltpu.VMEM((s, tq), I32),
                        pltpu.VMEM((ATTN_WIDTH, tq), F32),
                        pltpu.VMEM((N_HEADS, 1, tq), F32),
                        pltpu.VMEM((N_HEADS, 1, tq), F32),
                        pltpu.VMEM((IDX_HEADS * tq, LANES), BF16),
                        pltpu.VMEM((N_HEADS, tq, 2 * LANES), BF16),
                        pltpu.VMEM((1, tq), I32),
                        pltpu.VMEM((N_HEADS, ck, tq), F32),
                        pltpu.VMEM((s, tq), BF16)],
        compiler_params=pltpu.CompilerParams(
            dimension_semantics=("arbitrary", "arbitrary"), vmem_limit_bytes=56 * 1024 * 1024),
        name="attn",
    )(q, qi, wt, kid, k, kf, vt, attn_out_g.reshape(1, ATTN_WIDTH))


def _outproj_kernel(pool_ref, attn_ref, x_ref, wo_ref, g1_ref, sh_ref, sc_ref, ng_ref,
                    x1_ref, h2t_ref):
    mixed = _dot(pool_ref[0], wo_ref[0:POOL_WIDTH, :]) + _dot(attn_ref[0], wo_ref[POOL_WIDTH:, :])
    x1 = x_ref[0] + g1_ref[0] * mixed
    x1_ref[0] = x1
    ms = jnp.mean(x1 * x1, axis=-1, keepdims=True)
    h2 = (x1 * lax.rsqrt(ms + EPS) * ng_ref[...]) * (1.0 + sc_ref[0]) + sh_ref[0]
    h2t_ref[...] = h2.T.astype(BF16)


def _outproj(pool_out, attn_out, x, w_out, g1, sh2, sc2, norm_g, tm):
    b, s, d = x.shape
    nt = s // tm
    tok = lambda w: pl.BlockSpec((1, tm, w), lambda bi, ti: (bi, ti, 0))
    per_b = pl.BlockSpec((1, 1, d), lambda bi, ti: (bi, 0, 0))
    return pl.pallas_call(
        _outproj_kernel,
        grid=(b, nt),
        in_specs=[tok(POOL_WIDTH), tok(ATTN_WIDTH), tok(d),
                  pl.BlockSpec((POOL_WIDTH + ATTN_WIDTH, d), lambda bi, ti: (0, 0)),
                  per_b, per_b, per_b, pl.BlockSpec((1, d), lambda bi, ti: (0, 0))],
        out_specs=(tok(d), pl.BlockSpec((d, tm), lambda bi, ti: (0, bi * nt + ti))),
        out_shape=(jax.ShapeDtypeStruct((b, s, d), F32), jax.ShapeDtypeStruct((d, b * s), BF16)),
        compiler_params=pltpu.CompilerParams(dimension_semantics=("arbitrary", "arbitrary")),
        name="outproj",
    )(pool_out, attn_out, x, w_out.astype(BF16), g1, sh2, sc2, norm_g.reshape(1, d))


def _batcher_pairs(n):
    pairs = []

    def merge(lo, cnt, r):
        step = r * 2
        if step < cnt:
            merge(lo, cnt, step)
            merge(lo + r, cnt, step)
            for i in range(lo + r, lo + cnt - r, step):
                pairs.append((i, i + r))
        else:
            pairs.append((lo, lo + r))

    def sort(lo, cnt):
        if cnt > 1:
            m = cnt // 2
            sort(lo, m)
            sort(lo + m, m)
            merge(lo, cnt, 1)

    sort(0, n)
    return tuple(pairs)


_SORT16 = _batcher_pairs(PEER_TOPK)
MM_GROUP = 8
MM_COLS = 256


def _top16_desc(x):
    x = list(x)
    for i, j in _SORT16:
        x[i], x[j] = jnp.maximum(x[i], x[j]), jnp.minimum(x[i], x[j])
    for shift in (4, 2, 1):
        y = [jnp.maximum(x[k], pltpu.roll(x[PEER_TOPK - 1 - k], shift, 0)) for k in range(PEER_TOPK)]
        for d in (8, 4, 2, 1):
            for k in range(PEER_TOPK):
                if k & d == 0:
                    y[k], y[k + d] = jnp.maximum(y[k], y[k + d]), jnp.minimum(y[k], y[k + d])
        x = y
    return x


def _peer_kernel(h2t_ref, x1_ref, g2_ref, wqt_ref, sk_ref, u_ref, vt_ref, o_ref,
                 s1_ref, s2_ref, thr_ref, p_ref, acc_ref, s1b_ref, *, tm, eb, ne):
    e = pl.program_id(1)
    nsub = eb // PEER_KEYS
    ntl = tm // LANES

    @pl.when(e == 0)
    def _prologue():
        h2t = h2t_ref[...]
        sub = lax.broadcasted_iota(I32, (8, tm), 0)
        ninf = jnp.full((8, tm), -jnp.inf, F32)

        def compact(t, base):
            out = t[base + 7]
            for r in range(6, -1, -1):
                out = jnp.where(sub == r, t[base + r], out)
            return out

        def split8(st):
            return [st[k * 8:(k + 1) * 8, :] for k in range(PEER_KEYS // 8)]

        def head(hd, carry):
            q1 = _dot(wqt_ref[2 * hd], h2t).astype(BF16)
            q2 = _dot(wqt_ref[2 * hd + 1], h2t).astype(BF16)
            st1 = _dot(sk_ref[2 * hd], q1) * LOG2E
            st2 = _dot(sk_ref[2 * hd + 1], q2) * LOG2E
            for tl in range(ntl):
                s2_ref[hd, tl] = st2[:, tl * LANES:(tl + 1) * LANES]
            t1 = _top16_desc(split8(st1))
            t2 = _top16_desc(split8(st2))
            a2lo, a2hi = compact(t2, 0), compact(t2, 8)

            def candidates(t1x):
                cands = [t1x[0] + a2lo, t1x[0] + a2hi, t1x[1] + a2lo]
                for r1 in range(2, 8):
                    cands.append(jnp.where(sub < PEER_TOPK // (r1 + 1), t1x[r1] + a2lo, ninf))
                cands.append(compact(t1x, 8) + t2[0])
                return cands

            def kth(cands):
                return _top16_desc(cands + [ninf] * (PEER_TOPK - len(cands)))[PEER_TOPK - 1]

            cands = candidates(t1)
            thr = kth(cands)
            m = t1[0] + t2[0]
            z8 = jnp.zeros((8, tm), F32)
            for c in cands:
                z8 = z8 + jnp.where(c >= thr, jnp.exp2(c - m), 0.0)
            mz = m + jnp.log2(jnp.broadcast_to(jnp.sum(z8, axis=0, keepdims=True), (8, tm))) + 1.0
            s1_ref[hd] = st1 - mz[0:1, :]
            thr_ref[pl.ds(hd, 1), :] = kth(candidates([t - mz for t in t1]))[0:1, :]
            return carry

        lax.fori_loop(0, PEER_HEADS, head, 0)
        acc_ref[...] = jnp.zeros_like(acc_ref)

    i0 = pl.multiple_of(e * nsub, nsub)
    s1rows = [s1_ref[hd, pl.ds(i0, nsub), :] for hd in range(PEER_HEADS)]
    for hd in range(PEER_HEADS):
        s1b_ref[hd] = s1rows[hd]
    gsz = MM_GROUP * PEER_KEYS
    for g in range(nsub // MM_GROUP):
        grows = slice(g * gsz, (g + 1) * gsz)
        a_grp = _dot(u_ref[grows, :], h2t_ref[...])
        for sg in range(MM_GROUP):
            ii = g * MM_GROUP + sg
            rows = slice(ii * PEER_KEYS, (ii + 1) * PEER_KEYS)
            for tl in range(ntl):
                cols = slice(tl * LANES, (tl + 1) * LANES)
                w = jnp.zeros((PEER_KEYS, LANES), F32)
                for hd in range(PEER_HEADS):
                    sm = s1b_ref[hd, ii:ii + 1, cols] + s2_ref[hd, tl]
                    w = w + jnp.where(sm >= thr_ref[hd:hd + 1, cols], jnp.exp2(sm), 0.0)
                a = a_grp[sg * PEER_KEYS:(sg + 1) * PEER_KEYS, cols]
                gl = a * (1.0 + lax.erf(a * (1.0 / math.sqrt(2.0))))
                p_ref[rows, cols] = (gl * w).astype(BF16)
        acc_ref[...] += _dot(vt_ref[:, grows], p_ref[grows, :])

    @pl.when(e == ne - 1)
    def _epilogue():
        o_ref[...] = x1_ref[...] + g2_ref[0] * acc_ref[...].T


def _peer(h2t, x1, g2, wq, sub_keys, u_tab, v_tab, seq, tm, eb):
    d, t = h2t.shape
    ne = u_tab.shape[0] // eb
    half = PEER_KEYS
    wqt = wq.T.reshape(2 * PEER_HEADS, half, d).astype(BF16)
    sk = sub_keys.reshape(2 * PEER_HEADS, PEER_KEYS, half).astype(BF16)
    ub = u_tab.astype(BF16)
    vtb = v_tab.T.astype(BF16)
    hs = (PEER_HEADS, PEER_KEYS, tm)
    return pl.pallas_call(
        functools.partial(_peer_kernel, tm=tm, eb=eb, ne=ne),
        grid=(t // tm, ne),
        in_specs=[pl.BlockSpec((d, tm), lambda i, e: (0, i)),
                  pl.BlockSpec((tm, d), lambda i, e: (i, 0)),
                  pl.BlockSpec((1, 1, d), lambda i, e: ((i * tm) // seq, 0, 0)),
                  pl.BlockSpec((2 * PEER_HEADS, half, d), lambda i, e: (0, 0, 0)),
                  pl.BlockSpec((2 * PEER_HEADS, PEER_KEYS, half), lambda i, e: (0, 0, 0)),
                  pl.BlockSpec((eb, d), lambda i, e: (e, 0)),
                  pl.BlockSpec((d, eb), lambda i, e: (0, e))],
        out_specs=pl.BlockSpec((tm, d), lambda i, e: (i, 0)),
        out_shape=jax.ShapeDtypeStruct((t, d), F32),
        scratch_shapes=[pltpu.VMEM(hs, F32),
                        pltpu.VMEM((PEER_HEADS, tm // LANES, PEER_KEYS, LANES), F32),
                        pltpu.VMEM((PEER_HEADS, tm), F32),
                        pltpu.VMEM((eb, tm), BF16),
                        pltpu.VMEM((d, tm), F32),
                        pltpu.VMEM((PEER_HEADS, eb // PEER_KEYS, tm), F32)],
        compiler_params=pltpu.CompilerParams(
            dimension_semantics=("arbitrary", "arbitrary"), vmem_limit_bytes=58 * 1024 * 1024),
        name="peer",
    )(h2t, x1, g2, wqt, sk, ub, vtb)


def _tile(n, pref):
    t = min(pref, n)
    assert n % t == 0
    return t


def kernel(x, c, ada_w, ada_b, norm1_g, w_in, pool_w, pool_scale, pool_out_g, q_norm_g, k_norm_g,
           attn_out_g, w_out, norm2_g, peer_wq, peer_subkeys, peer_u, peer_v):
    b, s, d = x.shape
    depth = ada_w.shape[0]
    tm = _tile(s, 512)
    tq = _tile(s, 2 * LANES)
    ck = _tile(tm, 512)
    tp = _tile(s, 512)
    eb = 2048
    for l in range(depth):
        mod = _ada(c, ada_w[l], ada_b[l]).reshape(b, 1, 6 * d)
        sh1, sc1, g1, sh2, sc2, g2 = jnp.split(mod, 6, axis=-1)
        q, k, qi, kid, vt, wt, pool_out = _inproj(
            x, sh1, sc1, norm1_g[l], w_in[l], pool_w[l], pool_scale[l], pool_out_g[l],
            q_norm_g[l], k_norm_g[l], tm, ck)
        attn_out = _attn(q, k, qi, kid, vt, wt, attn_out_g[l], tq, ck)
        x1, h2t = _outproj(pool_out, attn_out, x, w_out[l], g1, sh2, sc2, norm2_g[l], tm)
        x = _peer(h2t, x1.reshape(b * s, d), g2, peer_wq[l], peer_subkeys[l],
                  peer_u[l], peer_v[l], s, tp, eb).reshape(b, s, d)
    return x
```

```python
import functools
import math

import ml_dtypes
import numpy as np

import jax
import jax.numpy as jnp
from jax import lax
from jax.experimental import pallas as pl
from jax.experimental.pallas import tpu as pltpu

F32 = jnp.float32
BF16 = jnp.bfloat16
I32 = jnp.int32

CHUNK = 64
EPS = 1e-6
POOL_WINDOWS = (2, 4, 8, 16)
POOL_GROUP = 128
POOL_WIDTH = 512
ATTN_WIDTH = 512
HEAD_DIM = 64
N_HEADS = 8
IDX_HEADS = 8
IDX_DIM = 64
TOPK_MAX = 256
PEER_HEADS = 8
PEER_KEYS = 128
PEER_TOPK = 16
NEG = -1e30

LANES = 128
POOL_HALO = 16
LOG2E = math.log2(math.e)
M_INIT = 0.5 * NEG
N_FEAT = 9

NT_DIMS = (((1,), (1,)), ((), ()))


def _dot(a, b):
    return jnp.dot(a, b, preferred_element_type=F32)


def _dot_nt(a, b):
    return lax.dot_general(a, b, NT_DIMS, preferred_element_type=F32)


def _f32_key(v):
    b = int(np.float32(v).view(np.int32))
    return b ^ 0x7FFFFFFF if b < 0 else b


def _bf16_split3(c):
    c = np.float32(c)
    p1 = np.float32(c.astype(ml_dtypes.bfloat16))
    r = np.float32(c - p1)
    p2 = np.float32(r.astype(ml_dtypes.bfloat16))
    p3 = np.float32(np.float32(r - p2).astype(ml_dtypes.bfloat16))
    return float(p1), float(p2), float(p3)


_SLOPE_L2E = tuple(float(np.float32(2.0 ** (-8.0 * (h + 1) / N_HEADS) * LOG2E)) for h in range(N_HEADS))


def _ada_kernel(c_ref, w_ref, b_ref, o_ref):
    c = c_ref[...]
    cond = c * jax.nn.sigmoid(c)
    o_ref[...] = jnp.dot(cond, w_ref[...], preferred_element_type=F32,
                         precision=lax.Precision.HIGHEST) + b_ref[...]


def _ada(c, w, b):
    bsz, d = c.shape
    n = w.shape[1]
    tn = 1024
    return pl.pallas_call(
        _ada_kernel,
        grid=(n // tn,),
        in_specs=[pl.BlockSpec((bsz, d), lambda j: (0, 0)),
                  pl.BlockSpec((d, tn), lambda j: (0, j)),
                  pl.BlockSpec((1, tn), lambda j: (0, j))],
        out_specs=pl.BlockSpec((bsz, tn), lambda j: (0, j)),
        out_shape=jax.ShapeDtypeStruct((bsz, n), F32),
        name="ada",
    )(c, w, b.reshape(1, n))


def _group_sumsq(t, ones_bd):
    sq = t * t
    hi = sq.astype(BF16)
    lo = (sq - hi.astype(F32)).astype(BF16)
    return _dot(hi, ones_bd) + _dot(lo, ones_bd)


def _inproj_kernel(x_ref, sh_ref, sc_ref, g_ref, wnat_ref, wki_ref, wvt_ref, wwt_ref, ones_ref,
                   qg_ref, kg_ref, poolw_ref, pscale_ref, pg_ref,
                   q_ref, k_ref, qi_ref, kid_ref, vt_ref, wt_ref, pool_ref,
                   ubuf, *, tm, kc):
    i = pl.program_id(1)
    x = x_ref[0]
    ms = jnp.mean(x * x, axis=-1, keepdims=True)
    h = (x * lax.rsqrt(ms + EPS) * g_ref[...]) * (1.0 + sc_ref[0]) + sh_ref[0]
    hb = h.astype(BF16)

    proj = _dot(hb, wnat_ref[...])
    u = proj[:, 0:512]
    q = proj[:, 512:1024]
    k = proj[:, 1024:1536]
    qi = proj[:, 1536:2048]

    ones_bd = ones_ref[...]
    qn = q * lax.rsqrt(_group_sumsq(q, ones_bd) * (1.0 / HEAD_DIM) + EPS) * qg_ref[...]
    kn = k * lax.rsqrt(_group_sumsq(k, ones_bd) * (1.0 / HEAD_DIM) + EPS) * kg_ref[...]
    q_ref[0] = qn.astype(BF16)
    k_ref[0] = kn.astype(BF16)
    qi_ref[0] = (qi * (IDX_DIM ** -0.5)).astype(BF16)
    kid_ref[0] = _dot(hb, wki_ref[...]).astype(BF16)
    vt = _dot_nt(wvt_ref[...], hb).astype(BF16)
    for ci in range(tm // kc):
        vt_ref[0, ci] = vt[:, ci * kc:(ci + 1) * kc]
    wt = _dot_nt(wwt_ref[...], hb)
    wt_ref[0] = wt[0:IDX_HEADS, :] * (IDX_HEADS ** -0.5)

    @pl.when(i == 0)
    def _():
        ubuf[0:POOL_HALO, :] = jnp.zeros((POOL_HALO, POOL_WIDTH), F32)

    ubuf[POOL_HALO:POOL_HALO + tm, :] = u
    tpos = i * tm + lax.broadcasted_iota(I32, (tm, POOL_GROUP), 0)
    parts = []
    for gi, win in enumerate(POOL_WINDOWS):
        lo_l, hi_l = gi * POOL_GROUP, (gi + 1) * POOL_GROUP
        ws = ubuf[POOL_HALO:POOL_HALO + tm, lo_l:hi_l]
        for j in range(1, win):
            ws = ws + ubuf[POOL_HALO - j:POOL_HALO - j + tm, lo_l:hi_l]
        cnt = jnp.minimum(tpos + 1, win).astype(F32)
        pooled = ws / cnt - u[:, lo_l:hi_l]
        parts.append(_dot(pooled.astype(BF16), poolw_ref[gi]))
    mixed = jnp.concatenate(parts, axis=-1) * pscale_ref[...]
    pms = jnp.mean(mixed * mixed, axis=-1, keepdims=True)
    pool_ref[0] = (mixed * lax.rsqrt(pms + EPS) * pg_ref[...]).astype(BF16)
    ubuf[0:POOL_HALO, :] = ubuf[tm:tm + POOL_HALO, :]


def _inproj(x, sh1, sc1, norm_g, w_in, pool_w, pool_scale, pool_out_g, q_norm_g, k_norm_g, tm, kc):
    b, s, d = x.shape
    nt = s // tm
    w_u, w_q, w_k, w_v, w_qi, w_ki, w_wi = jnp.split(
        w_in, (512, 1024, 1536, 2048, 2560, 2624), axis=-1)
    wnat = jnp.concatenate([w_u, w_q, w_k, w_qi], axis=-1).astype(BF16)
    wki = jnp.concatenate([w_ki, w_ki], axis=-1).astype(BF16)
    wvt = w_v.T.astype(BF16)
    wwt = jnp.concatenate([w_wi.T, jnp.zeros((16 - IDX_HEADS, d), F32)], axis=0).astype(BF16)
    gid = jnp.arange(ATTN_WIDTH) // HEAD_DIM
    ones_bd = (gid[:, None] == gid[None, :]).astype(BF16)
    qg = (jnp.tile(q_norm_g, N_HEADS) * (HEAD_DIM ** -0.5 * LOG2E)).reshape(1, ATTN_WIDTH)
    kg = jnp.tile(k_norm_g, N_HEADS).reshape(1, ATTN_WIDTH)

    full = lambda shape: pl.BlockSpec(shape, lambda bi, ti: (0,) * len(shape))
    tok = lambda w: pl.BlockSpec((1, tm, w), lambda bi, ti: (bi, ti, 0))
    per_b = pl.BlockSpec((1, 1, d), lambda bi, ti: (bi, 0, 0))
    out_shapes = (
        jax.ShapeDtypeStruct((b, s, ATTN_WIDTH), BF16),
        jax.ShapeDtypeStruct((b, s, ATTN_WIDTH), BF16),
        jax.ShapeDtypeStruct((b, s, IDX_HEADS * IDX_DIM), BF16),
        jax.ShapeDtypeStruct((b, s, 2 * IDX_DIM), BF16),
        jax.ShapeDtypeStruct((b, s // kc, ATTN_WIDTH, kc), BF16),
        jax.ShapeDtypeStruct((b, IDX_HEADS, s), F32),
        jax.ShapeDtypeStruct((b, s, POOL_WIDTH), BF16),
    )
    out_specs = (
        tok(ATTN_WIDTH), tok(ATTN_WIDTH), tok(IDX_HEADS * IDX_DIM), tok(2 * IDX_DIM),
        pl.BlockSpec((1, tm // kc, ATTN_WIDTH, kc), lambda bi, ti: (bi, ti, 0, 0)),
        pl.BlockSpec((1, IDX_HEADS, tm), lambda bi, ti: (bi, 0, ti)),
        tok(POOL_WIDTH),
    )
    return pl.pallas_call(
        functools.partial(_inproj_kernel, tm=tm, kc=kc),
        grid=(b, nt),
        in_specs=[tok(d), per_b, per_b, full((1, d)), full((d, 2048)), full((d, 2 * IDX_DIM)),
                  full((ATTN_WIDTH, d)), full((16, d)), full((ATTN_WIDTH, ATTN_WIDTH)),
                  full((1, ATTN_WIDTH)), full((1, ATTN_WIDTH)),
                  full((len(POOL_WINDOWS), POOL_GROUP, POOL_GROUP)),
                  full((1, POOL_WIDTH)), full((1, POOL_WIDTH))],
        out_specs=out_specs,
        out_shape=out_shapes,
        scratch_shapes=[pltpu.VMEM((tm + POOL_HALO, POOL_WIDTH), F32)],
        compiler_params=pltpu.CompilerParams(
            dimension_semantics=("arbitrary", "arbitrary"), vmem_limit_bytes=48 * 1024 * 1024),
        name="inproj",
    )(x, sh1, sc1, norm_g.reshape(1, d), wnat, wki, wvt, wwt, ones_bd, qg, kg,
      pool_w.astype(BF16), pool_scale.reshape(1, -1), pool_out_g.reshape(1, -1))


def _attn_kernel(q_ref, qi_ref, wt_ref, kid_ref, k_ref, kf_ref, vt_ref, aog_ref, o_ref,
                 key_ref, acc_ref, m_ref, l_ref, qim_ref, qmf_ref, cut_ref, lg_ref, hi_ref,
                 *, tq, ck, seq, topk):
    j = pl.program_id(1)
    nck = ((j + 1) * tq + ck - 1) // ck
    qpos = j * tq + lax.broadcasted_iota(I32, (1, tq), 1)
    qchunk = qpos // CHUNK
    row_i = lax.broadcasted_iota(I32, (ck, tq), 0)

    lane = lax.broadcasted_iota(I32, (tq, LANES), 1)
    tcol = (j * tq + lax.broadcasted_iota(I32, (tq, LANES), 0)).astype(F32)
    for h in range(N_HEADS):
        p = h // 2
        keep = (lane < HEAD_DIM) if h % 2 == 0 else (lane >= HEAD_DIM)
        qi_slab = qi_ref[0, :, p * LANES:(p + 1) * LANES]
        q_slab = q_ref[0, :, p * LANES:(p + 1) * LANES]
        qim_ref[h * tq:(h + 1) * tq, :] = jnp.where(keep, qi_slab, jnp.zeros_like(qi_slab))
        qmf_ref[h, :, 0:LANES] = jnp.where(keep, q_slab, jnp.zeros_like(q_slab))
        c1, c2, c3 = _bf16_split3(_SLOPE_L2E[h])
        u = -_SLOPE_L2E[h] * tcol
        u1 = u.astype(BF16).astype(F32)
        r = u - u1
        u2 = r.astype(BF16).astype(F32)
        u3 = (r - u2).astype(BF16).astype(F32)
        feat = jnp.zeros((tq, LANES), F32)
        for li, val in enumerate((64.0 * c1, 64.0 * c2, 64.0 * c3, c1, c2, c3, u1, u2, u3)):
            feat = jnp.where(lane == li, val, feat)
        qmf_ref[h, :, LANES:2 * LANES] = feat.astype(BF16)

    def scores(c, last):
        r0 = pl.multiple_of(c * ck, ck)
        d = _dot_nt(kid_ref[0, pl.ds(r0, ck), :], qim_ref[...])
        acc = jnp.zeros((ck, tq), F32)
        for h in range(IDX_HEADS):
            acc = acc + wt_ref[0, h:h + 1, :] * jnp.maximum(d[:, h * tq:(h + 1) * tq], 0.0)
        if last:
            acc = jnp.where(((r0 + row_i) // CHUNK) <= qchunk, acc, NEG)
        bits = pltpu.bitcast(acc, I32)
        key_ref[pl.ds(r0, ck), :] = jnp.where(bits < 0, bits ^ 0x7FFFFFFF, bits)
        hi_ref[pl.ds(r0, ck), :] = pltpu.bitcast(bits & -65536, F32).astype(BF16)

    def scores_body(c, carry):
        scores(c, False)
        return carry

    lax.fori_loop(0, nck - 1, scores_body, 0)
    scores(nck - 1, True)

    def sum_chunks(chunk_fn, init):
        def body(i, acc):
            r0 = pl.multiple_of(i * (2 * ck), 2 * ck)
            return acc + chunk_fn(r0) + chunk_fn(pl.multiple_of(r0 + ck, ck))
        acc = lax.fori_loop(0, nck // 2, body, init)
        return lax.cond(nck % 2 == 1,
                        lambda a: a + chunk_fn(pl.multiple_of((nck - 1) * ck, ck)),
                        lambda a: a, acc)

    def count(pred):
        def chunk(r0):
            m = pred(key_ref[pl.ds(r0, ck), :], r0).astype(I32)
            return m.reshape(ck // 8, 8, tq).sum(axis=0)
        return sum_chunks(chunk, jnp.zeros((8, tq), I32)).sum(axis=0, keepdims=True)

    def count_hi(trial16):
        pat = jnp.where(trial16 < 0, trial16 ^ 0x7FFF, trial16)
        tb = pltpu.bitcast(jnp.left_shift(pat, 16), F32).astype(BF16)

        def chunk(r0):
            m = hi_ref[pl.ds(r0, ck), :] >= tb
            ones = jnp.where(m, jnp.ones((ck, tq), BF16), jnp.zeros((ck, tq), BF16))
            parts = [ones[i * 16:(i + 1) * 16, :] for i in range(ck // 16)]
            while len(parts) > 1:
                parts = [parts[i] + parts[i + 1] for i in range(0, len(parts), 2)]
            return parts[0].astype(F32)

        cacc = sum_chunks(chunk, jnp.zeros((16, tq), F32))
        return cacc.sum(axis=0, keepdims=True).astype(I32)

    c_nonneg = count_hi(jnp.zeros((1, tq), I32))
    ok0 = c_nonneg >= topk
    state0 = (jnp.where(ok0, 0, -(2 ** 15)).astype(I32),
              jnp.where(ok0, c_nonneg, nck * ck), jnp.where(ok0, 0, c_nonneg))

    def hi_body(b, state):
        lo16, c_ge, c_gt = state
        trial = lo16 + jnp.left_shift(jnp.int32(1), 14 - b)
        c = count_hi(trial)
        ok = c >= topk
        return jnp.where(ok, trial, lo16), jnp.where(ok, c, c_ge), jnp.where(ok, c_gt, c)

    lo16, c_ge, c_gt = lax.fori_loop(0, 15, hi_body, state0)

    def bit_body(b, state):
        lo, c_ge, c_gt = state
        trial = lo + jnp.left_shift(jnp.int32(1), 15 - b)
        c = count(lambda kk, r0: kk >= trial)
        ok = c >= topk
        return jnp.where(ok, trial, lo), jnp.where(ok, c, c_ge), jnp.where(ok, c_gt, c)

    lo, c_ge, c_gt = lax.fori_loop(0, 16, bit_body, (jnp.left_shift(lo16, 16), c_ge, c_gt))

    need = topk - c_gt
    cut_ref[...] = jnp.full((1, tq), seq, I32)

    @pl.when(jnp.max(c_ge) > topk)
    def _():
        nbits = max(1, (seq - 1).bit_length())

        def idx_body(b, jv):
            trial = jv + jnp.left_shift(jnp.int32(1), nbits - 1 - b)
            c = count(lambda kk, r0: (kk == lo)
                      & ((r0 + lax.broadcasted_iota(I32, (ck, tq), 0)) < trial))
            return jnp.where(c < need, trial, jv)

        jv = lax.fori_loop(0, nbits, idx_body, jnp.zeros((1, tq), I32))
        cut_ref[...] = jv + 1

    cut = cut_ref[...]

    m_ref[...] = jnp.full((N_HEADS, 1, tq), M_INIT, F32)
    l_ref[...] = jnp.zeros((N_HEADS, 1, tq), F32)
    acc_ref[...] = jnp.zeros((ATTN_WIDTH, tq), F32)

    def attend(c, last):
        r0 = pl.multiple_of(c * ck, ck)
        kk = key_ref[pl.ds(r0, ck), :]
        kf = kf_ref[pl.ds(r0, ck), :]
        kch = k_ref[0, pl.ds(r0, ck), :]
        vtc = vt_ref[0, c]
        kpos = r0 + row_i
        sel = (kk > lo) | ((kk == lo) & (kpos < cut))
        if last:
            sel = sel & ((kpos // CHUNK) <= qchunk)
            ahead = jnp.maximum(kpos - qpos, 0).astype(F32)
        pen = jnp.where(sel, 0.0, NEG)
        for h in range(N_HEADS):
            p = h // 2
            ks = kch[:, p * LANES:(p + 1) * LANES]
            lg = _dot_nt(jnp.concatenate([ks, kf], axis=1), qmf_ref[h]) + pen
            if last:
                lg = lg - (2.0 * _SLOPE_L2E[h]) * ahead
            lg_ref[h] = lg
        for h in range(N_HEADS):
            m_old = m_ref[h]
            m_new = jnp.maximum(m_old, jnp.max(lg_ref[h], axis=0, keepdims=True))
            alpha = jnp.exp2(m_old - m_new)
            pr = jnp.exp2(lg_ref[h] - m_new)
            l_ref[h] = alpha * l_ref[h] + jnp.sum(pr, axis=0, keepdims=True)
            m_ref[h] = m_new
            rows = slice(h * HEAD_DIM, (h + 1) * HEAD_DIM)
            acc_ref[rows, :] = alpha * acc_ref[rows, :] + _dot(vtc[rows, :], pr.astype(BF16))

    def attend_body(c, carry):
        attend(c, False)
        return carry

    lax.fori_loop(0, nck - 1, attend_body, 0)
    attend(nck - 1, True)

    outs = []
    for h in range(N_HEADS):
        rows = slice(h * HEAD_DIM, (h + 1) * HEAD_DIM)
        o = acc_ref[rows, :] / l_ref[h]
        oms = jnp.mean(o * o, axis=0, keepdims=True)
        outs.append(o * lax.rsqrt(oms + EPS))
    ot = jnp.concatenate(outs, axis=0)
    o_ref[0] = (ot.T * aog_ref[...]).astype(BF16)


def _attn(q, k, qi, kid, vt, wt, attn_out_g, tq, ck):
    b, s, _ = q.shape
    topk = min(TOPK_MAX, s // 4)
    assert tq % LANES == 0 and ck % tq == 0 and s % ck == 0 and s // CHUNK <= 256
    pos = jnp.arange(s)
    feats = [pos // CHUNK] * 3 + [pos % CHUNK] * 3 + [jnp.ones_like(pos)] * 3
    kf = jnp.zeros((s, LANES), F32).at[:, 0:N_FEAT].set(
        jnp.stack(feats, axis=1).astype(F32)).astype(BF16)
    blk = lambda w: pl.BlockSpec((1, tq, w), lambda bi, ji: (bi, ji, 0))
    once = pl.Buffered(1)
    whole = lambda w: pl.BlockSpec((1, s, w), lambda bi, ji: (bi, 0, 0), pipeline_mode=once)
    return pl.pallas_call(
        functools.partial(_attn_kernel, tq=tq, ck=ck, seq=s, topk=topk),
        grid=(b, s // tq),
        in_specs=[blk(ATTN_WIDTH), blk(IDX_HEADS * IDX_DIM),
                  pl.BlockSpec((1, IDX_HEADS, tq), lambda bi, ji: (bi, 0, ji)),
                  whole(2 * IDX_DIM), whole(ATTN_WIDTH),
                  pl.BlockSpec((s, LANES), lambda bi, ji: (0, 0), pipeline_mode=once),
                  pl.BlockSpec((1, s // ck, ATTN_WIDTH, ck), lambda bi, ji: (bi, 0, 0, 0),
                               pipeline_mode=once),
                  pl.BlockSpec((1, ATTN_WIDTH), lambda bi, ji: (0, 0))],
        out_specs=blk(ATTN_WIDTH),
        out_shape=jax.ShapeDtypeStruct((b, s, ATTN_WIDTH), BF16),
        scratch_shapes=[pltpu.VMEM((s, tq), I32),
                        pltpu.VMEM((ATTN_WIDTH, tq), F32),
                        pltpu.VMEM((N_HEADS, 1, tq), F32),
                        pltpu.VMEM((N_HEADS, 1, tq), F32),
                        pltpu.VMEM((IDX_HEADS * tq, LANES), BF16),
                        pltpu.VMEM((N_HEADS, tq, 2 * LANES), BF16),
                        pltpu.VMEM((1, tq), I32),
                        pltpu.VMEM((N_HEADS, ck, tq), F32),
                        pltpu.VMEM((s, tq), BF16)],
        compiler_params=pltpu.CompilerParams(
            dimension_semantics=("arbitrary", "arbitrary"), vmem_limit_bytes=56 * 1024 * 1024),
        name="attn",
    )(q, qi, wt, kid, k, kf, vt, attn_out_g.reshape(1, ATTN_WIDTH))


def _outproj_kernel(pool_ref, attn_ref, x_ref, wo_ref, g1_ref, sh_ref, sc_ref, ng_ref,
                    x1_ref, h2t_ref):
    mixed = _dot(pool_ref[0], wo_ref[0:POOL_WIDTH, :]) + _dot(attn_ref[0], wo_ref[POOL_WIDTH:, :])
    x1 = x_ref[0] + g1_ref[0] * mixed
    x1_ref[0] = x1
    ms = jnp.mean(x1 * x1, axis=-1, keepdims=True)
    h2 = (x1 * lax.rsqrt(ms + EPS) * ng_ref[...]) * (1.0 + sc_ref[0]) + sh_ref[0]
    h2t_ref[...] = h2.T.astype(BF16)


def _outproj(pool_out, attn_out, x, w_out, g1, sh2, sc2, norm_g, tm):
    b, s, d = x.shape
    nt = s // tm
    tok = lambda w: pl.BlockSpec((1, tm, w), lambda bi, ti: (bi, ti, 0))
    per_b = pl.BlockSpec((1, 1, d), lambda bi, ti: (bi, 0, 0))
    return pl.pallas_call(
        _outproj_kernel,
        grid=(b, nt),
        in_specs=[tok(POOL_WIDTH), tok(ATTN_WIDTH), tok(d),
                  pl.BlockSpec((POOL_WIDTH + ATTN_WIDTH, d), lambda bi, ti: (0, 0)),
                  per_b, per_b, per_b, pl.BlockSpec((1, d), lambda bi, ti: (0, 0))],
        out_specs=(tok(d), pl.BlockSpec((d, tm), lambda bi, ti: (0, bi * nt + ti))),
        out_shape=(jax.ShapeDtypeStruct((b, s, d), F32), jax.ShapeDtypeStruct((d, b * s), BF16)),
        compiler_params=pltpu.CompilerParams(dimension_semantics=("arbitrary", "arbitrary")),
        name="outproj",
    )(pool_out, attn_out, x, w_out.astype(BF16), g1, sh2, sc2, norm_g.reshape(1, d))


def _batcher_pairs(n):
    pairs = []

    def merge(lo, cnt, r):
        step = r * 2
        if step < cnt:
            merge(lo, cnt, step)
            merge(lo + r, cnt, step)
            for i in range(lo + r, lo + cnt - r, step):
                pairs.append((i, i + r))
        else:
            pairs.append((lo, lo + r))

    def sort(lo, cnt):
        if cnt > 1:
            m = cnt // 2
            sort(lo, m)
            sort(lo + m, m)
            merge(lo, cnt, 1)

    sort(0, n)
    return tuple(pairs)


_SORT16 = _batcher_pairs(PEER_TOPK)
MM_GROUP = 8
MM_COLS = 256


def _top16_desc(x):
    x = list(x)
    for i, j in _SORT16:
        x[i], x[j] = jnp.maximum(x[i], x[j]), jnp.minimum(x[i], x[j])
    for shift in (4, 2, 1):
        y = [jnp.maximum(x[k], pltpu.roll(x[PEER_TOPK - 1 - k], shift, 0)) for k in range(PEER_TOPK)]
        for d in (8, 4, 2, 1):
            for k in range(PEER_TOPK):
                if k & d == 0:
                    y[k], y[k + d] = jnp.maximum(y[k], y[k + d]), jnp.minimum(y[k], y[k + d])
        x = y
    return x


def _peer_kernel(h2t_ref, x1_ref, g2_ref, wqt_ref, sk_ref, u_ref, vt_ref, o_ref,
                 s1_ref, s2_ref, thr_ref, p_ref, acc_ref, s1b_ref, at_ref, *, tm, eb, ne):
    e = pl.program_id(1)
    nsub = eb // PEER_KEYS
    ntl = tm // LANES

    @pl.when(e == 0)
    def _prologue():
        h2t = h2t_ref[...]
        sub = lax.broadcasted_iota(I32, (8, tm), 0)
        ninf = jnp.full((8, tm), -jnp.inf, F32)

        def compact(t, base):
            out = t[base + 7]
            for r in range(6, -1, -1):
                out = jnp.where(sub == r, t[base + r], out)
            return out

        def split8(st):
            return [st[k * 8:(k + 1) * 8, :] for k in range(PEER_KEYS // 8)]

        def head(hd, carry):
            q1 = _dot(wqt_ref[2 * hd], h2t).astype(BF16)
            q2 = _dot(wqt_ref[2 * hd + 1], h2t).astype(BF16)
            st1 = _dot(sk_ref[2 * hd], q1) * LOG2E
            st2 = _dot(sk_ref[2 * hd + 1], q2) * LOG2E
            for tl in range(ntl):
                s2_ref[hd, tl] = st2[:, tl * LANES:(tl + 1) * LANES]
            t1 = _top16_desc(split8(st1))
            t2 = _top16_desc(split8(st2))
            a2lo, a2hi = compact(t2, 0), compact(t2, 8)

            def candidates(t1x):
                cands = [t1x[0] + a2lo, t1x[0] + a2hi, t1x[1] + a2lo]
                for r1 in range(2, 8):
                    cands.append(jnp.where(sub < PEER_TOPK // (r1 + 1), t1x[r1] + a2lo, ninf))
                cands.append(compact(t1x, 8) + t2[0])
                return cands

            def kth(cands):
                return _top16_desc(cands + [ninf] * (PEER_TOPK - len(cands)))[PEER_TOPK - 1]

            cands = candidates(t1)
            thr = kth(cands)
            m = t1[0] + t2[0]
            z8 = jnp.zeros((8, tm), F32)
            for c in cands:
                z8 = z8 + jnp.where(c >= thr, jnp.exp2(c - m), 0.0)
            mz = m + jnp.log2(jnp.broadcast_to(jnp.sum(z8, axis=0, keepdims=True), (8, tm))) + 1.0
            s1_ref[hd] = st1 - mz[0:1, :]
            thr_ref[pl.ds(hd, 1), :] = kth(candidates([t - mz for t in t1]))[0:1, :]
            return carry

        lax.fori_loop(0, PEER_HEADS, head, 0)
        acc_ref[...] = jnp.zeros_like(acc_ref)

    i0 = pl.multiple_of(e * nsub, nsub)
    s1rows = [s1_ref[hd, pl.ds(i0, nsub), :] for hd in range(PEER_HEADS)]
    for hd in range(PEER_HEADS):
        s1b_ref[hd] = s1rows[hd]
    gsz = MM_GROUP * PEER_KEYS
    ngrp = nsub // MM_GROUP
    grp_rows = [slice(g * gsz, (g + 1) * gsz) for g in range(ngrp)]
    at_ref[0] = _dot(u_ref[grp_rows[0], :], h2t_ref[...])
    for g in range(ngrp):
        grows = grp_rows[g]
        if g + 1 < ngrp:
            at_ref[g + 1] = _dot(u_ref[grp_rows[g + 1], :], h2t_ref[...])
        if g >= 1:
            acc_ref[...] += _dot(vt_ref[:, grp_rows[g - 1]], p_ref[grp_rows[g - 1], :])
        for sg in range(MM_GROUP):
            ii = g * MM_GROUP + sg
            rows = slice(ii * PEER_KEYS, (ii + 1) * PEER_KEYS)
            for tl in range(ntl):
                cols = slice(tl * LANES, (tl + 1) * LANES)
                w = jnp.zeros((PEER_KEYS, LANES), F32)
                for hd in range(PEER_HEADS):
                    sm = s1b_ref[hd, ii:ii + 1, cols] + s2_ref[hd, tl]
                    w = w + jnp.where(sm >= thr_ref[hd:hd + 1, cols], jnp.exp2(sm), 0.0)
                a = at_ref[g, sg * PEER_KEYS:(sg + 1) * PEER_KEYS, cols]
                gl = a * (1.0 + lax.erf(a * (1.0 / math.sqrt(2.0))))
                p_ref[rows, cols] = (gl * w).astype(BF16)
    acc_ref[...] += _dot(vt_ref[:, grp_rows[-1]], p_ref[grp_rows[-1], :])

    @pl.when(e == ne - 1)
    def _epilogue():
        o_ref[...] = x1_ref[...] + g2_ref[0] * acc_ref[...].T


def _peer(h2t, x1, g2, wq, sub_keys, u_tab, v_tab, seq, tm, eb):
    d, t = h2t.shape
    ne = u_tab.shape[0] // eb
    half = PEER_KEYS
    wqt = wq.T.reshape(2 * PEER_HEADS, half, d).astype(BF16)
    sk = sub_keys.reshape(2 * PEER_HEADS, PEER_KEYS, half).astype(BF16)
    ub = u_tab.astype(BF16)
    vtb = v_tab.T.astype(BF16)
    hs = (PEER_HEADS, PEER_KEYS, tm)
    return pl.pallas_call(
        functools.partial(_peer_kernel, tm=tm, eb=eb, ne=ne),
        grid=(t // tm, ne),
        in_specs=[pl.BlockSpec((d, tm), lambda i, e: (0, i)),
                  pl.BlockSpec((tm, d), lambda i, e: (i, 0)),
                  pl.BlockSpec((1, 1, d), lambda i, e: ((i * tm) // seq, 0, 0)),
                  pl.BlockSpec((2 * PEER_HEADS, half, d), lambda i, e: (0, 0, 0)),
                  pl.BlockSpec((2 * PEER_HEADS, PEER_KEYS, half), lambda i, e: (0, 0, 0)),
                  pl.BlockSpec((eb, d), lambda i, e: (e, 0)),
                  pl.BlockSpec((d, eb), lambda i, e: (0, e))],
        out_specs=pl.BlockSpec((tm, d), lambda i, e: (i, 0)),
        out_shape=jax.ShapeDtypeStruct((t, d), F32),
        scratch_shapes=[pltpu.VMEM(hs, F32),
                        pltpu.VMEM((PEER_HEADS, tm // LANES, PEER_KEYS, LANES), F32),
                        pltpu.VMEM((PEER_HEADS, tm), F32),
                        pltpu.VMEM((eb, tm), BF16),
                        pltpu.VMEM((d, tm), F32),
                        pltpu.VMEM((PEER_HEADS, eb // PEER_KEYS, tm), F32),
                        pltpu.VMEM((eb // (MM_GROUP * PEER_KEYS), MM_GROUP * PEER_KEYS, tm), F32)],
        compiler_params=pltpu.CompilerParams(
            dimension_semantics=("arbitrary", "arbitrary"), vmem_limit_bytes=58 * 1024 * 1024),
        name="peer",
    )(h2t, x1, g2, wqt, sk, ub, vtb)


def _tile(n, pref):
    t = min(pref, n)
    assert n % t == 0
    return t


def kernel(x, c, ada_w, ada_b, norm1_g, w_in, pool_w, pool_scale, pool_out_g, q_norm_g, k_norm_g,
           attn_out_g, w_out, norm2_g, peer_wq, peer_subkeys, peer_u, peer_v):
    b, s, d = x.shape
    depth = ada_w.shape[0]
    tm = _tile(s, 512)
    tq = _tile(s, 2 * LANES)
    ck = _tile(tm, 512)
    tp = _tile(s, 512)
    eb = 2048
    for l in range(depth):
        mod = _ada(c, ada_w[l], ada_b[l]).reshape(b, 1, 6 * d)
        sh1, sc1, g1, sh2, sc2, g2 = jnp.split(mod, 6, axis=-1)
        q, k, qi, kid, vt, wt, pool_out = _inproj(
            x, sh1, sc1, norm1_g[l], w_in[l], pool_w[l], pool_scale[l], pool_out_g[l],
            q_norm_g[l], k_norm_g[l], tm, ck)
        attn_out = _attn(q, k, qi, kid, vt, wt, attn_out_g[l], tq, ck)
        x1, h2t = _outproj(pool_out, attn_out, x, w_out[l], g1, sh2, sc2, norm2_g[l], tm)
        x = _peer(h2t, x1.reshape(b * s, d), g2, peer_wq[l], peer_subkeys[l],
                  peer_u[l], peer_v[l], s, tp, eb).reshape(b, s, d)
    return x
```

```python
import functools
import math

import ml_dtypes
import numpy as np

import jax
import jax.numpy as jnp
from jax import lax
from jax.experimental import pallas as pl
from jax.experimental.pallas import tpu as pltpu

F32 = jnp.float32
BF16 = jnp.bfloat16
I32 = jnp.int32

CHUNK = 64
EPS = 1e-6
POOL_WINDOWS = (2, 4, 8, 16)
POOL_GROUP = 128
POOL_WIDTH = 512
ATTN_WIDTH = 512
HEAD_DIM = 64
N_HEADS = 8
IDX_HEADS = 8
IDX_DIM = 64
TOPK_MAX = 256
PEER_HEADS = 8
PEER_KEYS = 128
PEER_TOPK = 16
NEG = -1e30

LANES = 128
V7X_VMEM_BYTES = 64 * 1024 * 1024
VMEM_LIMIT_BYTES = (3 * V7X_VMEM_BYTES) // 4
POOL_HALO = 16
LOG2E = math.log2(math.e)
M_INIT = 0.5 * NEG
N_FEAT = 9

NT_DIMS = (((1,), (1,)), ((), ()))


def _dot(a, b):
    return jnp.dot(a, b, preferred_element_type=F32)


def _dot_nt(a, b):
    return lax.dot_general(a, b, NT_DIMS, preferred_element_type=F32)


def _bf16_split3(c):
    c = np.float32(c)
    p1 = np.float32(c.astype(ml_dtypes.bfloat16))
    r = np.float32(c - p1)
    p2 = np.float32(r.astype(ml_dtypes.bfloat16))
    p3 = np.float32(np.float32(r - p2).astype(ml_dtypes.bfloat16))
    return float(p1), float(p2), float(p3)


_SLOPE_L2E = tuple(float(np.float32(2.0 ** (-8.0 * (h + 1) / N_HEADS) * LOG2E)) for h in range(N_HEADS))


def _ada_kernel(c_ref, w_ref, b_ref, o_ref):
    c = c_ref[...]
    cond = c * jax.nn.sigmoid(c)
    o_ref[...] = jnp.dot(cond, w_ref[...], preferred_element_type=F32,
                         precision=lax.Precision.HIGHEST) + b_ref[...]


def _ada(c, w, b):
    bsz, d = c.shape
    n = w.shape[1]
    tn = 1024
    return pl.pallas_call(
        _ada_kernel,
        grid=(n // tn,),
        in_specs=[pl.BlockSpec((bsz, d), lambda j: (0, 0)),
                  pl.BlockSpec((d, tn), lambda j: (0, j)),
                  pl.BlockSpec((1, tn), lambda j: (0, j))],
        out_specs=pl.BlockSpec((bsz, tn), lambda j: (0, j)),
        out_shape=jax.ShapeDtypeStruct((bsz, n), F32),
        name="ada",
    )(c, w, b.reshape(1, n))


def _group_sumsq(t, ones_bd):
    sq = t * t
    hi = sq.astype(BF16)
    lo = (sq - hi.astype(F32)).astype(BF16)
    return _dot(hi, ones_bd) + _dot(lo, ones_bd)


def _inproj_kernel(x_ref, sh_ref, sc_ref, g_ref, wnat_ref, wki_ref, wvt_ref, wwt_ref, ones_ref,
                   qg_ref, kg_ref, poolw_ref, pscale_ref, pg_ref,
                   q_ref, k_ref, qi_ref, kid_ref, vt_ref, wt_ref, pool_ref,
                   ubuf, *, tm, kc):
    i = pl.program_id(1)
    x = x_ref[0]
    ms = jnp.mean(x * x, axis=-1, keepdims=True)
    h = (x * lax.rsqrt(ms + EPS) * g_ref[...]) * (1.0 + sc_ref[0]) + sh_ref[0]
    hb = h.astype(BF16)

    proj = _dot(hb, wnat_ref[...])
    u = proj[:, 0:512]
    q = proj[:, 512:1024]
    k = proj[:, 1024:1536]
    qi = proj[:, 1536:2048]

    ones_bd = ones_ref[...]
    qn = q * lax.rsqrt(_group_sumsq(q, ones_bd) * (1.0 / HEAD_DIM) + EPS) * qg_ref[...]
    kn = k * lax.rsqrt(_group_sumsq(k, ones_bd) * (1.0 / HEAD_DIM) + EPS) * kg_ref[...]
    q_ref[0] = qn.astype(BF16)
    k_ref[0] = kn.astype(BF16)
    qi_ref[0] = (qi * (IDX_DIM ** -0.5)).astype(BF16)
    kid_ref[0] = _dot(hb, wki_ref[...]).astype(BF16)
    vt = _dot_nt(wvt_ref[...], hb).astype(BF16)
    for ci in range(tm // kc):
        vt_ref[0, ci] = vt[:, ci * kc:(ci + 1) * kc]
    wt = _dot_nt(wwt_ref[...], hb)
    wt_ref[0] = wt[0:IDX_HEADS, :] * (IDX_HEADS ** -0.5)

    @pl.when(i == 0)
    def _():
        ubuf[0:POOL_HALO, :] = jnp.zeros((POOL_HALO, POOL_WIDTH), F32)

    ubuf[POOL_HALO:POOL_HALO + tm, :] = u
    tpos = i * tm + lax.broadcasted_iota(I32, (tm, POOL_GROUP), 0)
    parts = []
    for gi, win in enumerate(POOL_WINDOWS):
        lo_l, hi_l = gi * POOL_GROUP, (gi + 1) * POOL_GROUP
        ws = ubuf[POOL_HALO:POOL_HALO + tm, lo_l:hi_l]
        for j in range(1, win):
            ws = ws + ubuf[POOL_HALO - j:POOL_HALO - j + tm, lo_l:hi_l]
        cnt = jnp.minimum(tpos + 1, win).astype(F32)
        pooled = ws / cnt - u[:, lo_l:hi_l]
        parts.append(_dot(pooled.astype(BF16), poolw_ref[gi]))
    mixed = jnp.concatenate(parts, axis=-1) * pscale_ref[...]
    pms = jnp.mean(mixed * mixed, axis=-1, keepdims=True)
    pool_ref[0] = (mixed * lax.rsqrt(pms + EPS) * pg_ref[...]).astype(BF16)
    ubuf[0:POOL_HALO, :] = ubuf[tm:tm + POOL_HALO, :]


def _inproj(x, sh1, sc1, norm_g, w_in, pool_w, pool_scale, pool_out_g, q_norm_g, k_norm_g, tm, kc):
    b, s, d = x.shape
    nt = s // tm
    w_u, w_q, w_k, w_v, w_qi, w_ki, w_wi = jnp.split(
        w_in, (512, 1024, 1536, 2048, 2560, 2624), axis=-1)
    wnat = jnp.concatenate([w_u, w_q, w_k, w_qi], axis=-1).astype(BF16)
    wki = jnp.concatenate([w_ki, w_ki], axis=-1).astype(BF16)
    wvt = w_v.T.astype(BF16)
    wwt = jnp.concatenate([w_wi.T, jnp.zeros((16 - IDX_HEADS, d), F32)], axis=0).astype(BF16)
    gid = jnp.arange(ATTN_WIDTH) // HEAD_DIM
    ones_bd = (gid[:, None] == gid[None, :]).astype(BF16)
    qg = (jnp.tile(q_norm_g, N_HEADS) * (HEAD_DIM ** -0.5 * LOG2E)).reshape(1, ATTN_WIDTH)
    kg = jnp.tile(k_norm_g, N_HEADS).reshape(1, ATTN_WIDTH)

    full = lambda shape: pl.BlockSpec(shape, lambda bi, ti: (0,) * len(shape))
    tok = lambda w: pl.BlockSpec((1, tm, w), lambda bi, ti: (bi, ti, 0))
    per_b = pl.BlockSpec((1, 1, d), lambda bi, ti: (bi, 0, 0))
    out_shapes = (
        jax.ShapeDtypeStruct((b, s, ATTN_WIDTH), BF16),
        jax.ShapeDtypeStruct((b, s, ATTN_WIDTH), BF16),
        jax.ShapeDtypeStruct((b, s, IDX_HEADS * IDX_DIM), BF16),
        jax.ShapeDtypeStruct((b, s, 2 * IDX_DIM), BF16),
        jax.ShapeDtypeStruct((b, s // kc, ATTN_WIDTH, kc), BF16),
        jax.ShapeDtypeStruct((b, IDX_HEADS, s), F32),
        jax.ShapeDtypeStruct((b, s, POOL_WIDTH), BF16),
    )
    out_specs = (
        tok(ATTN_WIDTH), tok(ATTN_WIDTH), tok(IDX_HEADS * IDX_DIM), tok(2 * IDX_DIM),
        pl.BlockSpec((1, tm // kc, ATTN_WIDTH, kc), lambda bi, ti: (bi, ti, 0, 0)),
        pl.BlockSpec((1, IDX_HEADS, tm), lambda bi, ti: (bi, 0, ti)),
        tok(POOL_WIDTH),
    )
    return pl.pallas_call(
        functools.partial(_inproj_kernel, tm=tm, kc=kc),
        grid=(b, nt),
        in_specs=[tok(d), per_b, per_b, full((1, d)), full((d, 2048)), full((d, 2 * IDX_DIM)),
                  full((ATTN_WIDTH, d)), full((16, d)), full((ATTN_WIDTH, ATTN_WIDTH)),
                  full((1, ATTN_WIDTH)), full((1, ATTN_WIDTH)),
                  full((len(POOL_WINDOWS), POOL_GROUP, POOL_GROUP)),
                  full((1, POOL_WIDTH)), full((1, POOL_WIDTH))],
        out_specs=out_specs,
        out_shape=out_shapes,
        scratch_shapes=[pltpu.VMEM((tm + POOL_HALO, POOL_WIDTH), F32)],
        compiler_params=pltpu.CompilerParams(
            dimension_semantics=("arbitrary", "arbitrary"), vmem_limit_bytes=VMEM_LIMIT_BYTES),
        name="inproj",
    )(x, sh1, sc1, norm_g.reshape(1, d), wnat, wki, wvt, wwt, ones_bd, qg, kg,
      pool_w.astype(BF16), pool_scale.reshape(1, -1), pool_out_g.reshape(1, -1))


def _attn_kernel(q_ref, qi_ref, wt_ref, kid_ref, k_ref, kf_ref, vt_ref, aog_ref, o_ref,
                 key_ref, acc_ref, m_ref, l_ref, qim_ref, qmf_ref, cut_ref, lg_ref, hi_ref,
                 *, tq, ck, seq, topk):
    j = pl.program_id(1)
    nck = ((j + 1) * tq + ck - 1) // ck
    qpos = j * tq + lax.broadcasted_iota(I32, (1, tq), 1)
    qchunk = qpos // CHUNK
    row_i = lax.broadcasted_iota(I32, (ck, tq), 0)

    lane = lax.broadcasted_iota(I32, (tq, LANES), 1)
    tcol = (j * tq + lax.broadcasted_iota(I32, (tq, LANES), 0)).astype(F32)
    for h in range(N_HEADS):
        p = h // 2
        keep = (lane < HEAD_DIM) if h % 2 == 0 else (lane >= HEAD_DIM)
        qi_slab = qi_ref[0, :, p * LANES:(p + 1) * LANES]
        q_slab = q_ref[0, :, p * LANES:(p + 1) * LANES]
        qim_ref[h * tq:(h + 1) * tq, :] = jnp.where(keep, qi_slab, jnp.zeros_like(qi_slab))
        qmf_ref[h, :, 0:LANES] = jnp.where(keep, q_slab, jnp.zeros_like(q_slab))
        c1, c2, c3 = _bf16_split3(_SLOPE_L2E[h])
        u = -_SLOPE_L2E[h] * tcol
        u1 = u.astype(BF16).astype(F32)
        r = u - u1
        u2 = r.astype(BF16).astype(F32)
        u3 = (r - u2).astype(BF16).astype(F32)
        feat = jnp.zeros((tq, LANES), F32)
        for li, val in enumerate((64.0 * c1, 64.0 * c2, 64.0 * c3, c1, c2, c3, u1, u2, u3)):
            feat = jnp.where(lane == li, val, feat)
        qmf_ref[h, :, LANES:2 * LANES] = feat.astype(BF16)

    def scores(c, last):
        r0 = pl.multiple_of(c * ck, ck)
        d = _dot_nt(kid_ref[0, pl.ds(r0, ck), :], qim_ref[...])
        acc = jnp.zeros((ck, tq), F32)
        for h in range(IDX_HEADS):
            acc = acc + wt_ref[0, h:h + 1, :] * jnp.maximum(d[:, h * tq:(h + 1) * tq], 0.0)
        if last:
            acc = jnp.where(((r0 + row_i) // CHUNK) <= qchunk, acc, NEG)
        bits = pltpu.bitcast(acc, I32)
        key_ref[pl.ds(r0, ck), :] = jnp.where(bits < 0, bits ^ 0x7FFFFFFF, bits)
        hi_ref[pl.ds(r0, ck), :] = pltpu.bitcast(bits & -65536, F32).astype(BF16)

    def scores_body(c, carry):
        scores(c, False)
        return carry

    lax.fori_loop(0, nck - 1, scores_body, 0)
    scores(nck - 1, True)

    def sum_chunks(chunk_fn, init):
        def body(i, acc):
            r0 = pl.multiple_of(i * (2 * ck), 2 * ck)
            return acc + chunk_fn(r0) + chunk_fn(pl.multiple_of(r0 + ck, ck))
        acc = lax.fori_loop(0, nck // 2, body, init)
        return lax.cond(nck % 2 == 1,
                        lambda a: a + chunk_fn(pl.multiple_of((nck - 1) * ck, ck)),
                        lambda a: a, acc)

    def count(pred):
        def chunk(r0):
            m = pred(key_ref[pl.ds(r0, ck), :], r0).astype(I32)
            return m.reshape(ck // 8, 8, tq).sum(axis=0)
        return sum_chunks(chunk, jnp.zeros((8, tq), I32)).sum(axis=0, keepdims=True)

    def count_hi(trial16):
        pat = jnp.where(trial16 < 0, trial16 ^ 0x7FFF, trial16)
        tb = pltpu.bitcast(jnp.left_shift(pat, 16), F32).astype(BF16)

        def chunk(r0):
            m = hi_ref[pl.ds(r0, ck), :] >= tb
            ones = jnp.where(m, jnp.ones((ck, tq), BF16), jnp.zeros((ck, tq), BF16))
            parts = [ones[i * 16:(i + 1) * 16, :] for i in range(ck // 16)]
            while len(parts) > 1:
                parts = [parts[i] + parts[i + 1] for i in range(0, len(parts), 2)]
            return parts[0].astype(F32)

        cacc = sum_chunks(chunk, jnp.zeros((16, tq), F32))
        return cacc.sum(axis=0, keepdims=True).astype(I32)

    c_nonneg = count_hi(jnp.zeros((1, tq), I32))
    ok0 = c_nonneg >= topk
    state0 = (jnp.where(ok0, 0, -(2 ** 15)).astype(I32),
              jnp.where(ok0, c_nonneg, nck * ck), jnp.where(ok0, 0, c_nonneg))

    def hi_body(b, state):
        lo16, c_ge, c_gt = state
        trial = lo16 + jnp.left_shift(jnp.int32(1), 14 - b)
        c = count_hi(trial)
        ok = c >= topk
        return jnp.where(ok, trial, lo16), jnp.where(ok, c, c_ge), jnp.where(ok, c_gt, c)

    lo16, c_ge, c_gt = lax.fori_loop(0, 15, hi_body, state0)

    def bit_body(b, state):
        lo, c_ge, c_gt = state
        trial = lo + jnp.left_shift(jnp.int32(1), 15 - b)
        c = count(lambda kk, r0: kk >= trial)
        ok = c >= topk
        return jnp.where(ok, trial, lo), jnp.where(ok, c, c_ge), jnp.where(ok, c_gt, c)

    lo, c_ge, c_gt = lax.fori_loop(0, 16, bit_body, (jnp.left_shift(lo16, 16), c_ge, c_gt))

    need = topk - c_gt
    cut_ref[...] = jnp.full((1, tq), seq, I32)

    @pl.when(jnp.max(c_ge) > topk)
    def _():
        nbits = max(1, (seq - 1).bit_length())

        def idx_body(b, jv):
            trial = jv + jnp.left_shift(jnp.int32(1), nbits - 1 - b)
            c = count(lambda kk, r0: (kk == lo)
                      & ((r0 + lax.broadcasted_iota(I32, (ck, tq), 0)) < trial))
            return jnp.where(c < need, trial, jv)

        jv = lax.fori_loop(0, nbits, idx_body, jnp.zeros((1, tq), I32))
        cut_ref[...] = jv + 1

    cut = cut_ref[...]

    m_ref[...] = jnp.full((N_HEADS, 1, tq), M_INIT, F32)
    l_ref[...] = jnp.zeros((N_HEADS, 1, tq), F32)
    acc_ref[...] = jnp.zeros((ATTN_WIDTH, tq), F32)

    def attend(c, last):
        r0 = pl.multiple_of(c * ck, ck)
        kk = key_ref[pl.ds(r0, ck), :]
        kf = kf_ref[pl.ds(r0, ck), :]
        kch = k_ref[0, pl.ds(r0, ck), :]
        vtc = vt_ref[0, c]
        kpos = r0 + row_i
        sel = (kk > lo) | ((kk == lo) & (kpos < cut))
        if last:
            sel = sel & ((kpos // CHUNK) <= qchunk)
            ahead = jnp.maximum(kpos - qpos, 0).astype(F32)
        pen = jnp.where(sel, 0.0, NEG)
        for h in range(N_HEADS):
            p = h // 2
            ks = kch[:, p * LANES:(p + 1) * LANES]
            lg = _dot_nt(jnp.concatenate([ks, kf], axis=1), qmf_ref[h]) + pen
            if last:
                lg = lg - (2.0 * _SLOPE_L2E[h]) * ahead
            lg_ref[h] = lg
        for h in range(N_HEADS):
            m_old = m_ref[h]
            m_new = jnp.maximum(m_old, jnp.max(lg_ref[h], axis=0, keepdims=True))
            alpha = jnp.exp2(m_old - m_new)
            pr = jnp.exp2(lg_ref[h] - m_new)
            l_ref[h] = alpha * l_ref[h] + jnp.sum(pr, axis=0, keepdims=True)
            m_ref[h] = m_new
            rows = slice(h * HEAD_DIM, (h + 1) * HEAD_DIM)
            acc_ref[rows, :] = alpha * acc_ref[rows, :] + _dot(vtc[rows, :], pr.astype(BF16))

    def attend_body(c, carry):
        attend(c, False)
        return carry

    lax.fori_loop(0, nck - 1, attend_body, 0)
    attend(nck - 1, True)

    outs = []
    for h in range(N_HEADS):
        rows = slice(h * HEAD_DIM, (h + 1) * HEAD_DIM)
        o = acc_ref[rows, :] / l_ref[h]
        oms = jnp.mean(o * o, axis=0, keepdims=True)
        outs.append(o * lax.rsqrt(oms + EPS))
    ot = jnp.concatenate(outs, axis=0)
    o_ref[0] = (ot.T * aog_ref[...]).astype(BF16)


def _attn(q, k, qi, kid, vt, wt, attn_out_g, tq, ck):
    b, s, _ = q.shape
    topk = min(TOPK_MAX, s // 4)
    assert tq % LANES == 0 and ck % tq == 0 and s % ck == 0 and s // CHUNK <= 256
    pos = jnp.arange(s)
    feats = [pos // CHUNK] * 3 + [pos % CHUNK] * 3 + [jnp.ones_like(pos)] * 3
    kf = jnp.zeros((s, LANES), F32).at[:, 0:N_FEAT].set(
        jnp.stack(feats, axis=1).astype(F32)).astype(BF16)
    blk = lambda w: pl.BlockSpec((1, tq, w), lambda bi, ji: (bi, ji, 0))
    once = pl.Buffered(1)
    whole = lambda w: pl.BlockSpec((1, s, w), lambda bi, ji: (bi, 0, 0), pipeline_mode=once)
    return pl.pallas_call(
        functools.partial(_attn_kernel, tq=tq, ck=ck, seq=s, topk=topk),
        grid=(b, s // tq),
        in_specs=[blk(ATTN_WIDTH), blk(IDX_HEADS * IDX_DIM),
                  pl.BlockSpec((1, IDX_HEADS, tq), lambda bi, ji: (bi, 0, ji)),
                  whole(2 * IDX_DIM), whole(ATTN_WIDTH),
                  pl.BlockSpec((s, LANES), lambda bi, ji: (0, 0), pipeline_mode=once),
                  pl.BlockSpec((1, s // ck, ATTN_WIDTH, ck), lambda bi, ji: (bi, 0, 0, 0),
                               pipeline_mode=once),
                  pl.BlockSpec((1, ATTN_WIDTH), lambda bi, ji: (0, 0))],
        out_specs=blk(ATTN_WIDTH),
        out_shape=jax.ShapeDtypeStruct((b, s, ATTN_WIDTH), BF16),
        scratch_shapes=[pltpu.VMEM((s, tq), I32),
                        pltpu.VMEM((ATTN_WIDTH, tq), F32),
                        pltpu.VMEM((N_HEADS, 1, tq), F32),
                        pltpu.VMEM((N_HEADS, 1, tq), F32),
                        pltpu.VMEM((IDX_HEADS * tq, LANES), BF16),
                        pltpu.VMEM((N_HEADS, tq, 2 * LANES), BF16),
                        pltpu.VMEM((1, tq), I32),
                        pltpu.VMEM((N_HEADS, ck, tq), F32),
                        pltpu.VMEM((s, tq), BF16)],
        compiler_params=pltpu.CompilerParams(
            dimension_semantics=("arbitrary", "arbitrary"), vmem_limit_bytes=VMEM_LIMIT_BYTES),
        name="attn",
    )(q, qi, wt, kid, k, kf, vt, attn_out_g.reshape(1, ATTN_WIDTH))


def _outproj_kernel(pool_ref, attn_ref, x_ref, wo_ref, g1_ref, sh_ref, sc_ref, ng_ref,
                    x1_ref, h2t_ref):
    mixed = _dot(pool_ref[0], wo_ref[0:POOL_WIDTH, :]) + _dot(attn_ref[0], wo_ref[POOL_WIDTH:, :])
    x1 = x_ref[0] + g1_ref[0] * mixed
    x1_ref[0] = x1
    ms = jnp.mean(x1 * x1, axis=-1, keepdims=True)
    h2 = (x1 * lax.rsqrt(ms + EPS) * ng_ref[...]) * (1.0 + sc_ref[0]) + sh_ref[0]
    h2t_ref[...] = h2.T.astype(BF16)


def _outproj(pool_out, attn_out, x, w_out, g1, sh2, sc2, norm_g, tm):
    b, s, d = x.shape
    nt = s // tm
    tok = lambda w: pl.BlockSpec((1, tm, w), lambda bi, ti: (bi, ti, 0))
    per_b = pl.BlockSpec((1, 1, d), lambda bi, ti: (bi, 0, 0))
    return pl.pallas_call(
        _outproj_kernel,
        grid=(b, nt),
        in_specs=[tok(POOL_WIDTH), tok(ATTN_WIDTH), tok(d),
                  pl.BlockSpec((POOL_WIDTH + ATTN_WIDTH, d), lambda bi, ti: (0, 0)),
                  per_b, per_b, per_b, pl.BlockSpec((1, d), lambda bi, ti: (0, 0))],
        out_specs=(tok(d), pl.BlockSpec((d, tm), lambda bi, ti: (0, bi * nt + ti))),
        out_shape=(jax.ShapeDtypeStruct((b, s, d), F32), jax.ShapeDtypeStruct((d, b * s), BF16)),
        compiler_params=pltpu.CompilerParams(dimension_semantics=("arbitrary", "arbitrary")),
        name="outproj",
    )(pool_out, attn_out, x, w_out.astype(BF16), g1, sh2, sc2, norm_g.reshape(1, d))


def _batcher_pairs(n):
    pairs = []

    def merge(lo, cnt, r):
        step = r * 2
        if step < cnt:
            merge(lo, cnt, step)
            merge(lo + r, cnt, step)
            for i in range(lo + r, lo + cnt - r, step):
                pairs.append((i, i + r))
        else:
            pairs.append((lo, lo + r))

    def sort(lo, cnt):
        if cnt > 1:
            m = cnt // 2
            sort(lo, m)
            sort(lo + m, m)
            merge(lo, cnt, 1)

    sort(0, n)
    return tuple(pairs)


_SORT16 = _batcher_pairs(PEER_TOPK)
MM_GROUP = 8


def _top16_desc(x):
    x = list(x)
    for i, j in _SORT16:
        x[i], x[j] = jnp.maximum(x[i], x[j]), jnp.minimum(x[i], x[j])
    for shift in (4, 2, 1):
        y = [jnp.maximum(x[k], pltpu.roll(x[PEER_TOPK - 1 - k], shift, 0)) for k in range(PEER_TOPK)]
        for d in (8, 4, 2, 1):
            for k in range(PEER_TOPK):
                if k & d == 0:
                    y[k], y[k + d] = jnp.maximum(y[k], y[k + d]), jnp.minimum(y[k], y[k + d])
        x = y
    return x


def _peer_kernel(h2t_ref, x1_ref, g2_ref, wqt_ref, sk_ref, u_ref, vt_ref, o_ref,
                 s1_ref, s2_ref, thr_ref, p_ref, acc_ref, s1b_ref, *, tm, eb, ne):
    e = pl.program_id(1)
    nsub = eb // PEER_KEYS
    ntl = tm // LANES

    @pl.when(e == 0)
    def _prologue():
        h2t = h2t_ref[...]
        sub = lax.broadcasted_iota(I32, (8, tm), 0)
        ninf = jnp.full((8, tm), -jnp.inf, F32)

        def compact(t, base):
            out = t[base + 7]
            for r in range(6, -1, -1):
                out = jnp.where(sub == r, t[base + r], out)
            return out

        def split8(st):
            return [st[k * 8:(k + 1) * 8, :] for k in range(PEER_KEYS // 8)]

        def head(hd, carry):
            q1 = _dot(wqt_ref[2 * hd], h2t).astype(BF16)
            q2 = _dot(wqt_ref[2 * hd + 1], h2t).astype(BF16)
            st1 = _dot(sk_ref[2 * hd], q1) * LOG2E
            st2 = _dot(sk_ref[2 * hd + 1], q2) * LOG2E
            for tl in range(ntl):
                s2_ref[hd, tl] = st2[:, tl * LANES:(tl + 1) * LANES]
            t1 = _top16_desc(split8(st1))
            t2 = _top16_desc(split8(st2))
            a2lo, a2hi = compact(t2, 0), compact(t2, 8)

            def candidates(t1x):
                cands = [t1x[0] + a2lo, t1x[0] + a2hi, t1x[1] + a2lo]
                for r1 in range(2, 8):
                    cands.append(jnp.where(sub < PEER_TOPK // (r1 + 1), t1x[r1] + a2lo, ninf))
                cands.append(compact(t1x, 8) + t2[0])
                return cands

            def kth(cands):
                return _top16_desc(cands + [ninf] * (PEER_TOPK - len(cands)))[PEER_TOPK - 1]

            cands = candidates(t1)
            thr = kth(cands)
            m = t1[0] + t2[0]
            z8 = jnp.zeros((8, tm), F32)
            for c in cands:
                z8 = z8 + jnp.where(c >= thr, jnp.exp2(c - m), 0.0)
            mz = m + jnp.log2(jnp.broadcast_to(jnp.sum(z8, axis=0, keepdims=True), (8, tm))) + 1.0
            s1_ref[hd] = st1 - mz[0:1, :]
            thr_ref[pl.ds(hd, 1), :] = kth(candidates([t - mz for t in t1]))[0:1, :]
            return carry

        lax.fori_loop(0, PEER_HEADS, head, 0)
        acc_ref[...] = jnp.zeros_like(acc_ref)

    i0 = pl.multiple_of(e * nsub, nsub)
    s1rows = [s1_ref[hd, pl.ds(i0, nsub), :] for hd in range(PEER_HEADS)]
    for hd in range(PEER_HEADS):
        s1b_ref[hd] = s1rows[hd]
    gsz = MM_GROUP * PEER_KEYS
    for g in range(nsub // MM_GROUP):
        grows = slice(g * gsz, (g + 1) * gsz)
        a_grp = _dot(u_ref[grows, :], h2t_ref[...])
        for sg in range(MM_GROUP):
            ii = g * MM_GROUP + sg
            rows = slice(ii * PEER_KEYS, (ii + 1) * PEER_KEYS)
            for tl in range(ntl):
                cols = slice(tl * LANES, (tl + 1) * LANES)
                w = jnp.zeros((PEER_KEYS, LANES), F32)
                for hd in range(PEER_HEADS):
                    sm = s1b_ref[hd, ii:ii + 1, cols] + s2_ref[hd, tl]
                    w = w + jnp.where(sm >= thr_ref[hd:hd + 1, cols], jnp.exp2(sm), 0.0)
                a = a_grp[sg * PEER_KEYS:(sg + 1) * PEER_KEYS, cols]
                gl = a * (1.0 + lax.erf(a * (1.0 / math.sqrt(2.0))))
                p_ref[rows, cols] = (gl * w).astype(BF16)
        acc_ref[...] += _dot(vt_ref[:, grows], p_ref[grows, :])

    @pl.when(e == ne - 1)
    def _epilogue():
        o_ref[...] = x1_ref[...] + g2_ref[0] * acc_ref[...].T


def _peer(h2t, x1, g2, wq, sub_keys, u_tab, v_tab, seq, tm, eb):
    d, t = h2t.shape
    ne = u_tab.shape[0] // eb
    half = PEER_KEYS
    wqt = wq.T.reshape(2 * PEER_HEADS, half, d).astype(BF16)
    sk = sub_keys.reshape(2 * PEER_HEADS, PEER_KEYS, half).astype(BF16)
    ub = u_tab.astype(BF16)
    vtb = v_tab.T.astype(BF16)
    hs = (PEER_HEADS, PEER_KEYS, tm)
    return pl.pallas_call(
        functools.partial(_peer_kernel, tm=tm, eb=eb, ne=ne),
        grid=(t // tm, ne),
        in_specs=[pl.BlockSpec((d, tm), lambda i, e: (0, i)),
                  pl.BlockSpec((tm, d), lambda i, e: (i, 0)),
                  pl.BlockSpec((1, 1, d), lambda i, e: ((i * tm) // seq, 0, 0)),
                  pl.BlockSpec((2 * PEER_HEADS, half, d), lambda i, e: (0, 0, 0)),
                  pl.BlockSpec((2 * PEER_HEADS, PEER_KEYS, half), lambda i, e: (0, 0, 0)),
                  pl.BlockSpec((eb, d), lambda i, e: (e, 0)),
                  pl.BlockSpec((d, eb), lambda i, e: (0, e))],
        out_specs=pl.BlockSpec((tm, d), lambda i, e: (i, 0)),
        out_shape=jax.ShapeDtypeStruct((t, d), F32),
        scratch_shapes=[pltpu.VMEM(hs, F32),
                        pltpu.VMEM((PEER_HEADS, tm // LANES, PEER_KEYS, LANES), F32),
                        pltpu.VMEM((PEER_HEADS, tm), F32),
                        pltpu.VMEM((eb, tm), BF16),
                        pltpu.VMEM((d, tm), F32),
                        pltpu.VMEM((PEER_HEADS, eb // PEER_KEYS, tm), F32)],
        compiler_params=pltpu.CompilerParams(
            dimension_semantics=("arbitrary", "arbitrary"), vmem_limit_bytes=VMEM_LIMIT_BYTES),
        name="peer",
    )(h2t, x1, g2, wqt, sk, ub, vtb)


def _tile(n, pref):
    t = min(pref, n)
    assert n % t == 0
    return t


def kernel(x, c, ada_w, ada_b, norm1_g, w_in, pool_w, pool_scale, pool_out_g, q_norm_g, k_norm_g,
           attn_out_g, w_out, norm2_g, peer_wq, peer_subkeys, peer_u, peer_v):
    b, s, d = x.shape
    depth = ada_w.shape[0]
    tm = _tile(s, 512)
    tq = _tile(s, 2 * LANES)
    ck = _tile(tm, 512)
    tp = _tile(s, 512)
    eb = 2048
    for l in range(depth):
        mod = _ada(c, ada_w[l], ada_b[l]).reshape(b, 1, 6 * d)
        sh1, sc1, g1, sh2, sc2, g2 = jnp.split(mod, 6, axis=-1)
        q, k, qi, kid, vt, wt, pool_out = _inproj(
            x, sh1, sc1, norm1_g[l], w_in[l], pool_w[l], pool_scale[l], pool_out_g[l],
            q_norm_g[l], k_norm_g[l], tm, ck)
        attn_out = _attn(q, k, qi, kid, vt, wt, attn_out_g[l], tq, ck)
        x1, h2t = _outproj(pool_out, attn_out, x, w_out[l], g1, sh2, sc2, norm2_g[l], tm)
        x = _peer(h2t, x1.reshape(b * s, d), g2, peer_wq[l], peer_subkeys[l],
                  peer_u[l], peer_v[l], s, tp, eb).reshape(b, s, d)
    return x
```

```python
import functools
import math

import ml_dtypes
import numpy as np

import jax
import jax.numpy as jnp
from jax import lax
from jax.experimental import pallas as pl
from jax.experimental.pallas import tpu as pltpu

F32 = jnp.float32
BF16 = jnp.bfloat16
I32 = jnp.int32

CHUNK = 64
EPS = 1e-6
POOL_WINDOWS = (2, 4, 8, 16)
POOL_GROUP = 128
POOL_WIDTH = 512
ATTN_WIDTH = 512
HEAD_DIM = 64
N_HEADS = 8
IDX_HEADS = 8
IDX_DIM = 64
TOPK_MAX = 256
PEER_HEADS = 8
PEER_KEYS = 128
PEER_TOPK = 16
NEG = -1e30

LANES = 128
V7X_VMEM_BYTES = 64 * 1024 * 1024
VMEM_LIMIT_BYTES = (7 * V7X_VMEM_BYTES) // 8
POOL_HALO = 16
LOG2E = math.log2(math.e)
M_INIT = 0.5 * NEG
N_FEAT = 9

NT_DIMS = (((1,), (1,)), ((), ()))


def _dot(a, b):
    return jnp.dot(a, b, preferred_element_type=F32)


def _dot_nt(a, b):
    return lax.dot_general(a, b, NT_DIMS, preferred_element_type=F32)


def _bf16_split3(c):
    c = np.float32(c)
    p1 = np.float32(c.astype(ml_dtypes.bfloat16))
    r = np.float32(c - p1)
    p2 = np.float32(r.astype(ml_dtypes.bfloat16))
    p3 = np.float32(np.float32(r - p2).astype(ml_dtypes.bfloat16))
    return float(p1), float(p2), float(p3)


_SLOPE_L2E = tuple(float(np.float32(2.0 ** (-8.0 * (h + 1) / N_HEADS) * LOG2E)) for h in range(N_HEADS))


def _ada_kernel(c_ref, w_ref, b_ref, o_ref):
    c = c_ref[...]
    cond = c * jax.nn.sigmoid(c)
    o_ref[...] = jnp.dot(cond, w_ref[...], preferred_element_type=F32,
                         precision=lax.Precision.HIGHEST) + b_ref[...]


def _ada(c, w, b):
    bsz, d = c.shape
    n = w.shape[1]
    tn = 1024
    return pl.pallas_call(
        _ada_kernel,
        grid=(n // tn,),
        in_specs=[pl.BlockSpec((bsz, d), lambda j: (0, 0)),
                  pl.BlockSpec((d, tn), lambda j: (0, j)),
                  pl.BlockSpec((1, tn), lambda j: (0, j))],
        out_specs=pl.BlockSpec((bsz, tn), lambda j: (0, j)),
        out_shape=jax.ShapeDtypeStruct((bsz, n), F32),
        name="ada",
    )(c, w, b.reshape(1, n))


def _group_sumsq(t, ones_bd):
    sq = t * t
    hi = sq.astype(BF16)
    lo = (sq - hi.astype(F32)).astype(BF16)
    return _dot(hi, ones_bd) + _dot(lo, ones_bd)


def _inproj_kernel(x_ref, sh_ref, sc_ref, g_ref, wnat_ref, wki_ref, wvt_ref, wwt_ref, ones_ref,
                   qg_ref, kg_ref, poolw_ref, pscale_ref, pg_ref,
                   q_ref, k_ref, qi_ref, kid_ref, vt_ref, wt_ref, pool_ref,
                   ubuf, *, tm, kc):
    i = pl.program_id(1)
    x = x_ref[0]
    ms = jnp.mean(x * x, axis=-1, keepdims=True)
    h = (x * lax.rsqrt(ms + EPS) * g_ref[...]) * (1.0 + sc_ref[0]) + sh_ref[0]
    hb = h.astype(BF16)

    proj = _dot(hb, wnat_ref[...])
    u = proj[:, 0:512]
    q = proj[:, 512:1024]
    k = proj[:, 1024:1536]
    qi = proj[:, 1536:2048]

    ones_bd = ones_ref[...]
    qn = q * lax.rsqrt(_group_sumsq(q, ones_bd) * (1.0 / HEAD_DIM) + EPS) * qg_ref[...]
    kn = k * lax.rsqrt(_group_sumsq(k, ones_bd) * (1.0 / HEAD_DIM) + EPS) * kg_ref[...]
    q_ref[0] = qn.astype(BF16)
    k_ref[0] = kn.astype(BF16)
    qi_ref[0] = (qi * (IDX_DIM ** -0.5)).astype(BF16)
    kid_ref[0] = _dot(hb, wki_ref[...]).astype(BF16)
    vt = _dot_nt(wvt_ref[...], hb).astype(BF16)
    for ci in range(tm // kc):
        vt_ref[0, ci] = vt[:, ci * kc:(ci + 1) * kc]
    wt = _dot_nt(wwt_ref[...], hb)
    wt_ref[0] = wt[0:IDX_HEADS, :] * (IDX_HEADS ** -0.5)

    @pl.when(i == 0)
    def _():
        ubuf[0:POOL_HALO, :] = jnp.zeros((POOL_HALO, POOL_WIDTH), F32)

    ubuf[POOL_HALO:POOL_HALO + tm, :] = u
    tpos = i * tm + lax.broadcasted_iota(I32, (tm, POOL_GROUP), 0)
    parts = []
    for gi, win in enumerate(POOL_WINDOWS):
        lo_l, hi_l = gi * POOL_GROUP, (gi + 1) * POOL_GROUP
        ws = ubuf[POOL_HALO:POOL_HALO + tm, lo_l:hi_l]
        for j in range(1, win):
            ws = ws + ubuf[POOL_HALO - j:POOL_HALO - j + tm, lo_l:hi_l]
        cnt = jnp.minimum(tpos + 1, win).astype(F32)
        pooled = ws / cnt - u[:, lo_l:hi_l]
        parts.append(_dot(pooled.astype(BF16), poolw_ref[gi]))
    mixed = jnp.concatenate(parts, axis=-1) * pscale_ref[...]
    pms = jnp.mean(mixed * mixed, axis=-1, keepdims=True)
    pool_ref[0] = (mixed * lax.rsqrt(pms + EPS) * pg_ref[...]).astype(BF16)
    ubuf[0:POOL_HALO, :] = ubuf[tm:tm + POOL_HALO, :]


def _inproj(x, sh1, sc1, norm_g, w_in, pool_w, pool_scale, pool_out_g, q_norm_g, k_norm_g, tm, kc):
    b, s, d = x.shape
    nt = s // tm
    w_u, w_q, w_k, w_v, w_qi, w_ki, w_wi = jnp.split(
        w_in, (512, 1024, 1536, 2048, 2560, 2624), axis=-1)
    wnat = jnp.concatenate([w_u, w_q, w_k, w_qi], axis=-1).astype(BF16)
    wki = jnp.concatenate([w_ki, w_ki], axis=-1).astype(BF16)
    wvt = w_v.T.astype(BF16)
    wwt = jnp.concatenate([w_wi.T, jnp.zeros((16 - IDX_HEADS, d), F32)], axis=0).astype(BF16)
    gid = jnp.arange(ATTN_WIDTH) // HEAD_DIM
    ones_bd = (gid[:, None] == gid[None, :]).astype(BF16)
    qg = (jnp.tile(q_norm_g, N_HEADS) * (HEAD_DIM ** -0.5 * LOG2E)).reshape(1, ATTN_WIDTH)
    kg = jnp.tile(k_norm_g, N_HEADS).reshape(1, ATTN_WIDTH)

    full = lambda shape: pl.BlockSpec(shape, lambda bi, ti: (0,) * len(shape))
    tok = lambda w: pl.BlockSpec((1, tm, w), lambda bi, ti: (bi, ti, 0))
    per_b = pl.BlockSpec((1, 1, d), lambda bi, ti: (bi, 0, 0))
    out_shapes = (
        jax.ShapeDtypeStruct((b, s, ATTN_WIDTH), BF16),
        jax.ShapeDtypeStruct((b, s, ATTN_WIDTH), BF16),
        jax.ShapeDtypeStruct((b, s, IDX_HEADS * IDX_DIM), BF16),
        jax.ShapeDtypeStruct((b, s, 2 * IDX_DIM), BF16),
        jax.ShapeDtypeStruct((b, s // kc, ATTN_WIDTH, kc), BF16),
        jax.ShapeDtypeStruct((b, IDX_HEADS, s), F32),
        jax.ShapeDtypeStruct((b, s, POOL_WIDTH), BF16),
    )
    out_specs = (
        tok(ATTN_WIDTH), tok(ATTN_WIDTH), tok(IDX_HEADS * IDX_DIM), tok(2 * IDX_DIM),
        pl.BlockSpec((1, tm // kc, ATTN_WIDTH, kc), lambda bi, ti: (bi, ti, 0, 0)),
        pl.BlockSpec((1, IDX_HEADS, tm), lambda bi, ti: (bi, 0, ti)),
        tok(POOL_WIDTH),
    )
    return pl.pallas_call(
        functools.partial(_inproj_kernel, tm=tm, kc=kc),
        grid=(b, nt),
        in_specs=[tok(d), per_b, per_b, full((1, d)), full((d, 2048)), full((d, 2 * IDX_DIM)),
                  full((ATTN_WIDTH, d)), full((16, d)), full((ATTN_WIDTH, ATTN_WIDTH)),
                  full((1, ATTN_WIDTH)), full((1, ATTN_WIDTH)),
                  full((len(POOL_WINDOWS), POOL_GROUP, POOL_GROUP)),
                  full((1, POOL_WIDTH)), full((1, POOL_WIDTH))],
        out_specs=out_specs,
        out_shape=out_shapes,
        scratch_shapes=[pltpu.VMEM((tm + POOL_HALO, POOL_WIDTH), F32)],
        compiler_params=pltpu.CompilerParams(
            dimension_semantics=("arbitrary", "arbitrary"), vmem_limit_bytes=VMEM_LIMIT_BYTES),
        name="inproj",
    )(x, sh1, sc1, norm_g.reshape(1, d), wnat, wki, wvt, wwt, ones_bd, qg, kg,
      pool_w.astype(BF16), pool_scale.reshape(1, -1), pool_out_g.reshape(1, -1))


def _attn_kernel(q_ref, qi_ref, wt_ref, kid_ref, k_ref, kf_ref, vt_ref, aog_ref, o_ref,
                 key_ref, acc_ref, m_ref, l_ref, qim_ref, qmf_ref, cut_ref, lg_ref,
                 hi_ref, mid_ref, low_ref,
                 *, tq, ck, seq, topk):
    j = pl.program_id(1)
    nck = ((j + 1) * tq + ck - 1) // ck
    qpos = j * tq + lax.broadcasted_iota(I32, (1, tq), 1)
    qchunk = qpos // CHUNK
    row_i = lax.broadcasted_iota(I32, (ck, tq), 0)

    lane = lax.broadcasted_iota(I32, (tq, LANES), 1)
    tcol = (j * tq + lax.broadcasted_iota(I32, (tq, LANES), 0)).astype(F32)
    for h in range(N_HEADS):
        p = h // 2
        keep = (lane < HEAD_DIM) if h % 2 == 0 else (lane >= HEAD_DIM)
        qi_slab = qi_ref[0, :, p * LANES:(p + 1) * LANES]
        q_slab = q_ref[0, :, p * LANES:(p + 1) * LANES]
        qim_ref[h * tq:(h + 1) * tq, :] = jnp.where(keep, qi_slab, jnp.zeros_like(qi_slab))
        qmf_ref[h, :, 0:LANES] = jnp.where(keep, q_slab, jnp.zeros_like(q_slab))
        c1, c2, c3 = _bf16_split3(_SLOPE_L2E[h])
        u = -_SLOPE_L2E[h] * tcol
        u1 = u.astype(BF16).astype(F32)
        r = u - u1
        u2 = r.astype(BF16).astype(F32)
        u3 = (r - u2).astype(BF16).astype(F32)
        feat = jnp.zeros((tq, LANES), F32)
        for li, val in enumerate((64.0 * c1, 64.0 * c2, 64.0 * c3, c1, c2, c3, u1, u2, u3)):
            feat = jnp.where(lane == li, val, feat)
        qmf_ref[h, :, LANES:2 * LANES] = feat.astype(BF16)

    def scores(c, last):
        r0 = pl.multiple_of(c * ck, ck)
        d = _dot_nt(kid_ref[0, pl.ds(r0, ck), :], qim_ref[...])
        acc = jnp.zeros((ck, tq), F32)
        for h in range(IDX_HEADS):
            acc = acc + wt_ref[0, h:h + 1, :] * jnp.maximum(d[:, h * tq:(h + 1) * tq], 0.0)
        if last:
            acc = jnp.where(((r0 + row_i) // CHUNK) <= qchunk, acc, NEG)
        bits = pltpu.bitcast(acc, I32)
        key = jnp.where(bits < 0, bits ^ 0x7FFFFFFF, bits)
        key_ref[pl.ds(r0, ck), :] = key
        hi_ref[pl.ds(r0, ck), :] = pltpu.bitcast(bits & -65536, F32).astype(BF16)
        mid_ref[pl.ds(r0, ck), :] = ((key >> 8) & 0xFF).astype(F32).astype(BF16)
        low_ref[pl.ds(r0, ck), :] = (key & 0xFF).astype(F32).astype(BF16)

    def scores_body(c, carry):
        scores(c, False)
        return carry

    lax.fori_loop(0, nck - 1, scores_body, 0)
    scores(nck - 1, True)

    def sum_chunks(chunk_fn, init):
        def body(i, acc):
            r0 = pl.multiple_of(i * (2 * ck), 2 * ck)
            return acc + chunk_fn(r0) + chunk_fn(pl.multiple_of(r0 + ck, ck))
        acc = lax.fori_loop(0, nck // 2, body, init)
        return lax.cond(nck % 2 == 1,
                        lambda a: a + chunk_fn(pl.multiple_of((nck - 1) * ck, ck)),
                        lambda a: a, acc)

    def count(pred):
        def chunk(r0):
            m = pred(key_ref[pl.ds(r0, ck), :], r0).astype(I32)
            return m.reshape(ck // 8, 8, tq).sum(axis=0)
        return sum_chunks(chunk, jnp.zeros((8, tq), I32)).sum(axis=0, keepdims=True)

    def count_packed(ref, tb):
        def chunk(r0):
            m = ref[pl.ds(r0, ck), :] >= tb
            ones = jnp.where(m, jnp.ones((ck, tq), BF16), jnp.zeros((ck, tq), BF16))
            parts = [ones[i * 16:(i + 1) * 16, :] for i in range(ck // 16)]
            while len(parts) > 1:
                parts = [parts[i] + parts[i + 1] for i in range(0, len(parts), 2)]
            return parts[0].astype(F32)

        cacc = sum_chunks(chunk, jnp.zeros((16, tq), F32))
        return cacc.sum(axis=0, keepdims=True).astype(I32)

    def bf16_of_key16(k16):
        pat = jnp.where(k16 < 0, k16 ^ 0x7FFF, k16)
        return pltpu.bitcast(jnp.left_shift(pat, 16), F32).astype(BF16)

    def count_hi(trial16):
        return count_packed(hi_ref, bf16_of_key16(trial16))

    c_nonneg = count_hi(jnp.zeros((1, tq), I32))
    ok0 = c_nonneg >= topk
    state0 = (jnp.where(ok0, 0, -(2 ** 15)).astype(I32),
              jnp.where(ok0, c_nonneg, nck * ck), jnp.where(ok0, 0, c_nonneg))

    def hi_body(b, state):
        lo16, c_ge, c_gt = state
        trial = lo16 + jnp.left_shift(jnp.int32(1), 14 - b)
        c = count_hi(trial)
        ok = c >= topk
        return jnp.where(ok, trial, lo16), jnp.where(ok, c, c_ge), jnp.where(ok, c_gt, c)

    lo16, c_ge, c_gt = lax.fori_loop(0, 15, hi_body, state0)

    def resolve_byte(byte_ref, above_ref, above_val, c_ge, c_gt):
        def rewrite(c, carry):
            r0 = pl.multiple_of(c * ck, ck)
            above = above_ref[pl.ds(r0, ck), :]
            byte_ref[pl.ds(r0, ck), :] = jnp.where(
                above > above_val, jnp.full((ck, tq), 256.0, BF16),
                jnp.where(above < above_val, jnp.full((ck, tq), -1.0, BF16),
                          byte_ref[pl.ds(r0, ck), :]))
            return carry

        lax.fori_loop(0, nck, rewrite, 0)

        def byte_body(b, state):
            val, c_ge, c_gt = state
            trial = val + jnp.left_shift(jnp.int32(1), 7 - b)
            c = count_packed(byte_ref, trial.astype(F32).astype(BF16))
            ok = c >= topk
            return jnp.where(ok, trial, val), jnp.where(ok, c, c_ge), jnp.where(ok, c_gt, c)

        return lax.fori_loop(0, 8, byte_body, (jnp.zeros((1, tq), I32), c_ge, c_gt))

    b1, c_ge, c_gt = resolve_byte(mid_ref, hi_ref, bf16_of_key16(lo16), c_ge, c_gt)
    b0, c_ge, c_gt = resolve_byte(low_ref, mid_ref, b1.astype(F32).astype(BF16), c_ge, c_gt)
    lo = jnp.left_shift(lo16, 16) + jnp.left_shift(b1, 8) + b0

    need = topk - c_gt
    cut_ref[...] = jnp.full((1, tq), seq, I32)

    @pl.when(jnp.max(c_ge) > topk)
    def _():
        nbits = max(1, (seq - 1).bit_length())

        def idx_body(b, jv):
            trial = jv + jnp.left_shift(jnp.int32(1), nbits - 1 - b)
            c = count(lambda kk, r0: (kk == lo)
                      & ((r0 + lax.broadcasted_iota(I32, (ck, tq), 0)) < trial))
            return jnp.where(c < need, trial, jv)

        jv = lax.fori_loop(0, nbits, idx_body, jnp.zeros((1, tq), I32))
        cut_ref[...] = jv + 1

    cut = cut_ref[...]

    m_ref[...] = jnp.full((N_HEADS, 1, tq), M_INIT, F32)
    l_ref[...] = jnp.zeros((N_HEADS, 1, tq), F32)
    acc_ref[...] = jnp.zeros((ATTN_WIDTH, tq), F32)

    def attend(c, last):
        r0 = pl.multiple_of(c * ck, ck)
        kk = key_ref[pl.ds(r0, ck), :]
        kf = kf_ref[pl.ds(r0, ck), :]
        kch = k_ref[0, pl.ds(r0, ck), :]
        vtc = vt_ref[0, c]
        kpos = r0 + row_i
        sel = (kk > lo) | ((kk == lo) & (kpos < cut))
        if last:
            sel = sel & ((kpos // CHUNK) <= qchunk)
            ahead = jnp.maximum(kpos - qpos, 0).astype(F32)
        pen = jnp.where(sel, 0.0, NEG)
        for h in range(N_HEADS):
            p = h // 2
            ks = kch[:, p * LANES:(p + 1) * LANES]
            lg = _dot_nt(jnp.concatenate([ks, kf], axis=1), qmf_ref[h]) + pen
            if last:
                lg = lg - (2.0 * _SLOPE_L2E[h]) * ahead
            lg_ref[h] = lg
        for h in range(N_HEADS):
            m_old = m_ref[h]
            m_new = jnp.maximum(m_old, jnp.max(lg_ref[h], axis=0, keepdims=True))
            alpha = jnp.exp2(m_old - m_new)
            pr = jnp.exp2(lg_ref[h] - m_new)
            l_ref[h] = alpha * l_ref[h] + jnp.sum(pr, axis=0, keepdims=True)
            m_ref[h] = m_new
            rows = slice(h * HEAD_DIM, (h + 1) * HEAD_DIM)
            acc_ref[rows, :] = alpha * acc_ref[rows, :] + _dot(vtc[rows, :], pr.astype(BF16))

    def attend_body(c, carry):
        attend(c, False)
        return carry

    lax.fori_loop(0, nck - 1, attend_body, 0)
    attend(nck - 1, True)

    outs = []
    for h in range(N_HEADS):
        rows = slice(h * HEAD_DIM, (h + 1) * HEAD_DIM)
        o = acc_ref[rows, :] / l_ref[h]
        oms = jnp.mean(o * o, axis=0, keepdims=True)
        outs.append(o * lax.rsqrt(oms + EPS))
    ot = jnp.concatenate(outs, axis=0)
    o_ref[0] = (ot.T * aog_ref[...]).astype(BF16)


def _attn(q, k, qi, kid, vt, wt, attn_out_g, tq, ck):
    b, s, _ = q.shape
    topk = min(TOPK_MAX, s // 4)
    assert tq % LANES == 0 and ck % tq == 0 and s % ck == 0 and s // CHUNK <= 256
    pos = jnp.arange(s)
    feats = [pos // CHUNK] * 3 + [pos % CHUNK] * 3 + [jnp.ones_like(pos)] * 3
    kf = jnp.zeros((s, LANES), F32).at[:, 0:N_FEAT].set(
        jnp.stack(feats, axis=1).astype(F32)).astype(BF16)
    blk = lambda w: pl.BlockSpec((1, tq, w), lambda bi, ji: (bi, ji, 0))
    once = pl.Buffered(1)
    whole = lambda w: pl.BlockSpec((1, s, w), lambda bi, ji: (bi, 0, 0), pipeline_mode=once)
    return pl.pallas_call(
        functools.partial(_attn_kernel, tq=tq, ck=ck, seq=s, topk=topk),
        grid=(b, s // tq),
        in_specs=[blk(ATTN_WIDTH), blk(IDX_HEADS * IDX_DIM),
                  pl.BlockSpec((1, IDX_HEADS, tq), lambda bi, ji: (bi, 0, ji)),
                  whole(2 * IDX_DIM), whole(ATTN_WIDTH),
                  pl.BlockSpec((s, LANES), lambda bi, ji: (0, 0), pipeline_mode=once),
                  pl.BlockSpec((1, s // ck, ATTN_WIDTH, ck), lambda bi, ji: (bi, 0, 0, 0),
                               pipeline_mode=once),
                  pl.BlockSpec((1, ATTN_WIDTH), lambda bi, ji: (0, 0))],
        out_specs=blk(ATTN_WIDTH),
        out_shape=jax.ShapeDtypeStruct((b, s, ATTN_WIDTH), BF16),
        scratch_shapes=[pltpu.VMEM((s, tq), I32),
                        pltpu.VMEM((ATTN_WIDTH, tq), F32),
                        pltpu.VMEM((N_HEADS, 1, tq), F32),
                        pltpu.VMEM((N_HEADS, 1, tq), F32),
                        pltpu.VMEM((IDX_HEADS * tq, LANES), BF16),
                        pltpu.VMEM((N_HEADS, tq, 2 * LANES), BF16),
                        pltpu.VMEM((1, tq), I32),
                        pltpu.VMEM((N_HEADS, ck, tq), F32),
                        pltpu.VMEM((s, tq), BF16), pltpu.VMEM((s, tq), BF16),
                        pltpu.VMEM((s, tq), BF16)],
        compiler_params=pltpu.CompilerParams(
            dimension_semantics=("arbitrary", "arbitrary"), vmem_limit_bytes=VMEM_LIMIT_BYTES),
        name="attn",
    )(q, qi, wt, kid, k, kf, vt, attn_out_g.reshape(1, ATTN_WIDTH))


def _outproj_kernel(pool_ref, attn_ref, x_ref, wo_ref, g1_ref, sh_ref, sc_ref, ng_ref,
                    x1_ref, h2t_ref):
    mixed = _dot(pool_ref[0], wo_ref[0:POOL_WIDTH, :]) + _dot(attn_ref[0], wo_ref[POOL_WIDTH:, :])
    x1 = x_ref[0] + g1_ref[0] * mixed
    x1_ref[0] = x1
    ms = jnp.mean(x1 * x1, axis=-1, keepdims=True)
    h2 = (x1 * lax.rsqrt(ms + EPS) * ng_ref[...]) * (1.0 + sc_ref[0]) + sh_ref[0]
    h2t_ref[...] = h2.T.astype(BF16)


def _outproj(pool_out, attn_out, x, w_out, g1, sh2, sc2, norm_g, tm):
    b, s, d = x.shape
    nt = s // tm
    tok = lambda w: pl.BlockSpec((1, tm, w), lambda bi, ti: (bi, ti, 0))
    per_b = pl.BlockSpec((1, 1, d), lambda bi, ti: (bi, 0, 0))
    return pl.pallas_call(
        _outproj_kernel,
        grid=(b, nt),
        in_specs=[tok(POOL_WIDTH), tok(ATTN_WIDTH), tok(d),
                  pl.BlockSpec((POOL_WIDTH + ATTN_WIDTH, d), lambda bi, ti: (0, 0)),
                  per_b, per_b, per_b, pl.BlockSpec((1, d), lambda bi, ti: (0, 0))],
        out_specs=(tok(d), pl.BlockSpec((d, tm), lambda bi, ti: (0, bi * nt + ti))),
        out_shape=(jax.ShapeDtypeStruct((b, s, d), F32), jax.ShapeDtypeStruct((d, b * s), BF16)),
        compiler_params=pltpu.CompilerParams(dimension_semantics=("arbitrary", "arbitrary")),
        name="outproj",
    )(pool_out, attn_out, x, w_out.astype(BF16), g1, sh2, sc2, norm_g.reshape(1, d))


def _batcher_pairs(n):
    pairs = []

    def merge(lo, cnt, r):
        step = r * 2
        if step < cnt:
            merge(lo, cnt, step)
            merge(lo + r, cnt, step)
            for i in range(lo + r, lo + cnt - r, step):
                pairs.append((i, i + r))
        else:
            pairs.append((lo, lo + r))

    def sort(lo, cnt):
        if cnt > 1:
            m = cnt // 2
            sort(lo, m)
            sort(lo + m, m)
            merge(lo, cnt, 1)

    sort(0, n)
    return tuple(pairs)


_SORT16 = _batcher_pairs(PEER_TOPK)
MM_GROUP = 8


def _top16_desc(x):
    x = list(x)
    for i, j in _SORT16:
        x[i], x[j] = jnp.maximum(x[i], x[j]), jnp.minimum(x[i], x[j])
    for shift in (4, 2, 1):
        y = [jnp.maximum(x[k], pltpu.roll(x[PEER_TOPK - 1 - k], shift, 0)) for k in range(PEER_TOPK)]
        for d in (8, 4, 2, 1):
            for k in range(PEER_TOPK):
                if k & d == 0:
                    y[k], y[k + d] = jnp.maximum(y[k], y[k + d]), jnp.minimum(y[k], y[k + d])
        x = y
    return x


def _peer_kernel(h2t_ref, x1_ref, g2_ref, wqt_ref, sk_ref, u_ref, vt_ref, o_ref,
                 s1_ref, s2_ref, thr_ref, p_ref, acc_ref, s1b_ref, *, tm, eb, ne):
    e = pl.program_id(1)
    nsub = eb // PEER_KEYS
    ntl = tm // LANES

    @pl.when(e == 0)
    def _prologue():
        h2t = h2t_ref[...]
        sub = lax.broadcasted_iota(I32, (8, tm), 0)
        ninf = jnp.full((8, tm), -jnp.inf, F32)

        def compact(t, base):
            out = t[base + 7]
            for r in range(6, -1, -1):
                out = jnp.where(sub == r, t[base + r], out)
            return out

        def split8(st):
            return [st[k * 8:(k + 1) * 8, :] for k in range(PEER_KEYS // 8)]

        def head(hd, carry):
            q1 = _dot(wqt_ref[2 * hd], h2t).astype(BF16)
            q2 = _dot(wqt_ref[2 * hd + 1], h2t).astype(BF16)
            st1 = _dot(sk_ref[2 * hd], q1) * LOG2E
            st2 = _dot(sk_ref[2 * hd + 1], q2) * LOG2E
            for tl in range(ntl):
                s2_ref[hd, tl] = st2[:, tl * LANES:(tl + 1) * LANES]
            t1 = _top16_desc(split8(st1))
            t2 = _top16_desc(split8(st2))
            a2lo, a2hi = compact(t2, 0), compact(t2, 8)

            def candidates(t1x):
                cands = [t1x[0] + a2lo, t1x[0] + a2hi, t1x[1] + a2lo]
                for r1 in range(2, 8):
                    cands.append(jnp.where(sub < PEER_TOPK // (r1 + 1), t1x[r1] + a2lo, ninf))
                cands.append(compact(t1x, 8) + t2[0])
                return cands

            def kth(cands):
                return _top16_desc(cands + [ninf] * (PEER_TOPK - len(cands)))[PEER_TOPK - 1]

            cands = candidates(t1)
            thr = kth(cands)
            m = t1[0] + t2[0]
            z8 = jnp.zeros((8, tm), F32)
            for c in cands:
                z8 = z8 + jnp.where(c >= thr, jnp.exp2(c - m), 0.0)
            mz = m + jnp.log2(jnp.broadcast_to(jnp.sum(z8, axis=0, keepdims=True), (8, tm))) + 1.0
            s1_ref[hd] = st1 - mz[0:1, :]
            thr_ref[pl.ds(hd, 1), :] = kth(candidates([t - mz for t in t1]))[0:1, :]
            return carry

        lax.fori_loop(0, PEER_HEADS, head, 0)
        acc_ref[...] = jnp.zeros_like(acc_ref)

    i0 = pl.multiple_of(e * nsub, nsub)
    s1rows = [s1_ref[hd, pl.ds(i0, nsub), :] for hd in range(PEER_HEADS)]
    for hd in range(PEER_HEADS):
        s1b_ref[hd] = s1rows[hd]
    gsz = MM_GROUP * PEER_KEYS
    for g in range(nsub // MM_GROUP):
        grows = slice(g * gsz, (g + 1) * gsz)
        a_grp = _dot(u_ref[grows, :], h2t_ref[...])
        for sg in range(MM_GROUP):
            ii = g * MM_GROUP + sg
            rows = slice(ii * PEER_KEYS, (ii + 1) * PEER_KEYS)
            for tl in range(ntl):
                cols = slice(tl * LANES, (tl + 1) * LANES)
                w = jnp.zeros((PEER_KEYS, LANES), F32)
                for hd in range(PEER_HEADS):
                    sm = s1b_ref[hd, ii:ii + 1, cols] + s2_ref[hd, tl]
                    w = w + jnp.where(sm >= thr_ref[hd:hd + 1, cols], jnp.exp2(sm), 0.0)
                a = a_grp[sg * PEER_KEYS:(sg + 1) * PEER_KEYS, cols]
                gl = a * (1.0 + lax.erf(a * (1.0 / math.sqrt(2.0))))
                p_ref[rows, cols] = (gl * w).astype(BF16)
        acc_ref[...] += _dot(vt_ref[:, grows], p_ref[grows, :])

    @pl.when(e == ne - 1)
    def _epilogue():
        o_ref[...] = x1_ref[...] + g2_ref[0] * acc_ref[...].T


def _peer(h2t, x1, g2, wq, sub_keys, u_tab, v_tab, seq, tm, eb):
    d, t = h2t.shape
    ne = u_tab.shape[0] // eb
    half = PEER_KEYS
    wqt = wq.T.reshape(2 * PEER_HEADS, half, d).astype(BF16)
    sk = sub_keys.reshape(2 * PEER_HEADS, PEER_KEYS, half).astype(BF16)
    ub = u_tab.astype(BF16)
    vtb = v_tab.T.astype(BF16)
    hs = (PEER_HEADS, PEER_KEYS, tm)
    return pl.pallas_call(
        functools.partial(_peer_kernel, tm=tm, eb=eb, ne=ne),
        grid=(t // tm, ne),
        in_specs=[pl.BlockSpec((d, tm), lambda i, e: (0, i)),
                  pl.BlockSpec((tm, d), lambda i, e: (i, 0)),
                  pl.BlockSpec((1, 1, d), lambda i, e: ((i * tm) // seq, 0, 0)),
                  pl.BlockSpec((2 * PEER_HEADS, half, d), lambda i, e: (0, 0, 0)),
                  pl.BlockSpec((2 * PEER_HEADS, PEER_KEYS, half), lambda i, e: (0, 0, 0)),
                  pl.BlockSpec((eb, d), lambda i, e: (e, 0)),
                  pl.BlockSpec((d, eb), lambda i, e: (0, e))],
        out_specs=pl.BlockSpec((tm, d), lambda i, e: (i, 0)),
        out_shape=jax.ShapeDtypeStruct((t, d), F32),
        scratch_shapes=[pltpu.VMEM(hs, F32),
                        pltpu.VMEM((PEER_HEADS, tm // LANES, PEER_KEYS, LANES), F32),
                        pltpu.VMEM((PEER_HEADS, tm), F32),
                        pltpu.VMEM((eb, tm), BF16),
                        pltpu.VMEM((d, tm), F32),
                        pltpu.VMEM((PEER_HEADS, eb // PEER_KEYS, tm), F32)],
        compiler_params=pltpu.CompilerParams(
            dimension_semantics=("arbitrary", "arbitrary"), vmem_limit_bytes=VMEM_LIMIT_BYTES),
        name="peer",
    )(h2t, x1, g2, wqt, sk, ub, vtb)


def _tile(n, pref):
    t = min(pref, n)
    assert n % t == 0
    return t


def kernel(x, c, ada_w, ada_b, norm1_g, w_in, pool_w, pool_scale, pool_out_g, q_norm_g, k_norm_g,
           attn_out_g, w_out, norm2_g, peer_wq, peer_subkeys, peer_u, peer_v):
    b, s, d = x.shape
    depth = ada_w.shape[0]
    tm = _tile(s, 512)
    tq = _tile(s, 2 * LANES)
    ck = _tile(tm, 512)
    tp = _tile(s, 512)
    eb = 2048
    for l in range(depth):
        mod = _ada(c, ada_w[l], ada_b[l]).reshape(b, 1, 6 * d)
        sh1, sc1, g1, sh2, sc2, g2 = jnp.split(mod, 6, axis=-1)
        q, k, qi, kid, vt, wt, pool_out = _inproj(
            x, sh1, sc1, norm1_g[l], w_in[l], pool_w[l], pool_scale[l], pool_out_g[l],
            q_norm_g[l], k_norm_g[l], tm, ck)
        attn_out = _attn(q, k, qi, kid, vt, wt, attn_out_g[l], tq, ck)
        x1, h2t = _outproj(pool_out, attn_out, x, w_out[l], g1, sh2, sc2, norm2_g[l], tm)
        x = _peer(h2t, x1.reshape(b * s, d), g2, peer_wq[l], peer_subkeys[l],
                  peer_u[l], peer_v[l], s, tp, eb).reshape(b, s, d)
    return x
```

```python
import functools
import math

import ml_dtypes
import numpy as np

import jax
import jax.numpy as jnp
from jax import lax
from jax.experimental import pallas as pl
from jax.experimental.pallas import tpu as pltpu

F32 = jnp.float32
BF16 = jnp.bfloat16
I32 = jnp.int32

CHUNK = 64
EPS = 1e-6
POOL_WINDOWS = (2, 4, 8, 16)
POOL_GROUP = 128
POOL_WIDTH = 512
ATTN_WIDTH = 512
HEAD_DIM = 64
N_HEADS = 8
IDX_HEADS = 8
IDX_DIM = 64
TOPK_MAX = 256
PEER_HEADS = 8
PEER_KEYS = 128
PEER_TOPK = 16
NEG = -1e30

LANES = 128
V7X_VMEM_BYTES = 64 * 1024 * 1024
VMEM_LIMIT_BYTES = (7 * V7X_VMEM_BYTES) // 8
POOL_HALO = 16
LOG2E = math.log2(math.e)
M_INIT = 0.5 * NEG
N_FEAT = 9

NT_DIMS = (((1,), (1,)), ((), ()))


def _dot(a, b):
    return jnp.dot(a, b, preferred_element_type=F32)


def _dot_nt(a, b):
    return lax.dot_general(a, b, NT_DIMS, preferred_element_type=F32)


def _bf16_split3(c):
    c = np.float32(c)
    p1 = np.float32(c.astype(ml_dtypes.bfloat16))
    r = np.float32(c - p1)
    p2 = np.float32(r.astype(ml_dtypes.bfloat16))
    p3 = np.float32(np.float32(r - p2).astype(ml_dtypes.bfloat16))
    return float(p1), float(p2), float(p3)


_SLOPE_L2E = tuple(float(np.float32(2.0 ** (-8.0 * (h + 1) / N_HEADS) * LOG2E)) for h in range(N_HEADS))


def _ada_kernel(c_ref, w_ref, b_ref, o_ref):
    c = c_ref[...]
    cond = c * jax.nn.sigmoid(c)
    o_ref[...] = jnp.dot(cond, w_ref[...], preferred_element_type=F32,
                         precision=lax.Precision.HIGHEST) + b_ref[...]


def _ada(c, w, b):
    bsz, d = c.shape
    n = w.shape[1]
    tn = 1024
    return pl.pallas_call(
        _ada_kernel,
        grid=(n // tn,),
        in_specs=[pl.BlockSpec((bsz, d), lambda j: (0, 0)),
                  pl.BlockSpec((d, tn), lambda j: (0, j)),
                  pl.BlockSpec((1, tn), lambda j: (0, j))],
        out_specs=pl.BlockSpec((bsz, tn), lambda j: (0, j)),
        out_shape=jax.ShapeDtypeStruct((bsz, n), F32),
        name="ada",
    )(c, w, b.reshape(1, n))


def _group_sumsq(t, ones_bd):
    sq = t * t
    hi = sq.astype(BF16)
    lo = (sq - hi.astype(F32)).astype(BF16)
    return _dot(hi, ones_bd) + _dot(lo, ones_bd)


def _inproj_kernel(x_ref, sh_ref, sc_ref, g_ref, wnat_ref, wki_ref, wvt_ref, wwt_ref, ones_ref,
                   qg_ref, kg_ref, poolw_ref, pscale_ref, pg_ref,
                   q_ref, k_ref, qi_ref, kid_ref, vt_ref, wt_ref, pool_ref,
                   ubuf, *, tm, kc):
    i = pl.program_id(1)
    x = x_ref[0]
    ms = jnp.mean(x * x, axis=-1, keepdims=True)
    h = (x * lax.rsqrt(ms + EPS) * g_ref[...]) * (1.0 + sc_ref[0]) + sh_ref[0]
    hb = h.astype(BF16)

    proj = _dot(hb, wnat_ref[...])
    u = proj[:, 0:512]
    q = proj[:, 512:1024]
    k = proj[:, 1024:1536]
    qi = proj[:, 1536:2048]

    ones_bd = ones_ref[...]
    qn = q * lax.rsqrt(_group_sumsq(q, ones_bd) * (1.0 / HEAD_DIM) + EPS) * qg_ref[...]
    kn = k * lax.rsqrt(_group_sumsq(k, ones_bd) * (1.0 / HEAD_DIM) + EPS) * kg_ref[...]
    q_ref[0] = qn.astype(BF16)
    k_ref[0] = kn.astype(BF16)
    qi_ref[0] = (qi * (IDX_DIM ** -0.5)).astype(BF16)
    kid_ref[0] = _dot(hb, wki_ref[...]).astype(BF16)
    vt = _dot_nt(wvt_ref[...], hb).astype(BF16)
    for ci in range(tm // kc):
        vt_ref[0, ci] = vt[:, ci * kc:(ci + 1) * kc]
    wt = _dot_nt(wwt_ref[...], hb)
    wt_ref[0] = wt[0:IDX_HEADS, :] * (IDX_HEADS ** -0.5)

    @pl.when(i == 0)
    def _():
        ubuf[0:POOL_HALO, :] = jnp.zeros((POOL_HALO, POOL_WIDTH), F32)

    ubuf[POOL_HALO:POOL_HALO + tm, :] = u
    tpos = i * tm + lax.broadcasted_iota(I32, (tm, POOL_GROUP), 0)
    parts = []
    for gi, win in enumerate(POOL_WINDOWS):
        lo_l, hi_l = gi * POOL_GROUP, (gi + 1) * POOL_GROUP
        ws = ubuf[POOL_HALO:POOL_HALO + tm, lo_l:hi_l]
        for j in range(1, win):
            ws = ws + ubuf[POOL_HALO - j:POOL_HALO - j + tm, lo_l:hi_l]
        cnt = jnp.minimum(tpos + 1, win).astype(F32)
        pooled = ws / cnt - u[:, lo_l:hi_l]
        parts.append(_dot(pooled.astype(BF16), poolw_ref[gi]))
    mixed = jnp.concatenate(parts, axis=-1) * pscale_ref[...]
    pms = jnp.mean(mixed * mixed, axis=-1, keepdims=True)
    pool_ref[0] = (mixed * lax.rsqrt(pms + EPS) * pg_ref[...]).astype(BF16)
    ubuf[0:POOL_HALO, :] = ubuf[tm:tm + POOL_HALO, :]


def _inproj(x, sh1, sc1, norm_g, w_in, pool_w, pool_scale, pool_out_g, q_norm_g, k_norm_g, tm, kc):
    b, s, d = x.shape
    nt = s // tm
    w_u, w_q, w_k, w_v, w_qi, w_ki, w_wi = jnp.split(
        w_in, (512, 1024, 1536, 2048, 2560, 2624), axis=-1)
    wnat = jnp.concatenate([w_u, w_q, w_k, w_qi], axis=-1).astype(BF16)
    wki = jnp.concatenate([w_ki, w_ki], axis=-1).astype(BF16)
    wvt = w_v.T.astype(BF16)
    wwt = jnp.concatenate([w_wi.T, jnp.zeros((16 - IDX_HEADS, d), F32)], axis=0).astype(BF16)
    gid = jnp.arange(ATTN_WIDTH) // HEAD_DIM
    ones_bd = (gid[:, None] == gid[None, :]).astype(BF16)
    qg = (jnp.tile(q_norm_g, N_HEADS) * (HEAD_DIM ** -0.5 * LOG2E)).reshape(1, ATTN_WIDTH)
    kg = jnp.tile(k_norm_g, N_HEADS).reshape(1, ATTN_WIDTH)

    full = lambda shape: pl.BlockSpec(shape, lambda bi, ti: (0,) * len(shape))
    tok = lambda w: pl.BlockSpec((1, tm, w), lambda bi, ti: (bi, ti, 0))
    per_b = pl.BlockSpec((1, 1, d), lambda bi, ti: (bi, 0, 0))
    out_shapes = (
        jax.ShapeDtypeStruct((b, s, ATTN_WIDTH), BF16),
        jax.ShapeDtypeStruct((b, s, ATTN_WIDTH), BF16),
        jax.ShapeDtypeStruct((b, s, IDX_HEADS * IDX_DIM), BF16),
        jax.ShapeDtypeStruct((b, s, 2 * IDX_DIM), BF16),
        jax.ShapeDtypeStruct((b, s // kc, ATTN_WIDTH, kc), BF16),
        jax.ShapeDtypeStruct((b, IDX_HEADS, s), F32),
        jax.ShapeDtypeStruct((b, s, POOL_WIDTH), BF16),
    )
    out_specs = (
        tok(ATTN_WIDTH), tok(ATTN_WIDTH), tok(IDX_HEADS * IDX_DIM), tok(2 * IDX_DIM),
        pl.BlockSpec((1, tm // kc, ATTN_WIDTH, kc), lambda bi, ti: (bi, ti, 0, 0)),
        pl.BlockSpec((1, IDX_HEADS, tm), lambda bi, ti: (bi, 0, ti)),
        tok(POOL_WIDTH),
    )
    return pl.pallas_call(
        functools.partial(_inproj_kernel, tm=tm, kc=kc),
        grid=(b, nt),
        in_specs=[tok(d), per_b, per_b, full((1, d)), full((d, 2048)), full((d, 2 * IDX_DIM)),
                  full((ATTN_WIDTH, d)), full((16, d)), full((ATTN_WIDTH, ATTN_WIDTH)),
                  full((1, ATTN_WIDTH)), full((1, ATTN_WIDTH)),
                  full((len(POOL_WINDOWS), POOL_GROUP, POOL_GROUP)),
                  full((1, POOL_WIDTH)), full((1, POOL_WIDTH))],
        out_specs=out_specs,
        out_shape=out_shapes,
        scratch_shapes=[pltpu.VMEM((tm + POOL_HALO, POOL_WIDTH), F32)],
        compiler_params=pltpu.CompilerParams(
            dimension_semantics=("arbitrary", "arbitrary"), vmem_limit_bytes=VMEM_LIMIT_BYTES),
        name="inproj",
    )(x, sh1, sc1, norm_g.reshape(1, d), wnat, wki, wvt, wwt, ones_bd, qg, kg,
      pool_w.astype(BF16), pool_scale.reshape(1, -1), pool_out_g.reshape(1, -1))


def _attn_kernel(q_ref, qi_ref, wt_ref, kid_ref, k_ref, kf_ref, vt_ref, aog_ref, o_ref,
                 key_ref, acc_ref, m_ref, l_ref, qim_ref, qmf_ref, cut_ref, lg_ref,
                 hi_ref, mid_ref, low_ref,
                 *, tq, ck, seq, topk):
    j = pl.program_id(1)
    nck = ((j + 1) * tq + ck - 1) // ck
    qpos = j * tq + lax.broadcasted_iota(I32, (1, tq), 1)
    qchunk = qpos // CHUNK
    row_i = lax.broadcasted_iota(I32, (ck, tq), 0)

    lane = lax.broadcasted_iota(I32, (tq, LANES), 1)
    tcol = (j * tq + lax.broadcasted_iota(I32, (tq, LANES), 0)).astype(F32)
    for h in range(N_HEADS):
        p = h // 2
        keep = (lane < HEAD_DIM) if h % 2 == 0 else (lane >= HEAD_DIM)
        qi_slab = qi_ref[0, :, p * LANES:(p + 1) * LANES]
        q_slab = q_ref[0, :, p * LANES:(p + 1) * LANES]
        qim_ref[h * tq:(h + 1) * tq, :] = jnp.where(keep, qi_slab, jnp.zeros_like(qi_slab))
        qmf_ref[h, :, 0:LANES] = jnp.where(keep, q_slab, jnp.zeros_like(q_slab))
        c1, c2, c3 = _bf16_split3(_SLOPE_L2E[h])
        u = -_SLOPE_L2E[h] * tcol
        u1 = u.astype(BF16).astype(F32)
        r = u - u1
        u2 = r.astype(BF16).astype(F32)
        u3 = (r - u2).astype(BF16).astype(F32)
        feat = jnp.zeros((tq, LANES), F32)
        for li, val in enumerate((64.0 * c1, 64.0 * c2, 64.0 * c3, c1, c2, c3, u1, u2, u3)):
            feat = jnp.where(lane == li, val, feat)
        qmf_ref[h, :, LANES:2 * LANES] = feat.astype(BF16)

    def scores(c, last):
        r0 = pl.multiple_of(c * ck, ck)
        d = _dot_nt(kid_ref[0, pl.ds(r0, ck), :], qim_ref[...])
        acc = jnp.zeros((ck, tq), F32)
        for h in range(IDX_HEADS):
            acc = acc + wt_ref[0, h:h + 1, :] * jnp.maximum(d[:, h * tq:(h + 1) * tq], 0.0)
        if last:
            acc = jnp.where(((r0 + row_i) // CHUNK) <= qchunk, acc, NEG)
        bits = pltpu.bitcast(acc, I32)
        key = jnp.where(bits < 0, bits ^ 0x7FFFFFFF, bits)
        key_ref[pl.ds(r0, ck), :] = key
        hi_ref[pl.ds(r0, ck), :] = pltpu.bitcast(bits & -65536, F32).astype(BF16)
        mid_ref[pl.ds(r0, ck), :] = ((key >> 8) & 0xFF).astype(F32).astype(BF16)
        low_ref[pl.ds(r0, ck), :] = (key & 0xFF).astype(F32).astype(BF16)

    def scores_body(c, carry):
        scores(c, False)
        return carry

    lax.fori_loop(0, nck - 1, scores_body, 0)
    scores(nck - 1, True)

    npair = (nck + 1) // 2

    @pl.when(nck % 2 == 1)
    def _():
        r0 = pl.multiple_of(nck * ck, ck)
        neg = jnp.full((ck, tq), NEG, F32)
        bits = pltpu.bitcast(neg, I32)
        key = bits ^ 0x7FFFFFFF
        key_ref[pl.ds(r0, ck), :] = key
        hi_ref[pl.ds(r0, ck), :] = pltpu.bitcast(bits & -65536, F32).astype(BF16)
        mid_ref[pl.ds(r0, ck), :] = ((key >> 8) & 0xFF).astype(F32).astype(BF16)
        low_ref[pl.ds(r0, ck), :] = (key & 0xFF).astype(F32).astype(BF16)

    def sum_chunks(chunk_fn, init):
        def body(i, acc):
            r0 = pl.multiple_of(i * (2 * ck), 2 * ck)
            return acc + chunk_fn(r0) + chunk_fn(pl.multiple_of(r0 + ck, ck))
        return lax.fori_loop(0, npair, body, init)

    def count(pred):
        def chunk(r0):
            m = pred(key_ref[pl.ds(r0, ck), :], r0).astype(I32)
            return m.reshape(ck // 8, 8, tq).sum(axis=0)
        return sum_chunks(chunk, jnp.zeros((8, tq), I32)).sum(axis=0, keepdims=True)

    def count_packed(ref, tb):
        def chunk(r0):
            m = ref[pl.ds(r0, ck), :] >= tb
            ones = jnp.where(m, jnp.ones((ck, tq), BF16), jnp.zeros((ck, tq), BF16))
            parts = [ones[i * 16:(i + 1) * 16, :] for i in range(ck // 16)]
            while len(parts) > 1:
                parts = [parts[i] + parts[i + 1] for i in range(0, len(parts), 2)]
            return parts[0].astype(F32)

        cacc = sum_chunks(chunk, jnp.zeros((16, tq), F32))
        return cacc.sum(axis=0, keepdims=True).astype(I32)

    def bf16_of_key16(k16):
        pat = jnp.where(k16 < 0, k16 ^ 0x7FFF, k16)
        return pltpu.bitcast(jnp.left_shift(pat, 16), F32).astype(BF16)

    def count_hi(trial16):
        return count_packed(hi_ref, bf16_of_key16(trial16))

    c_nonneg = count_hi(jnp.zeros((1, tq), I32))
    ok0 = c_nonneg >= topk
    state0 = (jnp.where(ok0, 0, -(2 ** 15)).astype(I32),
              jnp.where(ok0, c_nonneg, 2 * npair * ck), jnp.where(ok0, 0, c_nonneg))

    def hi_body(b, state):
        lo16, c_ge, c_gt = state
        trial = lo16 + jnp.left_shift(jnp.int32(1), 14 - b)
        c = count_hi(trial)
        ok = c >= topk
        return jnp.where(ok, trial, lo16), jnp.where(ok, c, c_ge), jnp.where(ok, c_gt, c)

    lo16, c_ge, c_gt = lax.fori_loop(0, 15, hi_body, state0)

    def resolve_byte(byte_ref, above_ref, above_val, c_ge, c_gt):
        def rewrite(c, carry):
            r0 = pl.multiple_of(c * ck, ck)
            above = above_ref[pl.ds(r0, ck), :]
            byte_ref[pl.ds(r0, ck), :] = jnp.where(
                above > above_val, jnp.full((ck, tq), 256.0, BF16),
                jnp.where(above < above_val, jnp.full((ck, tq), -1.0, BF16),
                          byte_ref[pl.ds(r0, ck), :]))
            return carry

        lax.fori_loop(0, 2 * npair, rewrite, 0)

        def byte_body(b, state):
            val, c_ge, c_gt = state
            trial = val + jnp.left_shift(jnp.int32(1), 7 - b)
            c = count_packed(byte_ref, trial.astype(F32).astype(BF16))
            ok = c >= topk
            return jnp.where(ok, trial, val), jnp.where(ok, c, c_ge), jnp.where(ok, c_gt, c)

        return lax.fori_loop(0, 8, byte_body, (jnp.zeros((1, tq), I32), c_ge, c_gt))

    b1, c_ge, c_gt = resolve_byte(mid_ref, hi_ref, bf16_of_key16(lo16), c_ge, c_gt)
    b0, c_ge, c_gt = resolve_byte(low_ref, mid_ref, b1.astype(F32).astype(BF16), c_ge, c_gt)
    lo = jnp.left_shift(lo16, 16) + jnp.left_shift(b1, 8) + b0

    need = topk - c_gt
    cut_ref[...] = jnp.full((1, tq), seq, I32)

    @pl.when(jnp.max(c_ge) > topk)
    def _():
        nbits = max(1, (seq - 1).bit_length())

        def idx_body(b, jv):
            trial = jv + jnp.left_shift(jnp.int32(1), nbits - 1 - b)
            c = count(lambda kk, r0: (kk == lo)
                      & ((r0 + lax.broadcasted_iota(I32, (ck, tq), 0)) < trial))
            return jnp.where(c < need, trial, jv)

        jv = lax.fori_loop(0, nbits, idx_body, jnp.zeros((1, tq), I32))
        cut_ref[...] = jv + 1

    cut = cut_ref[...]

    m_ref[...] = jnp.full((N_HEADS, 1, tq), M_INIT, F32)
    l_ref[...] = jnp.zeros((N_HEADS, 1, tq), F32)
    acc_ref[...] = jnp.zeros((ATTN_WIDTH, tq), F32)

    def attend(c, last):
        r0 = pl.multiple_of(c * ck, ck)
        kk = key_ref[pl.ds(r0, ck), :]
        kf = kf_ref[pl.ds(r0, ck), :]
        kch = k_ref[0, pl.ds(r0, ck), :]
        vtc = vt_ref[0, c]
        kpos = r0 + row_i
        sel = (kk > lo) | ((kk == lo) & (kpos < cut))
        if last:
            sel = sel & ((kpos // CHUNK) <= qchunk)
            ahead = jnp.maximum(kpos - qpos, 0).astype(F32)
        pen = jnp.where(sel, 0.0, NEG)
        for h in range(N_HEADS):
            p = h // 2
            ks = kch[:, p * LANES:(p + 1) * LANES]
            lg = _dot_nt(jnp.concatenate([ks, kf], axis=1), qmf_ref[h]) + pen
            if last:
                lg = lg - (2.0 * _SLOPE_L2E[h]) * ahead
            lg_ref[h] = lg
        for h in range(N_HEADS):
            m_old = m_ref[h]
            m_new = jnp.maximum(m_old, jnp.max(lg_ref[h], axis=0, keepdims=True))
            alpha = jnp.exp2(m_old - m_new)
            pr = jnp.exp2(lg_ref[h] - m_new)
            l_ref[h] = alpha * l_ref[h] + jnp.sum(pr, axis=0, keepdims=True)
            m_ref[h] = m_new
            rows = slice(h * HEAD_DIM, (h + 1) * HEAD_DIM)
            acc_ref[rows, :] = alpha * acc_ref[rows, :] + _dot(vtc[rows, :], pr.astype(BF16))

    def attend_body(c, carry):
        attend(c, False)
        return carry

    lax.fori_loop(0, nck - 1, attend_body, 0)
    attend(nck - 1, True)

    outs = []
    for h in range(N_HEADS):
        rows = slice(h * HEAD_DIM, (h + 1) * HEAD_DIM)
        o = acc_ref[rows, :] / l_ref[h]
        oms = jnp.mean(o * o, axis=0, keepdims=True)
        outs.append(o * lax.rsqrt(oms + EPS))
    ot = jnp.concatenate(outs, axis=0)
    o_ref[0] = (ot.T * aog_ref[...]).astype(BF16)


def _attn(q, k, qi, kid, vt, wt, attn_out_g, tq, ck):
    b, s, _ = q.shape
    topk = min(TOPK_MAX, s // 4)
    assert tq % LANES == 0 and ck % tq == 0 and s % (2 * ck) == 0 and s // CHUNK <= 256
    pos = jnp.arange(s)
    feats = [pos // CHUNK] * 3 + [pos % CHUNK] * 3 + [jnp.ones_like(pos)] * 3
    kf = jnp.zeros((s, LANES), F32).at[:, 0:N_FEAT].set(
        jnp.stack(feats, axis=1).astype(F32)).astype(BF16)
    blk = lambda w: pl.BlockSpec((1, tq, w), lambda bi, ji: (bi, ji, 0))
    once = pl.Buffered(1)
    whole = lambda w: pl.BlockSpec((1, s, w), lambda bi, ji: (bi, 0, 0), pipeline_mode=once)
    return pl.pallas_call(
        functools.partial(_attn_kernel, tq=tq, ck=ck, seq=s, topk=topk),
        grid=(b, s // tq),
        in_specs=[blk(ATTN_WIDTH), blk(IDX_HEADS * IDX_DIM),
                  pl.BlockSpec((1, IDX_HEADS, tq), lambda bi, ji: (bi, 0, ji)),
                  whole(2 * IDX_DIM), whole(ATTN_WIDTH),
                  pl.BlockSpec((s, LANES), lambda bi, ji: (0, 0), pipeline_mode=once),
                  pl.BlockSpec((1, s // ck, ATTN_WIDTH, ck), lambda bi, ji: (bi, 0, 0, 0),
                               pipeline_mode=once),
                  pl.BlockSpec((1, ATTN_WIDTH), lambda bi, ji: (0, 0))],
        out_specs=blk(ATTN_WIDTH),
        out_shape=jax.ShapeDtypeStruct((b, s, ATTN_WIDTH), BF16),
        scratch_shapes=[pltpu.VMEM((s, tq), I32),
                        pltpu.VMEM((ATTN_WIDTH, tq), F32),
                        pltpu.VMEM((N_HEADS, 1, tq), F32),
                        pltpu.VMEM((N_HEADS, 1, tq), F32),
                        pltpu.VMEM((IDX_HEADS * tq, LANES), BF16),
                        pltpu.VMEM((N_HEADS, tq, 2 * LANES), BF16),
                        pltpu.VMEM((1, tq), I32),
                        pltpu.VMEM((N_HEADS, ck, tq), F32),
                        pltpu.VMEM((s, tq), BF16), pltpu.VMEM((s, tq), BF16),
                        pltpu.VMEM((s, tq), BF16)],
        compiler_params=pltpu.CompilerParams(
            dimension_semantics=("arbitrary", "arbitrary"), vmem_limit_bytes=VMEM_LIMIT_BYTES),
        name="attn",
    )(q, qi, wt, kid, k, kf, vt, attn_out_g.reshape(1, ATTN_WIDTH))


def _outproj_kernel(pool_ref, attn_ref, x_ref, wo_ref, g1_ref, sh_ref, sc_ref, ng_ref,
                    x1_ref, h2t_ref):
    mixed = _dot(pool_ref[0], wo_ref[0:POOL_WIDTH, :]) + _dot(attn_ref[0], wo_ref[POOL_WIDTH:, :])
    x1 = x_ref[0] + g1_ref[0] * mixed
    x1_ref[0] = x1
    ms = jnp.mean(x1 * x1, axis=-1, keepdims=True)
    h2 = (x1 * lax.rsqrt(ms + EPS) * ng_ref[...]) * (1.0 + sc_ref[0]) + sh_ref[0]
    h2t_ref[...] = h2.T.astype(BF16)


def _outproj(pool_out, attn_out, x, w_out, g1, sh2, sc2, norm_g, tm):
    b, s, d = x.shape
    nt = s // tm
    tok = lambda w: pl.BlockSpec((1, tm, w), lambda bi, ti: (bi, ti, 0))
    per_b = pl.BlockSpec((1, 1, d), lambda bi, ti: (bi, 0, 0))
    return pl.pallas_call(
        _outproj_kernel,
        grid=(b, nt),
        in_specs=[tok(POOL_WIDTH), tok(ATTN_WIDTH), tok(d),
                  pl.BlockSpec((POOL_WIDTH + ATTN_WIDTH, d), lambda bi, ti: (0, 0)),
                  per_b, per_b, per_b, pl.BlockSpec((1, d), lambda bi, ti: (0, 0))],
        out_specs=(tok(d), pl.BlockSpec((d, tm), lambda bi, ti: (0, bi * nt + ti))),
        out_shape=(jax.ShapeDtypeStruct((b, s, d), F32), jax.ShapeDtypeStruct((d, b * s), BF16)),
        compiler_params=pltpu.CompilerParams(dimension_semantics=("arbitrary", "arbitrary")),
        name="outproj",
    )(pool_out, attn_out, x, w_out.astype(BF16), g1, sh2, sc2, norm_g.reshape(1, d))


def _batcher_pairs(n):
    pairs = []

    def merge(lo, cnt, r):
        step = r * 2
        if step < cnt:
            merge(lo, cnt, step)
            merge(lo + r, cnt, step)
            for i in range(lo + r, lo + cnt - r, step):
                pairs.append((i, i + r))
        else:
            pairs.append((lo, lo + r))

    def sort(lo, cnt):
        if cnt > 1:
            m = cnt // 2
            sort(lo, m)
            sort(lo + m, m)
            merge(lo, cnt, 1)

    sort(0, n)
    return tuple(pairs)


_SORT16 = _batcher_pairs(PEER_TOPK)
MM_GROUP = 8


def _top16_desc(x):
    x = list(x)
    for i, j in _SORT16:
        x[i], x[j] = jnp.maximum(x[i], x[j]), jnp.minimum(x[i], x[j])
    for shift in (4, 2, 1):
        y = [jnp.maximum(x[k], pltpu.roll(x[PEER_TOPK - 1 - k], shift, 0)) for k in range(PEER_TOPK)]
        for d in (8, 4, 2, 1):
            for k in range(PEER_TOPK):
                if k & d == 0:
                    y[k], y[k + d] = jnp.maximum(y[k], y[k + d]), jnp.minimum(y[k], y[k + d])
        x = y
    return x


def _peer_kernel(h2t_ref, x1_ref, g2_ref, wqt_ref, sk_ref, u_ref, vt_ref, o_ref,
                 s1_ref, s2_ref, thr_ref, p_ref, acc_ref, s1b_ref, *, tm, eb, ne):
    e = pl.program_id(1)
    nsub = eb // PEER_KEYS
    ntl = tm // LANES

    @pl.when(e == 0)
    def _prologue():
        h2t = h2t_ref[...]
        sub = lax.broadcasted_iota(I32, (8, tm), 0)
        ninf = jnp.full((8, tm), -jnp.inf, F32)

        def compact(t, base):
            out = t[base + 7]
            for r in range(6, -1, -1):
                out = jnp.where(sub == r, t[base + r], out)
            return out

        def split8(st):
            return [st[k * 8:(k + 1) * 8, :] for k in range(PEER_KEYS // 8)]

        def head(hd, carry):
            q1 = _dot(wqt_ref[2 * hd], h2t).astype(BF16)
            q2 = _dot(wqt_ref[2 * hd + 1], h2t).astype(BF16)
            st1 = _dot(sk_ref[2 * hd], q1) * LOG2E
            st2 = _dot(sk_ref[2 * hd + 1], q2) * LOG2E
            for tl in range(ntl):
                s2_ref[hd, tl] = st2[:, tl * LANES:(tl + 1) * LANES]
            t1 = _top16_desc(split8(st1))
            t2 = _top16_desc(split8(st2))
            a2lo, a2hi = compact(t2, 0), compact(t2, 8)

            def candidates(t1x):
                cands = [t1x[0] + a2lo, t1x[0] + a2hi, t1x[1] + a2lo]
                for r1 in range(2, 8):
                    cands.append(jnp.where(sub < PEER_TOPK // (r1 + 1), t1x[r1] + a2lo, ninf))
                cands.append(compact(t1x, 8) + t2[0])
                return cands

            def kth(cands):
                return _top16_desc(cands + [ninf] * (PEER_TOPK - len(cands)))[PEER_TOPK - 1]

            cands = candidates(t1)
            thr = kth(cands)
            m = t1[0] + t2[0]
            z8 = jnp.zeros((8, tm), F32)
            for c in cands:
                z8 = z8 + jnp.where(c >= thr, jnp.exp2(c - m), 0.0)
            mz = m + jnp.log2(jnp.broadcast_to(jnp.sum(z8, axis=0, keepdims=True), (8, tm))) + 1.0
            s1_ref[hd] = st1 - mz[0:1, :]
            thr_ref[pl.ds(hd, 1), :] = kth(candidates([t - mz for t in t1]))[0:1, :]
            return carry

        lax.fori_loop(0, PEER_HEADS, head, 0)
        acc_ref[...] = jnp.zeros_like(acc_ref)

    i0 = pl.multiple_of(e * nsub, nsub)
    s1rows = [s1_ref[hd, pl.ds(i0, nsub), :] for hd in range(PEER_HEADS)]
    for hd in range(PEER_HEADS):
        s1b_ref[hd] = s1rows[hd]
    gsz = MM_GROUP * PEER_KEYS
    for g in range(nsub // MM_GROUP):
        grows = slice(g * gsz, (g + 1) * gsz)
        a_grp = _dot(u_ref[grows, :], h2t_ref[...])
        for sg in range(MM_GROUP):
            ii = g * MM_GROUP + sg
            rows = slice(ii * PEER_KEYS, (ii + 1) * PEER_KEYS)
            for tl in range(ntl):
                cols = slice(tl * LANES, (tl + 1) * LANES)
                w = jnp.zeros((PEER_KEYS, LANES), F32)
                for hd in range(PEER_HEADS):
                    sm = s1b_ref[hd, ii:ii + 1, cols] + s2_ref[hd, tl]
                    w = w + jnp.where(sm >= thr_ref[hd:hd + 1, cols], jnp.exp2(sm), 0.0)
                a = a_grp[sg * PEER_KEYS:(sg + 1) * PEER_KEYS, cols]
                gl = a * (1.0 + lax.erf(a * (1.0 / math.sqrt(2.0))))
                p_ref[rows, cols] = (gl * w).astype(BF16)
        acc_ref[...] += _dot(vt_ref[:, grows], p_ref[grows, :])

    @pl.when(e == ne - 1)
    def _epilogue():
        o_ref[...] = x1_ref[...] + g2_ref[0] * acc_ref[...].T


def _peer(h2t, x1, g2, wq, sub_keys, u_tab, v_tab, seq, tm, eb):
    d, t = h2t.shape
    ne = u_tab.shape[0] // eb
    half = PEER_KEYS
    wqt = wq.T.reshape(2 * PEER_HEADS, half, d).astype(BF16)
    sk = sub_keys.reshape(2 * PEER_HEADS, PEER_KEYS, half).astype(BF16)
    ub = u_tab.astype(BF16)
    vtb = v_tab.T.astype(BF16)
    hs = (PEER_HEADS, PEER_KEYS, tm)
    return pl.pallas_call(
        functools.partial(_peer_kernel, tm=tm, eb=eb, ne=ne),
        grid=(t // tm, ne),
        in_specs=[pl.BlockSpec((d, tm), lambda i, e: (0, i)),
                  pl.BlockSpec((tm, d), lambda i, e: (i, 0)),
                  pl.BlockSpec((1, 1, d), lambda i, e: ((i * tm) // seq, 0, 0)),
                  pl.BlockSpec((2 * PEER_HEADS, half, d), lambda i, e: (0, 0, 0)),
                  pl.BlockSpec((2 * PEER_HEADS, PEER_KEYS, half), lambda i, e: (0, 0, 0)),
                  pl.BlockSpec((eb, d), lambda i, e: (e, 0)),
                  pl.BlockSpec((d, eb), lambda i, e: (0, e))],
        out_specs=pl.BlockSpec((tm, d), lambda i, e: (i, 0)),
        out_shape=jax.ShapeDtypeStruct((t, d), F32),
        scratch_shapes=[pltpu.VMEM(hs, F32),
                        pltpu.VMEM((PEER_HEADS, tm // LANES, PEER_KEYS, LANES), F32),
                        pltpu.VMEM((PEER_HEADS, tm), F32),
                        pltpu.VMEM((eb, tm), BF16),
                        pltpu.VMEM((d, tm), F32),
                        pltpu.VMEM((PEER_HEADS, eb // PEER_KEYS, tm), F32)],
        compiler_params=pltpu.CompilerParams(
            dimension_semantics=("arbitrary", "arbitrary"), vmem_limit_bytes=VMEM_LIMIT_BYTES),
        name="peer",
    )(h2t, x1, g2, wqt, sk, ub, vtb)


def _tile(n, pref):
    t = min(pref, n)
    assert n % t == 0
    return t


def kernel(x, c, ada_w, ada_b, norm1_g, w_in, pool_w, pool_scale, pool_out_g, q_norm_g, k_norm_g,
           attn_out_g, w_out, norm2_g, peer_wq, peer_subkeys, peer_u, peer_v):
    b, s, d = x.shape
    depth = ada_w.shape[0]
    tm = _tile(s, 512)
    tq = _tile(s, 2 * LANES)
    ck = _tile(tm, 512)
    tp = _tile(s, 512)
    eb = 2048
    for l in range(depth):
        mod = _ada(c, ada_w[l], ada_b[l]).reshape(b, 1, 6 * d)
        sh1, sc1, g1, sh2, sc2, g2 = jnp.split(mod, 6, axis=-1)
        q, k, qi, kid, vt, wt, pool_out = _inproj(
            x, sh1, sc1, norm1_g[l], w_in[l], pool_w[l], pool_scale[l], pool_out_g[l],
            q_norm_g[l], k_norm_g[l], tm, ck)
        attn_out = _attn(q, k, qi, kid, vt, wt, attn_out_g[l], tq, ck)
        x1, h2t = _outproj(pool_out, attn_out, x, w_out[l], g1, sh2, sc2, norm2_g[l], tm)
        x = _peer(h2t, x1.reshape(b * s, d), g2, peer_wq[l], peer_subkeys[l],
                  peer_u[l], peer_v[l], s, tp, eb).reshape(b, s, d)
    return x
```

```python
import functools
import math

import ml_dtypes
import numpy as np

import jax
import jax.numpy as jnp
from jax import lax
from jax.experimental import pallas as pl
from jax.experimental.pallas import tpu as pltpu

F32 = jnp.float32
BF16 = jnp.bfloat16
I32 = jnp.int32

CHUNK = 64
EPS = 1e-6
POOL_WINDOWS = (2, 4, 8, 16)
POOL_GROUP = 128
POOL_WIDTH = 512
ATTN_WIDTH = 512
HEAD_DIM = 64
N_HEADS = 8
IDX_HEADS = 8
IDX_DIM = 64
TOPK_MAX = 256
PEER_HEADS = 8
PEER_KEYS = 128
PEER_TOPK = 16
NEG = -1e30

LANES = 128
V7X_VMEM_BYTES = 64 * 1024 * 1024
VMEM_LIMIT_BYTES = (7 * V7X_VMEM_BYTES) // 8
POOL_HALO = 16
LOG2E = math.log2(math.e)
M_INIT = 0.5 * NEG
N_FEAT = 9

NT_DIMS = (((1,), (1,)), ((), ()))


def _dot(a, b):
    return jnp.dot(a, b, preferred_element_type=F32)


def _dot_nt(a, b):
    return lax.dot_general(a, b, NT_DIMS, preferred_element_type=F32)


def _bf16_split3(c):
    c = np.float32(c)
    p1 = np.float32(c.astype(ml_dtypes.bfloat16))
    r = np.float32(c - p1)
    p2 = np.float32(r.astype(ml_dtypes.bfloat16))
    p3 = np.float32(np.float32(r - p2).astype(ml_dtypes.bfloat16))
    return float(p1), float(p2), float(p3)


_SLOPE_L2E = tuple(float(np.float32(2.0 ** (-8.0 * (h + 1) / N_HEADS) * LOG2E)) for h in range(N_HEADS))


def _ada_kernel(c_ref, w_ref, b_ref, o_ref):
    c = c_ref[...]
    cond = c * jax.nn.sigmoid(c)
    o_ref[...] = jnp.dot(cond, w_ref[...], preferred_element_type=F32,
                         precision=lax.Precision.HIGHEST) + b_ref[...]


def _ada(c, w, b):
    bsz, d = c.shape
    n = w.shape[1]
    tn = 1024
    return pl.pallas_call(
        _ada_kernel,
        grid=(n // tn,),
        in_specs=[pl.BlockSpec((bsz, d), lambda j: (0, 0)),
                  pl.BlockSpec((d, tn), lambda j: (0, j)),
                  pl.BlockSpec((1, tn), lambda j: (0, j))],
        out_specs=pl.BlockSpec((bsz, tn), lambda j: (0, j)),
        out_shape=jax.ShapeDtypeStruct((bsz, n), F32),
        name="ada",
    )(c, w, b.reshape(1, n))


def _group_sumsq(t, ones_bd):
    sq = t * t
    hi = sq.astype(BF16)
    lo = (sq - hi.astype(F32)).astype(BF16)
    return _dot(hi, ones_bd) + _dot(lo, ones_bd)


def _inproj_kernel(x_ref, sh_ref, sc_ref, g_ref, wnat_ref, wki_ref, wvt_ref, wwt_ref, ones_ref,
                   qg_ref, kg_ref, poolw_ref, pscale_ref, pg_ref,
                   q_ref, k_ref, qi_ref, kid_ref, vt_ref, wt_ref, pool_ref,
                   ubuf, *, tm, kc):
    i = pl.program_id(1)
    x = x_ref[0]
    ms = jnp.mean(x * x, axis=-1, keepdims=True)
    h = (x * lax.rsqrt(ms + EPS) * g_ref[...]) * (1.0 + sc_ref[0]) + sh_ref[0]
    hb = h.astype(BF16)

    proj = _dot(hb, wnat_ref[...])
    u = proj[:, 0:512]
    q = proj[:, 512:1024]
    k = proj[:, 1024:1536]
    qi = proj[:, 1536:2048]

    ones_bd = ones_ref[...]
    qn = q * lax.rsqrt(_group_sumsq(q, ones_bd) * (1.0 / HEAD_DIM) + EPS) * qg_ref[...]
    kn = k * lax.rsqrt(_group_sumsq(k, ones_bd) * (1.0 / HEAD_DIM) + EPS) * kg_ref[...]
    q_ref[0] = qn.astype(BF16)
    k_ref[0] = kn.astype(BF16)
    qi_ref[0] = (qi * (IDX_DIM ** -0.5)).astype(BF16)
    kid_ref[0] = _dot(hb, wki_ref[...]).astype(BF16)
    vt = _dot_nt(wvt_ref[...], hb).astype(BF16)
    for ci in range(tm // kc):
        vt_ref[0, ci] = vt[:, ci * kc:(ci + 1) * kc]
    wt = _dot_nt(wwt_ref[...], hb)
    wt_ref[0] = wt[0:IDX_HEADS, :] * (IDX_HEADS ** -0.5)

    @pl.when(i == 0)
    def _():
        ubuf[0:POOL_HALO, :] = jnp.zeros((POOL_HALO, POOL_WIDTH), F32)

    ubuf[POOL_HALO:POOL_HALO + tm, :] = u
    tpos = i * tm + lax.broadcasted_iota(I32, (tm, POOL_GROUP), 0)
    parts = []
    for gi, win in enumerate(POOL_WINDOWS):
        lo_l, hi_l = gi * POOL_GROUP, (gi + 1) * POOL_GROUP
        ws = ubuf[POOL_HALO:POOL_HALO + tm, lo_l:hi_l]
        for j in range(1, win):
            ws = ws + ubuf[POOL_HALO - j:POOL_HALO - j + tm, lo_l:hi_l]
        cnt = jnp.minimum(tpos + 1, win).astype(F32)
        pooled = ws / cnt - u[:, lo_l:hi_l]
        parts.append(_dot(pooled.astype(BF16), poolw_ref[gi]))
    mixed = jnp.concatenate(parts, axis=-1) * pscale_ref[...]
    pms = jnp.mean(mixed * mixed, axis=-1, keepdims=True)
    pool_ref[0] = (mixed * lax.rsqrt(pms + EPS) * pg_ref[...]).astype(BF16)
    ubuf[0:POOL_HALO, :] = ubuf[tm:tm + POOL_HALO, :]


def _inproj(x, sh1, sc1, norm_g, w_in, pool_w, pool_scale, pool_out_g, q_norm_g, k_norm_g, tm, kc):
    b, s, d = x.shape
    nt = s // tm
    w_u, w_q, w_k, w_v, w_qi, w_ki, w_wi = jnp.split(
        w_in, (512, 1024, 1536, 2048, 2560, 2624), axis=-1)
    wnat = jnp.concatenate([w_u, w_q, w_k, w_qi], axis=-1).astype(BF16)
    wki = jnp.concatenate([w_ki, w_ki], axis=-1).astype(BF16)
    wvt = w_v.T.astype(BF16)
    wwt = jnp.concatenate([w_wi.T, jnp.zeros((16 - IDX_HEADS, d), F32)], axis=0).astype(BF16)
    gid = jnp.arange(ATTN_WIDTH) // HEAD_DIM
    ones_bd = (gid[:, None] == gid[None, :]).astype(BF16)
    qg = (jnp.tile(q_norm_g, N_HEADS) * (HEAD_DIM ** -0.5 * LOG2E)).reshape(1, ATTN_WIDTH)
    kg = jnp.tile(k_norm_g, N_HEADS).reshape(1, ATTN_WIDTH)

    full = lambda shape: pl.BlockSpec(shape, lambda bi, ti: (0,) * len(shape))
    tok = lambda w: pl.BlockSpec((1, tm, w), lambda bi, ti: (bi, ti, 0))
    per_b = pl.BlockSpec((1, 1, d), lambda bi, ti: (bi, 0, 0))
    out_shapes = (
        jax.ShapeDtypeStruct((b, s, ATTN_WIDTH), BF16),
        jax.ShapeDtypeStruct((b, s, ATTN_WIDTH), BF16),
        jax.ShapeDtypeStruct((b, s, IDX_HEADS * IDX_DIM), BF16),
        jax.ShapeDtypeStruct((b, s, 2 * IDX_DIM), BF16),
        jax.ShapeDtypeStruct((b, s // kc, ATTN_WIDTH, kc), BF16),
        jax.ShapeDtypeStruct((b, IDX_HEADS, s), F32),
        jax.ShapeDtypeStruct((b, s, POOL_WIDTH), BF16),
    )
    out_specs = (
        tok(ATTN_WIDTH), tok(ATTN_WIDTH), tok(IDX_HEADS * IDX_DIM), tok(2 * IDX_DIM),
        pl.BlockSpec((1, tm // kc, ATTN_WIDTH, kc), lambda bi, ti: (bi, ti, 0, 0)),
        pl.BlockSpec((1, IDX_HEADS, tm), lambda bi, ti: (bi, 0, ti)),
        tok(POOL_WIDTH),
    )
    return pl.pallas_call(
        functools.partial(_inproj_kernel, tm=tm, kc=kc),
        grid=(b, nt),
        in_specs=[tok(d), per_b, per_b, full((1, d)), full((d, 2048)), full((d, 2 * IDX_DIM)),
                  full((ATTN_WIDTH, d)), full((16, d)), full((ATTN_WIDTH, ATTN_WIDTH)),
                  full((1, ATTN_WIDTH)), full((1, ATTN_WIDTH)),
                  full((len(POOL_WINDOWS), POOL_GROUP, POOL_GROUP)),
                  full((1, POOL_WIDTH)), full((1, POOL_WIDTH))],
        out_specs=out_specs,
        out_shape=out_shapes,
        scratch_shapes=[pltpu.VMEM((tm + POOL_HALO, POOL_WIDTH), F32)],
        compiler_params=pltpu.CompilerParams(
            dimension_semantics=("arbitrary", "arbitrary"), vmem_limit_bytes=VMEM_LIMIT_BYTES),
        name="inproj",
    )(x, sh1, sc1, norm_g.reshape(1, d), wnat, wki, wvt, wwt, ones_bd, qg, kg,
      pool_w.astype(BF16), pool_scale.reshape(1, -1), pool_out_g.reshape(1, -1))


def _attn_kernel(q_ref, qi_ref, wt_ref, kid_ref, k_ref, kf_ref, vt_ref, aog_ref, o_ref,
                 key_ref, acc_ref, m_ref, l_ref, qim_ref, qmf_ref, cut_ref, lg_ref,
                 hi_ref, mid_ref, low_ref,
                 *, tq, ck, seq, topk):
    j = pl.program_id(1)
    nck = ((j + 1) * tq + ck - 1) // ck
    qpos = j * tq + lax.broadcasted_iota(I32, (1, tq), 1)
    qchunk = qpos // CHUNK
    row_i = lax.broadcasted_iota(I32, (ck, tq), 0)

    lane = lax.broadcasted_iota(I32, (tq, LANES), 1)
    tcol = (j * tq + lax.broadcasted_iota(I32, (tq, LANES), 0)).astype(F32)
    for h in range(N_HEADS):
        p = h // 2
        keep = (lane < HEAD_DIM) if h % 2 == 0 else (lane >= HEAD_DIM)
        qi_slab = qi_ref[0, :, p * LANES:(p + 1) * LANES]
        q_slab = q_ref[0, :, p * LANES:(p + 1) * LANES]
        qim_ref[h * tq:(h + 1) * tq, :] = jnp.where(keep, qi_slab, jnp.zeros_like(qi_slab))
        qmf_ref[h, :, 0:LANES] = jnp.where(keep, q_slab, jnp.zeros_like(q_slab))
        c1, c2, c3 = _bf16_split3(_SLOPE_L2E[h])
        u = -_SLOPE_L2E[h] * tcol
        u1 = u.astype(BF16).astype(F32)
        r = u - u1
        u2 = r.astype(BF16).astype(F32)
        u3 = (r - u2).astype(BF16).astype(F32)
        feat = jnp.zeros((tq, LANES), F32)
        for li, val in enumerate((64.0 * c1, 64.0 * c2, 64.0 * c3, c1, c2, c3, u1, u2, u3)):
            feat = jnp.where(lane == li, val, feat)
        qmf_ref[h, :, LANES:2 * LANES] = feat.astype(BF16)

    def scores(c, last):
        r0 = pl.multiple_of(c * ck, ck)
        d = _dot_nt(kid_ref[0, pl.ds(r0, ck), :], qim_ref[...])
        acc = jnp.zeros((ck, tq), F32)
        for h in range(IDX_HEADS):
            acc = acc + wt_ref[0, h:h + 1, :] * jnp.maximum(d[:, h * tq:(h + 1) * tq], 0.0)
        if last:
            acc = jnp.where(((r0 + row_i) // CHUNK) <= qchunk, acc, NEG)
        bits = pltpu.bitcast(acc, I32)
        key = jnp.where(bits < 0, bits ^ 0x7FFFFFFF, bits)
        key_ref[pl.ds(r0, ck), :] = key
        hi_ref[pl.ds(r0, ck), :] = pltpu.bitcast(bits & -65536, F32).astype(BF16)
        mid_ref[pl.ds(r0, ck), :] = ((key >> 8) & 0xFF).astype(F32).astype(BF16)
        low_ref[pl.ds(r0, ck), :] = (key & 0xFF).astype(F32).astype(BF16)

    def scores_body(c, carry):
        scores(c, False)
        return carry

    lax.fori_loop(0, nck - 1, scores_body, 0)
    scores(nck - 1, True)

    def sum_chunks(chunk_fn, init):
        def body(i, acc):
            r0 = pl.multiple_of(i * (2 * ck), 2 * ck)
            return acc + chunk_fn(r0) + chunk_fn(pl.multiple_of(r0 + ck, ck))
        acc = lax.fori_loop(0, nck // 2, body, init)
        return lax.cond(nck % 2 == 1,
                        lambda a: a + chunk_fn(pl.multiple_of((nck - 1) * ck, ck)),
                        lambda a: a, acc)

    def count(pred):
        def chunk(r0):
            m = pred(key_ref[pl.ds(r0, ck), :], r0).astype(I32)
            return m.reshape(ck // 8, 8, tq).sum(axis=0)
        return sum_chunks(chunk, jnp.zeros((8, tq), I32)).sum(axis=0, keepdims=True)

    def count_packed(ref, tb):
        def chunk(r0):
            m = ref[pl.ds(r0, ck), :] >= tb
            ones = jnp.where(m, jnp.ones((ck, tq), BF16), jnp.zeros((ck, tq), BF16))
            parts = [ones[i * 16:(i + 1) * 16, :] for i in range(ck // 16)]
            while len(parts) > 1:
                parts = [parts[i] + parts[i + 1] for i in range(0, len(parts), 2)]
            return parts[0].astype(F32)

        cacc = sum_chunks(chunk, jnp.zeros((16, tq), F32))
        return cacc.sum(axis=0, keepdims=True).astype(I32)

    def bf16_of_key16(k16):
        pat = jnp.where(k16 < 0, k16 ^ 0x7FFF, k16)
        return pltpu.bitcast(jnp.left_shift(pat, 16), F32).astype(BF16)

    def count_hi(trial16):
        return count_packed(hi_ref, bf16_of_key16(trial16))

    c_nonneg = count_hi(jnp.zeros((1, tq), I32))
    ok0 = c_nonneg >= topk
    state0 = (jnp.where(ok0, 0, -(2 ** 15)).astype(I32),
              jnp.where(ok0, c_nonneg, nck * ck), jnp.where(ok0, 0, c_nonneg))

    def hi_body(b, state):
        lo16, c_ge, c_gt = state
        trial = lo16 + jnp.left_shift(jnp.int32(1), 14 - b)
        c = count_hi(trial)
        ok = c >= topk
        return jnp.where(ok, trial, lo16), jnp.where(ok, c, c_ge), jnp.where(ok, c_gt, c)

    lo16, c_ge, c_gt = lax.fori_loop(0, 15, hi_body, state0)

    def resolve_byte(byte_ref, above_ref, above_val, c_ge, c_gt):
        def rewrite(c, carry):
            r0 = pl.multiple_of(c * ck, ck)
            above = above_ref[pl.ds(r0, ck), :]
            byte_ref[pl.ds(r0, ck), :] = jnp.where(
                above > above_val, jnp.full((ck, tq), 256.0, BF16),
                jnp.where(above < above_val, jnp.full((ck, tq), -1.0, BF16),
                          byte_ref[pl.ds(r0, ck), :]))
            return carry

        lax.fori_loop(0, nck, rewrite, 0)

        def byte_body(b, state):
            val, c_ge, c_gt = state
            trial = val + jnp.left_shift(jnp.int32(1), 7 - b)
            c = count_packed(byte_ref, trial.astype(F32).astype(BF16))
            ok = c >= topk
            return jnp.where(ok, trial, val), jnp.where(ok, c, c_ge), jnp.where(ok, c_gt, c)

        return lax.fori_loop(0, 8, byte_body, (jnp.zeros((1, tq), I32), c_ge, c_gt))

    b1, c_ge, c_gt = resolve_byte(mid_ref, hi_ref, bf16_of_key16(lo16), c_ge, c_gt)
    b0, c_ge, c_gt = resolve_byte(low_ref, mid_ref, b1.astype(F32).astype(BF16), c_ge, c_gt)
    lo = jnp.left_shift(lo16, 16) + jnp.left_shift(b1, 8) + b0

    need = topk - c_gt
    cut_ref[...] = jnp.full((1, tq), seq, I32)

    @pl.when(jnp.max(c_ge) > topk)
    def _():
        nbits = max(1, (seq - 1).bit_length())

        def idx_body(b, jv):
            trial = jv + jnp.left_shift(jnp.int32(1), nbits - 1 - b)
            c = count(lambda kk, r0: (kk == lo)
                      & ((r0 + lax.broadcasted_iota(I32, (ck, tq), 0)) < trial))
            return jnp.where(c < need, trial, jv)

        jv = lax.fori_loop(0, nbits, idx_body, jnp.zeros((1, tq), I32))
        cut_ref[...] = jv + 1

    cut = cut_ref[...]

    m_ref[...] = jnp.full((N_HEADS, 1, tq), M_INIT, F32)
    l_ref[...] = jnp.zeros((N_HEADS, 1, tq), F32)
    acc_ref[...] = jnp.zeros((ATTN_WIDTH, tq), F32)

    def attend(c, last):
        r0 = pl.multiple_of(c * ck, ck)
        kk = key_ref[pl.ds(r0, ck), :]
        kf = kf_ref[pl.ds(r0, ck), :]
        kch = k_ref[0, pl.ds(r0, ck), :]
        vtc = vt_ref[0, c]
        kpos = r0 + row_i
        sel = (kk > lo) | ((kk == lo) & (kpos < cut))
        if last:
            sel = sel & ((kpos // CHUNK) <= qchunk)
            ahead = jnp.maximum(kpos - qpos, 0).astype(F32)
        pen = jnp.where(sel, 0.0, NEG)
        for h in range(N_HEADS):
            p = h // 2
            ks = kch[:, p * LANES:(p + 1) * LANES]
            lg = _dot_nt(jnp.concatenate([ks, kf], axis=1), qmf_ref[h]) + pen
            if last:
                lg = lg - (2.0 * _SLOPE_L2E[h]) * ahead
            lg_ref[h] = lg
        for h in range(N_HEADS):
            m_old = m_ref[h]
            m_new = jnp.maximum(m_old, jnp.max(lg_ref[h], axis=0, keepdims=True))
            alpha = jnp.exp2(m_old - m_new)
            pr = jnp.exp2(lg_ref[h] - m_new)
            l_ref[h] = alpha * l_ref[h] + jnp.sum(pr, axis=0, keepdims=True)
            m_ref[h] = m_new
            rows = slice(h * HEAD_DIM, (h + 1) * HEAD_DIM)
            acc_ref[rows, :] = alpha * acc_ref[rows, :] + _dot(vtc[rows, :], pr.astype(BF16))

    def attend_body(c, carry):
        attend(c, False)
        return carry

    lax.fori_loop(0, nck - 1, attend_body, 0)
    attend(nck - 1, True)

    outs = []
    for h in range(N_HEADS):
        rows = slice(h * HEAD_DIM, (h + 1) * HEAD_DIM)
        o = acc_ref[rows, :] / l_ref[h]
        oms = jnp.mean(o * o, axis=0, keepdims=True)
        outs.append(o * lax.rsqrt(oms + EPS))
    ot = jnp.concatenate(outs, axis=0)
    o_ref[0] = (ot.T * aog_ref[...]).astype(BF16)


def _attn(q, k, qi, kid, vt, wt, attn_out_g, tq, ck):
    b, s, _ = q.shape
    topk = min(TOPK_MAX, s // 4)
    assert tq % LANES == 0 and ck % tq == 0 and s % ck == 0 and s // CHUNK <= 256
    pos = jnp.arange(s)
    feats = [pos // CHUNK] * 3 + [pos % CHUNK] * 3 + [jnp.ones_like(pos)] * 3
    kf = jnp.zeros((s, LANES), F32).at[:, 0:N_FEAT].set(
        jnp.stack(feats, axis=1).astype(F32)).astype(BF16)
    blk = lambda w: pl.BlockSpec((1, tq, w), lambda bi, ji: (bi, ji, 0))
    once = pl.Buffered(1)
    whole = lambda w: pl.BlockSpec((1, s, w), lambda bi, ji: (bi, 0, 0), pipeline_mode=once)
    return pl.pallas_call(
        functools.partial(_attn_kernel, tq=tq, ck=ck, seq=s, topk=topk),
        grid=(b, s // tq),
        in_specs=[blk(ATTN_WIDTH), blk(IDX_HEADS * IDX_DIM),
                  pl.BlockSpec((1, IDX_HEADS, tq), lambda bi, ji: (bi, 0, ji)),
                  whole(2 * IDX_DIM), whole(ATTN_WIDTH),
                  pl.BlockSpec((s, LANES), lambda bi, ji: (0, 0), pipeline_mode=once),
                  pl.BlockSpec((1, s // ck, ATTN_WIDTH, ck), lambda bi, ji: (bi, 0, 0, 0),
                               pipeline_mode=once),
                  pl.BlockSpec((1, ATTN_WIDTH), lambda bi, ji: (0, 0))],
        out_specs=blk(ATTN_WIDTH),
        out_shape=jax.ShapeDtypeStruct((b, s, ATTN_WIDTH), BF16),
        scratch_shapes=[pltpu.VMEM((s, tq), I32),
                        pltpu.VMEM((ATTN_WIDTH, tq), F32),
                        pltpu.VMEM((N_HEADS, 1, tq), F32),
                        pltpu.VMEM((N_HEADS, 1, tq), F32),
                        pltpu.VMEM((IDX_HEADS * tq, LANES), BF16),
                        pltpu.VMEM((N_HEADS, tq, 2 * LANES), BF16),
                        pltpu.VMEM((1, tq), I32),
                        pltpu.VMEM((N_HEADS, ck, tq), F32),
                        pltpu.VMEM((s, tq), BF16), pltpu.VMEM((s, tq), BF16),
                        pltpu.VMEM((s, tq), BF16)],
        compiler_params=pltpu.CompilerParams(
            dimension_semantics=("arbitrary", "arbitrary"), vmem_limit_bytes=VMEM_LIMIT_BYTES),
        name="attn",
    )(q, qi, wt, kid, k, kf, vt, attn_out_g.reshape(1, ATTN_WIDTH))


def _outproj_kernel(pool_ref, attn_ref, x_ref, wo_ref, g1_ref, sh_ref, sc_ref, ng_ref,
                    x1_ref, h2t_ref):
    mixed = _dot(pool_ref[0], wo_ref[0:POOL_WIDTH, :]) + _dot(attn_ref[0], wo_ref[POOL_WIDTH:, :])
    x1 = x_ref[0] + g1_ref[0] * mixed
    x1_ref[0] = x1
    ms = jnp.mean(x1 * x1, axis=-1, keepdims=True)
    h2 = (x1 * lax.rsqrt(ms + EPS) * ng_ref[...]) * (1.0 + sc_ref[0]) + sh_ref[0]
    h2t_ref[...] = h2.T.astype(BF16)


def _outproj(pool_out, attn_out, x, w_out, g1, sh2, sc2, norm_g, tm):
    b, s, d = x.shape
    nt = s // tm
    tok = lambda w: pl.BlockSpec((1, tm, w), lambda bi, ti: (bi, ti, 0))
    per_b = pl.BlockSpec((1, 1, d), lambda bi, ti: (bi, 0, 0))
    return pl.pallas_call(
        _outproj_kernel,
        grid=(b, nt),
        in_specs=[tok(POOL_WIDTH), tok(ATTN_WIDTH), tok(d),
                  pl.BlockSpec((POOL_WIDTH + ATTN_WIDTH, d), lambda bi, ti: (0, 0)),
                  per_b, per_b, per_b, pl.BlockSpec((1, d), lambda bi, ti: (0, 0))],
        out_specs=(tok(d), pl.BlockSpec((d, tm), lambda bi, ti: (0, bi * nt + ti))),
        out_shape=(jax.ShapeDtypeStruct((b, s, d), F32), jax.ShapeDtypeStruct((d, b * s), BF16)),
        compiler_params=pltpu.CompilerParams(dimension_semantics=("arbitrary", "arbitrary")),
        name="outproj",
    )(pool_out, attn_out, x, w_out.astype(BF16), g1, sh2, sc2, norm_g.reshape(1, d))


def _batcher_pairs(n):
    pairs = []

    def merge(lo, cnt, r):
        step = r * 2
        if step < cnt:
            merge(lo, cnt, step)
            merge(lo + r, cnt, step)
            for i in range(lo + r, lo + cnt - r, step):
                pairs.append((i, i + r))
        else:
            pairs.append((lo, lo + r))

    def sort(lo, cnt):
        if cnt > 1:
            m = cnt // 2
            sort(lo, m)
            sort(lo + m, m)
            merge(lo, cnt, 1)

    sort(0, n)
    return tuple(pairs)


_SORT16 = _batcher_pairs(PEER_TOPK)
MM_GROUP = 8


def _top16_desc(x):
    x = list(x)
    for i, j in _SORT16:
        x[i], x[j] = jnp.maximum(x[i], x[j]), jnp.minimum(x[i], x[j])
    for shift in (4, 2, 1):
        y = [jnp.maximum(x[k], pltpu.roll(x[PEER_TOPK - 1 - k], shift, 0)) for k in range(PEER_TOPK)]
        for d in (8, 4, 2, 1):
            for k in range(PEER_TOPK):
                if k & d == 0:
                    y[k], y[k + d] = jnp.maximum(y[k], y[k + d]), jnp.minimum(y[k], y[k + d])
        x = y
    return x


def _peer_kernel(h2t_ref, x1_ref, g2_ref, wqt_ref, sk_ref, u_ref, vt_ref, o_ref,
                 s1_ref, s2_ref, thr_ref, p_ref, acc_ref, s1b_ref, *, tm, eb, ne):
    e = pl.program_id(1)
    nsub = eb // PEER_KEYS
    ntl = tm // LANES

    @pl.when(e == 0)
    def _prologue():
        h2t = h2t_ref[...]
        sub = lax.broadcasted_iota(I32, (8, tm), 0)
        ninf = jnp.full((8, tm), -jnp.inf, F32)

        def compact(t, base):
            out = t[base + 7]
            for r in range(6, -1, -1):
                out = jnp.where(sub == r, t[base + r], out)
            return out

        def split8(st):
            return [st[k * 8:(k + 1) * 8, :] for k in range(PEER_KEYS // 8)]

        def head(hd, carry):
            q1 = _dot(wqt_ref[2 * hd], h2t).astype(BF16)
            q2 = _dot(wqt_ref[2 * hd + 1], h2t).astype(BF16)
            st1 = _dot(sk_ref[2 * hd], q1) * LOG2E
            st2 = _dot(sk_ref[2 * hd + 1], q2) * LOG2E
            for tl in range(ntl):
                s2_ref[hd, tl] = st2[:, tl * LANES:(tl + 1) * LANES]
            t1 = _top16_desc(split8(st1))
            t2 = _top16_desc(split8(st2))
            a2lo, a2hi = compact(t2, 0), compact(t2, 8)

            def candidates(t1x):
                cands = [t1x[0] + a2lo, t1x[0] + a2hi, t1x[1] + a2lo]
                for r1 in range(2, 8):
                    cands.append(jnp.where(sub < PEER_TOPK // (r1 + 1), t1x[r1] + a2lo, ninf))
                cands.append(compact(t1x, 8) + t2[0])
                return cands

            def kth(cands):
                return _top16_desc(cands + [ninf] * (PEER_TOPK - len(cands)))[PEER_TOPK - 1]

            cands = candidates(t1)
            thr = kth(cands)
            m = t1[0] + t2[0]
            z8 = jnp.zeros((8, tm), F32)
            for c in cands:
                z8 = z8 + jnp.where(c >= thr, jnp.exp2(c - m), 0.0)
            mz = m + jnp.log2(jnp.broadcast_to(jnp.sum(z8, axis=0, keepdims=True), (8, tm))) + 1.0
            s1_ref[hd] = st1 - mz[0:1, :]
            thr_ref[pl.ds(hd, 1), :] = kth(candidates([t - mz for t in t1]))[0:1, :]
            return carry

        lax.fori_loop(0, PEER_HEADS, head, 0)
        acc_ref[...] = jnp.zeros_like(acc_ref)

    i0 = pl.multiple_of(e * nsub, nsub)
    s1rows = [s1_ref[hd, pl.ds(i0, nsub), :] for hd in range(PEER_HEADS)]
    for hd in range(PEER_HEADS):
        s1b_ref[hd] = s1rows[hd]
    gsz = MM_GROUP * PEER_KEYS
    for g in range(nsub // MM_GROUP):
        grows = slice(g * gsz, (g + 1) * gsz)
        a_grp = _dot(u_ref[grows, :], h2t_ref[...])
        for sg in range(MM_GROUP):
            ii = g * MM_GROUP + sg
            rows = slice(ii * PEER_KEYS, (ii + 1) * PEER_KEYS)
            for tl in range(ntl):
                cols = slice(tl * LANES, (tl + 1) * LANES)
                w = jnp.zeros((PEER_KEYS, LANES), F32)
                for hd in range(PEER_HEADS):
                    sm = s1b_ref[hd, ii:ii + 1, cols] + s2_ref[hd, tl]
                    w = w + jnp.where(sm >= thr_ref[hd:hd + 1, cols], jnp.exp2(sm), 0.0)
                a = a_grp[sg * PEER_KEYS:(sg + 1) * PEER_KEYS, cols]
                gl = a * (1.0 + lax.erf(a * (1.0 / math.sqrt(2.0))))
                p_ref[rows, cols] = (gl * w).astype(BF16)
    acc_ref[...] += _dot(vt_ref[...], p_ref[...])

    @pl.when(e == ne - 1)
    def _epilogue():
        o_ref[...] = x1_ref[...] + g2_ref[0] * acc_ref[...].T


def _peer(h2t, x1, g2, wq, sub_keys, u_tab, v_tab, seq, tm, eb):
    d, t = h2t.shape
    ne = u_tab.shape[0] // eb
    half = PEER_KEYS
    wqt = wq.T.reshape(2 * PEER_HEADS, half, d).astype(BF16)
    sk = sub_keys.reshape(2 * PEER_HEADS, PEER_KEYS, half).astype(BF16)
    ub = u_tab.astype(BF16)
    vtb = v_tab.T.astype(BF16)
    hs = (PEER_HEADS, PEER_KEYS, tm)
    return pl.pallas_call(
        functools.partial(_peer_kernel, tm=tm, eb=eb, ne=ne),
        grid=(t // tm, ne),
        in_specs=[pl.BlockSpec((d, tm), lambda i, e: (0, i)),
                  pl.BlockSpec((tm, d), lambda i, e: (i, 0)),
                  pl.BlockSpec((1, 1, d), lambda i, e: ((i * tm) // seq, 0, 0)),
                  pl.BlockSpec((2 * PEER_HEADS, half, d), lambda i, e: (0, 0, 0)),
                  pl.BlockSpec((2 * PEER_HEADS, PEER_KEYS, half), lambda i, e: (0, 0, 0)),
                  pl.BlockSpec((eb, d), lambda i, e: (e, 0)),
                  pl.BlockSpec((d, eb), lambda i, e: (0, e))],
        out_specs=pl.BlockSpec((tm, d), lambda i, e: (i, 0)),
        out_shape=jax.ShapeDtypeStruct((t, d), F32),
        scratch_shapes=[pltpu.VMEM(hs, F32),
                        pltpu.VMEM((PEER_HEADS, tm // LANES, PEER_KEYS, LANES), F32),
                        pltpu.VMEM((PEER_HEADS, tm), F32),
                        pltpu.VMEM((eb, tm), BF16),
                        pltpu.VMEM((d, tm), F32),
                        pltpu.VMEM((PEER_HEADS, eb // PEER_KEYS, tm), F32)],
        compiler_params=pltpu.CompilerParams(
            dimension_semantics=("arbitrary", "arbitrary"), vmem_limit_bytes=VMEM_LIMIT_BYTES),
        name="peer",
    )(h2t, x1, g2, wqt, sk, ub, vtb)


def _tile(n, pref):
    t = min(pref, n)
    assert n % t == 0
    return t


def kernel(x, c, ada_w, ada_b, norm1_g, w_in, pool_w, pool_scale, pool_out_g, q_norm_g, k_norm_g,
           attn_out_g, w_out, norm2_g, peer_wq, peer_subkeys, peer_u, peer_v):
    b, s, d = x.shape
    depth = ada_w.shape[0]
    tm = _tile(s, 512)
    tq = _tile(s, 2 * LANES)
    ck = _tile(tm, 512)
    tp = _tile(s, 512)
    eb = 2048
    for l in range(depth):
        mod = _ada(c, ada_w[l], ada_b[l]).reshape(b, 1, 6 * d)
        sh1, sc1, g1, sh2, sc2, g2 = jnp.split(mod, 6, axis=-1)
        q, k, qi, kid, vt, wt, pool_out = _inproj(
            x, sh1, sc1, norm1_g[l], w_in[l], pool_w[l], pool_scale[l], pool_out_g[l],
            q_norm_g[l], k_norm_g[l], tm, ck)
        attn_out = _attn(q, k, qi, kid, vt, wt, attn_out_g[l], tq, ck)
        x1, h2t = _outproj(pool_out, attn_out, x, w_out[l], g1, sh2, sc2, norm2_g[l], tm)
        x = _peer(h2t, x1.reshape(b * s, d), g2, peer_wq[l], peer_subkeys[l],
                  peer_u[l], peer_v[l], s, tp, eb).reshape(b, s, d)
    return x
```

```python
import functools
import math

import ml_dtypes
import numpy as np

import jax
import jax.numpy as jnp
from jax import lax
from jax.experimental import pallas as pl
from jax.experimental.pallas import tpu as pltpu

F32 = jnp.float32
BF16 = jnp.bfloat16
I32 = jnp.int32

CHUNK = 64
EPS = 1e-6
POOL_WINDOWS = (2, 4, 8, 16)
POOL_GROUP = 128
POOL_WIDTH = 512
ATTN_WIDTH = 512
HEAD_DIM = 64
N_HEADS = 8
IDX_HEADS = 8
IDX_DIM = 64
TOPK_MAX = 256
PEER_HEADS = 8
PEER_KEYS = 128
PEER_TOPK = 16
NEG = -1e30

LANES = 128
V7X_VMEM_BYTES = 64 * 1024 * 1024
VMEM_LIMIT_BYTES = (7 * V7X_VMEM_BYTES) // 8
POOL_HALO = 16
LOG2E = math.log2(math.e)
M_INIT = 0.5 * NEG
N_FEAT = 9

NT_DIMS = (((1,), (1,)), ((), ()))


def _dot(a, b):
    return jnp.dot(a, b, preferred_element_type=F32)


def _dot_nt(a, b):
    return lax.dot_general(a, b, NT_DIMS, preferred_element_type=F32)


def _bf16_split3(c):
    c = np.float32(c)
    p1 = np.float32(c.astype(ml_dtypes.bfloat16))
    r = np.float32(c - p1)
    p2 = np.float32(r.astype(ml_dtypes.bfloat16))
    p3 = np.float32(np.float32(r - p2).astype(ml_dtypes.bfloat16))
    return float(p1), float(p2), float(p3)


_SLOPE_L2E = tuple(float(np.float32(2.0 ** (-8.0 * (h + 1) / N_HEADS) * LOG2E)) for h in range(N_HEADS))


def _ada_kernel(c_ref, w_ref, b_ref, o_ref):
    c = c_ref[...]
    cond = c * jax.nn.sigmoid(c)
    o_ref[...] = jnp.dot(cond, w_ref[...], preferred_element_type=F32,
                         precision=lax.Precision.HIGHEST) + b_ref[...]


def _ada(c, w, b):
    bsz, d = c.shape
    n = w.shape[1]
    tn = 1024
    return pl.pallas_call(
        _ada_kernel,
        grid=(n // tn,),
        in_specs=[pl.BlockSpec((bsz, d), lambda j: (0, 0)),
                  pl.BlockSpec((d, tn), lambda j: (0, j)),
                  pl.BlockSpec((1, tn), lambda j: (0, j))],
        out_specs=pl.BlockSpec((bsz, tn), lambda j: (0, j)),
        out_shape=jax.ShapeDtypeStruct((bsz, n), F32),
        name="ada",
    )(c, w, b.reshape(1, n))


def _group_sumsq(t, ones_bd):
    sq = t * t
    hi = sq.astype(BF16)
    lo = (sq - hi.astype(F32)).astype(BF16)
    return _dot(hi, ones_bd) + _dot(lo, ones_bd)


def _inproj_kernel(x_ref, sh_ref, sc_ref, g_ref, wnat_ref, wki_ref, wvt_ref, wwt_ref, ones_ref,
                   qg_ref, kg_ref, poolw_ref, pscale_ref, pg_ref,
                   q_ref, k_ref, qi_ref, kid_ref, vt_ref, wt_ref, pool_ref,
                   ubuf, *, tm, kc):
    i = pl.program_id(1)
    x = x_ref[0]
    ms = jnp.mean(x * x, axis=-1, keepdims=True)
    h = (x * lax.rsqrt(ms + EPS) * g_ref[...]) * (1.0 + sc_ref[0]) + sh_ref[0]
    hb = h.astype(BF16)

    proj = _dot(hb, wnat_ref[...])
    u = proj[:, 0:512]
    q = proj[:, 512:1024]
    k = proj[:, 1024:1536]
    qi = proj[:, 1536:2048]

    ones_bd = ones_ref[...]
    qn = q * lax.rsqrt(_group_sumsq(q, ones_bd) * (1.0 / HEAD_DIM) + EPS) * qg_ref[...]
    kn = k * lax.rsqrt(_group_sumsq(k, ones_bd) * (1.0 / HEAD_DIM) + EPS) * kg_ref[...]
    q_ref[0] = qn.astype(BF16)
    k_ref[0] = kn.astype(BF16)
    qi_ref[0] = (qi * (IDX_DIM ** -0.5)).astype(BF16)
    kid_ref[0] = _dot(hb, wki_ref[...]).astype(BF16)
    vt = _dot_nt(wvt_ref[...], hb).astype(BF16)
    for ci in range(tm // kc):
        vt_ref[0, ci] = vt[:, ci * kc:(ci + 1) * kc]
    wt = _dot_nt(wwt_ref[...], hb)
    wt_ref[0] = wt[0:IDX_HEADS, :] * (IDX_HEADS ** -0.5)

    @pl.when(i == 0)
    def _():
        ubuf[0:POOL_HALO, :] = jnp.zeros((POOL_HALO, POOL_WIDTH), F32)

    ubuf[POOL_HALO:POOL_HALO + tm, :] = u
    tpos = i * tm + lax.broadcasted_iota(I32, (tm, POOL_GROUP), 0)
    parts = []
    for gi, win in enumerate(POOL_WINDOWS):
        lo_l, hi_l = gi * POOL_GROUP, (gi + 1) * POOL_GROUP
        ws = ubuf[POOL_HALO:POOL_HALO + tm, lo_l:hi_l]
        for j in range(1, win):
            ws = ws + ubuf[POOL_HALO - j:POOL_HALO - j + tm, lo_l:hi_l]
        cnt = jnp.minimum(tpos + 1, win).astype(F32)
        pooled = ws / cnt - u[:, lo_l:hi_l]
        parts.append(_dot(pooled.astype(BF16), poolw_ref[gi]))
    mixed = jnp.concatenate(parts, axis=-1) * pscale_ref[...]
    pms = jnp.mean(mixed * mixed, axis=-1, keepdims=True)
    pool_ref[0] = (mixed * lax.rsqrt(pms + EPS) * pg_ref[...]).astype(BF16)
    ubuf[0:POOL_HALO, :] = ubuf[tm:tm + POOL_HALO, :]


def _inproj(x, sh1, sc1, norm_g, w_in, pool_w, pool_scale, pool_out_g, q_norm_g, k_norm_g, tm, kc):
    b, s, d = x.shape
    nt = s // tm
    w_u, w_q, w_k, w_v, w_qi, w_ki, w_wi = jnp.split(
        w_in, (512, 1024, 1536, 2048, 2560, 2624), axis=-1)
    wnat = jnp.concatenate([w_u, w_q, w_k, w_qi], axis=-1).astype(BF16)
    wki = jnp.concatenate([w_ki, w_ki], axis=-1).astype(BF16)
    wvt = w_v.T.astype(BF16)
    wwt = jnp.concatenate([w_wi.T, jnp.zeros((16 - IDX_HEADS, d), F32)], axis=0).astype(BF16)
    gid = jnp.arange(ATTN_WIDTH) // HEAD_DIM
    ones_bd = (gid[:, None] == gid[None, :]).astype(BF16)
    qg = (jnp.tile(q_norm_g, N_HEADS) * (HEAD_DIM ** -0.5 * LOG2E)).reshape(1, ATTN_WIDTH)
    kg = jnp.tile(k_norm_g, N_HEADS).reshape(1, ATTN_WIDTH)

    full = lambda shape: pl.BlockSpec(shape, lambda bi, ti: (0,) * len(shape))
    tok = lambda w: pl.BlockSpec((1, tm, w), lambda bi, ti: (bi, ti, 0))
    per_b = pl.BlockSpec((1, 1, d), lambda bi, ti: (bi, 0, 0))
    out_shapes = (
        jax.ShapeDtypeStruct((b, s, ATTN_WIDTH), BF16),
        jax.ShapeDtypeStruct((b, s, ATTN_WIDTH), BF16),
        jax.ShapeDtypeStruct((b, s, IDX_HEADS * IDX_DIM), BF16),
        jax.ShapeDtypeStruct((b, s, 2 * IDX_DIM), BF16),
        jax.ShapeDtypeStruct((b, s // kc, ATTN_WIDTH, kc), BF16),
        jax.ShapeDtypeStruct((b, IDX_HEADS, s), F32),
        jax.ShapeDtypeStruct((b, s, POOL_WIDTH), BF16),
    )
    out_specs = (
        tok(ATTN_WIDTH), tok(ATTN_WIDTH), tok(IDX_HEADS * IDX_DIM), tok(2 * IDX_DIM),
        pl.BlockSpec((1, tm // kc, ATTN_WIDTH, kc), lambda bi, ti: (bi, ti, 0, 0)),
        pl.BlockSpec((1, IDX_HEADS, tm), lambda bi, ti: (bi, 0, ti)),
        tok(POOL_WIDTH),
    )
    return pl.pallas_call(
        functools.partial(_inproj_kernel, tm=tm, kc=kc),
        grid=(b, nt),
        in_specs=[tok(d), per_b, per_b, full((1, d)), full((d, 2048)), full((d, 2 * IDX_DIM)),
                  full((ATTN_WIDTH, d)), full((16, d)), full((ATTN_WIDTH, ATTN_WIDTH)),
                  full((1, ATTN_WIDTH)), full((1, ATTN_WIDTH)),
                  full((len(POOL_WINDOWS), POOL_GROUP, POOL_GROUP)),
                  full((1, POOL_WIDTH)), full((1, POOL_WIDTH))],
        out_specs=out_specs,
        out_shape=out_shapes,
        scratch_shapes=[pltpu.VMEM((tm + POOL_HALO, POOL_WIDTH), F32)],
        compiler_params=pltpu.CompilerParams(
            dimension_semantics=("arbitrary", "arbitrary"), vmem_limit_bytes=VMEM_LIMIT_BYTES),
        name="inproj",
    )(x, sh1, sc1, norm_g.reshape(1, d), wnat, wki, wvt, wwt, ones_bd, qg, kg,
      pool_w.astype(BF16), pool_scale.reshape(1, -1), pool_out_g.reshape(1, -1))


def _attn_kernel(q_ref, qi_ref, wt_ref, kid_ref, k_ref, kf_ref, vt_ref, aog_ref, o_ref,
                 key_ref, acc_ref, m_ref, l_ref, qim_ref, qmf_ref, cut_ref, lg_ref,
                 hi_ref, mid_ref, low_ref, sel_ref,
                 *, tq, ck, seq, topk):
    j = pl.program_id(1)
    nck = ((j + 1) * tq + ck - 1) // ck
    qpos = j * tq + lax.broadcasted_iota(I32, (1, tq), 1)
    qchunk = qpos // CHUNK
    row_i = lax.broadcasted_iota(I32, (ck, tq), 0)

    lane = lax.broadcasted_iota(I32, (tq, LANES), 1)
    tcol = (j * tq + lax.broadcasted_iota(I32, (tq, LANES), 0)).astype(F32)
    for h in range(N_HEADS):
        p = h // 2
        keep = (lane < HEAD_DIM) if h % 2 == 0 else (lane >= HEAD_DIM)
        qi_slab = qi_ref[0, :, p * LANES:(p + 1) * LANES]
        q_slab = q_ref[0, :, p * LANES:(p + 1) * LANES]
        qim_ref[h * tq:(h + 1) * tq, :] = jnp.where(keep, qi_slab, jnp.zeros_like(qi_slab))
        qmf_ref[h, :, 0:LANES] = jnp.where(keep, q_slab, jnp.zeros_like(q_slab))
        c1, c2, c3 = _bf16_split3(_SLOPE_L2E[h])
        u = -_SLOPE_L2E[h] * tcol
        u1 = u.astype(BF16).astype(F32)
        r = u - u1
        u2 = r.astype(BF16).astype(F32)
        u3 = (r - u2).astype(BF16).astype(F32)
        feat = jnp.zeros((tq, LANES), F32)
        for li, val in enumerate((64.0 * c1, 64.0 * c2, 64.0 * c3, c1, c2, c3, u1, u2, u3)):
            feat = jnp.where(lane == li, val, feat)
        qmf_ref[h, :, LANES:2 * LANES] = feat.astype(BF16)

    def scores(c, last):
        r0 = pl.multiple_of(c * ck, ck)
        d = _dot_nt(kid_ref[0, pl.ds(r0, ck), :], qim_ref[...])
        acc = jnp.zeros((ck, tq), F32)
        for h in range(IDX_HEADS):
            acc = acc + wt_ref[0, h:h + 1, :] * jnp.maximum(d[:, h * tq:(h + 1) * tq], 0.0)
        if last:
            acc = jnp.where(((r0 + row_i) // CHUNK) <= qchunk, acc, NEG)
        bits = pltpu.bitcast(acc, I32)
        key = jnp.where(bits < 0, bits ^ 0x7FFFFFFF, bits)
        key_ref[pl.ds(r0, ck), :] = key
        hi_ref[pl.ds(r0, ck), :] = pltpu.bitcast(bits & -65536, F32).astype(BF16)
        mid_ref[pl.ds(r0, ck), :] = ((key >> 8) & 0xFF).astype(F32).astype(BF16)
        low_ref[pl.ds(r0, ck), :] = (key & 0xFF).astype(F32).astype(BF16)

    def scores_body(c, carry):
        scores(c, False)
        return carry

    lax.fori_loop(0, nck - 1, scores_body, 0)
    scores(nck - 1, True)

    def sum_chunks(chunk_fn, init):
        def body(i, acc):
            r0 = pl.multiple_of(i * (2 * ck), 2 * ck)
            return acc + chunk_fn(r0) + chunk_fn(pl.multiple_of(r0 + ck, ck))
        acc = lax.fori_loop(0, nck // 2, body, init)
        return lax.cond(nck % 2 == 1,
                        lambda a: a + chunk_fn(pl.multiple_of((nck - 1) * ck, ck)),
                        lambda a: a, acc)

    def count(pred):
        def chunk(r0):
            m = pred(key_ref[pl.ds(r0, ck), :], r0).astype(I32)
            return m.reshape(ck // 8, 8, tq).sum(axis=0)
        return sum_chunks(chunk, jnp.zeros((8, tq), I32)).sum(axis=0, keepdims=True)

    def count_packed(ref, tb):
        def chunk(r0):
            m = ref[pl.ds(r0, ck), :] >= tb
            ones = jnp.where(m, jnp.ones((ck, tq), BF16), jnp.zeros((ck, tq), BF16))
            parts = [ones[i * 16:(i + 1) * 16, :] for i in range(ck // 16)]
            while len(parts) > 1:
                parts = [parts[i] + parts[i + 1] for i in range(0, len(parts), 2)]
            return parts[0].astype(F32)

        cacc = sum_chunks(chunk, jnp.zeros((16, tq), F32))
        return cacc.sum(axis=0, keepdims=True).astype(I32)

    def bf16_of_key16(k16):
        pat = jnp.where(k16 < 0, k16 ^ 0x7FFF, k16)
        return pltpu.bitcast(jnp.left_shift(pat, 16), F32).astype(BF16)

    def count_hi(trial16):
        return count_packed(hi_ref, bf16_of_key16(trial16))

    c_nonneg = count_hi(jnp.zeros((1, tq), I32))
    ok0 = c_nonneg >= topk
    state0 = (jnp.where(ok0, 0, -(2 ** 15)).astype(I32),
              jnp.where(ok0, c_nonneg, nck * ck), jnp.where(ok0, 0, c_nonneg))

    def hi_body(b, state):
        lo16, c_ge, c_gt = state
        trial = lo16 + jnp.left_shift(jnp.int32(1), 14 - b)
        c = count_hi(trial)
        sub = (trial > 0) & (trial < 128)
        ok = (c >= topk) & jnp.logical_not(sub)
        return (jnp.where(ok, trial, lo16), jnp.where(ok, c, c_ge),
                jnp.where(ok | sub, c_gt, c))

    lo16, c_ge, c_gt = lax.fori_loop(0, 15, hi_body, state0)

    def resolve_byte(byte_ref, above_ref, above_val, c_ge, c_gt):
        def rewrite(c, carry):
            r0 = pl.multiple_of(c * ck, ck)
            above = above_ref[pl.ds(r0, ck), :]
            byte_ref[pl.ds(r0, ck), :] = jnp.where(
                above > above_val, jnp.full((ck, tq), 256.0, BF16),
                jnp.where(above < above_val, jnp.full((ck, tq), -1.0, BF16),
                          byte_ref[pl.ds(r0, ck), :]))
            return carry

        lax.fori_loop(0, nck, rewrite, 0)

        def byte_body(b, state):
            val, c_ge, c_gt = state
            trial = val + jnp.left_shift(jnp.int32(1), 7 - b)
            c = count_packed(byte_ref, trial.astype(F32).astype(BF16))
            ok = c >= topk
            return jnp.where(ok, trial, val), jnp.where(ok, c, c_ge), jnp.where(ok, c_gt, c)

        return lax.fori_loop(0, 8, byte_body, (jnp.zeros((1, tq), I32), c_ge, c_gt))

    b1, c_ge, c_gt = resolve_byte(mid_ref, hi_ref, bf16_of_key16(lo16), c_ge, c_gt)
    b0, c_ge, c_gt = resolve_byte(low_ref, mid_ref, b1.astype(F32).astype(BF16), c_ge, c_gt)
    lo = jnp.left_shift(lo16, 16) + jnp.left_shift(b1, 8) + b0

    c_ge = count(lambda kk, r0: kk >= lo)
    c_gt = count(lambda kk, r0: kk > lo)
    sel_ref[0:1, :] = lo
    sel_ref[1:2, :] = c_ge
    sel_ref[2:3, :] = c_gt

    @pl.when(jnp.max(((c_ge < topk) | (c_gt >= topk)).astype(I32)) > 0)
    def _():
        c0 = count(lambda kk, r0: kk >= 0)
        ok0 = c0 >= topk
        st = (jnp.where(ok0, 0, -(2 ** 31)).astype(I32),
              jnp.where(ok0, c0, nck * ck), jnp.where(ok0, 0, c0))

        def key_body(b, state):
            lo_, ge_, gt_ = state
            trial = lo_ + jnp.left_shift(jnp.int32(1), 30 - b)
            c = count(lambda kk, r0: kk >= trial)
            ok = c >= topk
            return jnp.where(ok, trial, lo_), jnp.where(ok, c, ge_), jnp.where(ok, gt_, c)

        lo_, ge_, gt_ = lax.fori_loop(0, 31, key_body, st)
        sel_ref[0:1, :] = lo_
        sel_ref[1:2, :] = ge_
        sel_ref[2:3, :] = gt_

    lo = sel_ref[0:1, :]
    c_ge = sel_ref[1:2, :]
    c_gt = sel_ref[2:3, :]

    need = topk - c_gt
    cut_ref[...] = jnp.full((1, tq), seq, I32)

    @pl.when(jnp.max(c_ge) > topk)
    def _():
        nbits = max(1, (seq - 1).bit_length())

        def idx_body(b, jv):
            trial = jv + jnp.left_shift(jnp.int32(1), nbits - 1 - b)
            c = count(lambda kk, r0: (kk == lo)
                      & ((r0 + lax.broadcasted_iota(I32, (ck, tq), 0)) < trial))
            return jnp.where(c < need, trial, jv)

        jv = lax.fori_loop(0, nbits, idx_body, jnp.zeros((1, tq), I32))
        cut_ref[...] = jv + 1

    cut = cut_ref[...]

    m_ref[...] = jnp.full((N_HEADS, 1, tq), M_INIT, F32)
    l_ref[...] = jnp.zeros((N_HEADS, 1, tq), F32)
    acc_ref[...] = jnp.zeros((ATTN_WIDTH, tq), F32)

    def attend(c, last):
        r0 = pl.multiple_of(c * ck, ck)
        kk = key_ref[pl.ds(r0, ck), :]
        kf = kf_ref[pl.ds(r0, ck), :]
        kch = k_ref[0, pl.ds(r0, ck), :]
        vtc = vt_ref[0, c]
        kpos = r0 + row_i
        sel = (kk > lo) | ((kk == lo) & (kpos < cut))
        if last:
            sel = sel & ((kpos // CHUNK) <= qchunk)
            ahead = jnp.maximum(kpos - qpos, 0).astype(F32)
        pen = jnp.where(sel, 0.0, NEG)
        for h in range(N_HEADS):
            p = h // 2
            ks = kch[:, p * LANES:(p + 1) * LANES]
            lg = _dot_nt(jnp.concatenate([ks, kf], axis=1), qmf_ref[h]) + pen
            if last:
                lg = lg - (2.0 * _SLOPE_L2E[h]) * ahead
            lg_ref[h] = lg
        for h in range(N_HEADS):
            m_old = m_ref[h]
            m_new = jnp.maximum(m_old, jnp.max(lg_ref[h], axis=0, keepdims=True))
            alpha = jnp.exp2(m_old - m_new)
            pr = jnp.exp2(lg_ref[h] - m_new)
            l_ref[h] = alpha * l_ref[h] + jnp.sum(pr, axis=0, keepdims=True)
            m_ref[h] = m_new
            rows = slice(h * HEAD_DIM, (h + 1) * HEAD_DIM)
            acc_ref[rows, :] = alpha * acc_ref[rows, :] + _dot(vtc[rows, :], pr.astype(BF16))

    def attend_body(c, carry):
        attend(c, False)
        return carry

    lax.fori_loop(0, nck - 1, attend_body, 0)
    attend(nck - 1, True)

    outs = []
    for h in range(N_HEADS):
        rows = slice(h * HEAD_DIM, (h + 1) * HEAD_DIM)
        o = acc_ref[rows, :] / l_ref[h]
        oms = jnp.mean(o * o, axis=0, keepdims=True)
        outs.append(o * lax.rsqrt(oms + EPS))
    ot = jnp.concatenate(outs, axis=0)
    o_ref[0] = (ot.T * aog_ref[...]).astype(BF16)


def _attn(q, k, qi, kid, vt, wt, attn_out_g, tq, ck):
    b, s, _ = q.shape
    topk = min(TOPK_MAX, s // 4)
    assert tq % LANES == 0 and ck % tq == 0 and s % ck == 0 and s // CHUNK <= 256
    pos = jnp.arange(s)
    feats = [pos // CHUNK] * 3 + [pos % CHUNK] * 3 + [jnp.ones_like(pos)] * 3
    kf = jnp.zeros((s, LANES), F32).at[:, 0:N_FEAT].set(
        jnp.stack(feats, axis=1).astype(F32)).astype(BF16)
    blk = lambda w: pl.BlockSpec((1, tq, w), lambda bi, ji: (bi, ji, 0))
    once = pl.Buffered(1)
    whole = lambda w: pl.BlockSpec((1, s, w), lambda bi, ji: (bi, 0, 0), pipeline_mode=once)
    return pl.pallas_call(
        functools.partial(_attn_kernel, tq=tq, ck=ck, seq=s, topk=topk),
        grid=(b, s // tq),
        in_specs=[blk(ATTN_WIDTH), blk(IDX_HEADS * IDX_DIM),
                  pl.BlockSpec((1, IDX_HEADS, tq), lambda bi, ji: (bi, 0, ji)),
                  whole(2 * IDX_DIM), whole(ATTN_WIDTH),
                  pl.BlockSpec((s, LANES), lambda bi, ji: (0, 0), pipeline_mode=once),
                  pl.BlockSpec((1, s // ck, ATTN_WIDTH, ck), lambda bi, ji: (bi, 0, 0, 0),
                               pipeline_mode=once),
                  pl.BlockSpec((1, ATTN_WIDTH), lambda bi, ji: (0, 0))],
        out_specs=blk(ATTN_WIDTH),
        out_shape=jax.ShapeDtypeStruct((b, s, ATTN_WIDTH), BF16),
        scratch_shapes=[pltpu.VMEM((s, tq), I32),
                        pltpu.VMEM((ATTN_WIDTH, tq), F32),
                        pltpu.VMEM((N_HEADS, 1, tq), F32),
                        pltpu.VMEM((N_HEADS, 1, tq), F32),
                        pltpu.VMEM((IDX_HEADS * tq, LANES), BF16),
                        pltpu.VMEM((N_HEADS, tq, 2 * LANES), BF16),
                        pltpu.VMEM((1, tq), I32),
                        pltpu.VMEM((N_HEADS, ck, tq), F32),
                        pltpu.VMEM((s, tq), BF16), pltpu.VMEM((s, tq), BF16),
                        pltpu.VMEM((s, tq), BF16), pltpu.VMEM((8, tq), I32)],
        compiler_params=pltpu.CompilerParams(
            dimension_semantics=("arbitrary", "arbitrary"), vmem_limit_bytes=VMEM_LIMIT_BYTES),
        name="attn",
    )(q, qi, wt, kid, k, kf, vt, attn_out_g.reshape(1, ATTN_WIDTH))


def _outproj_kernel(pool_ref, attn_ref, x_ref, wo_ref, g1_ref, sh_ref, sc_ref, ng_ref,
                    x1_ref, h2t_ref):
    mixed = _dot(pool_ref[0], wo_ref[0:POOL_WIDTH, :]) + _dot(attn_ref[0], wo_ref[POOL_WIDTH:, :])
    x1 = x_ref[0] + g1_ref[0] * mixed
    x1_ref[0] = x1
    ms = jnp.mean(x1 * x1, axis=-1, keepdims=True)
    h2 = (x1 * lax.rsqrt(ms + EPS) * ng_ref[...]) * (1.0 + sc_ref[0]) + sh_ref[0]
    h2t_ref[...] = h2.T.astype(BF16)


def _outproj(pool_out, attn_out, x, w_out, g1, sh2, sc2, norm_g, tm):
    b, s, d = x.shape
    nt = s // tm
    tok = lambda w: pl.BlockSpec((1, tm, w), lambda bi, ti: (bi, ti, 0))
    per_b = pl.BlockSpec((1, 1, d), lambda bi, ti: (bi, 0, 0))
    return pl.pallas_call(
        _outproj_kernel,
        grid=(b, nt),
        in_specs=[tok(POOL_WIDTH), tok(ATTN_WIDTH), tok(d),
                  pl.BlockSpec((POOL_WIDTH + ATTN_WIDTH, d), lambda bi, ti: (0, 0)),
                  per_b, per_b, per_b, pl.BlockSpec((1, d), lambda bi, ti: (0, 0))],
        out_specs=(tok(d), pl.BlockSpec((d, tm), lambda bi, ti: (0, bi * nt + ti))),
        out_shape=(jax.ShapeDtypeStruct((b, s, d), F32), jax.ShapeDtypeStruct((d, b * s), BF16)),
        compiler_params=pltpu.CompilerParams(dimension_semantics=("arbitrary", "arbitrary")),
        name="outproj",
    )(pool_out, attn_out, x, w_out.astype(BF16), g1, sh2, sc2, norm_g.reshape(1, d))


def _batcher_pairs(n):
    pairs = []

    def merge(lo, cnt, r):
        step = r * 2
        if step < cnt:
            merge(lo, cnt, step)
            merge(lo + r, cnt, step)
            for i in range(lo + r, lo + cnt - r, step):
                pairs.append((i, i + r))
        else:
            pairs.append((lo, lo + r))

    def sort(lo, cnt):
        if cnt > 1:
            m = cnt // 2
            sort(lo, m)
            sort(lo + m, m)
            merge(lo, cnt, 1)

    sort(0, n)
    return tuple(pairs)


_SORT16 = _batcher_pairs(PEER_TOPK)
MM_GROUP = 8


def _top16_desc(x):
    x = list(x)
    for i, j in _SORT16:
        x[i], x[j] = jnp.maximum(x[i], x[j]), jnp.minimum(x[i], x[j])
    for shift in (4, 2, 1):
        y = [jnp.maximum(x[k], pltpu.roll(x[PEER_TOPK - 1 - k], shift, 0)) for k in range(PEER_TOPK)]
        for d in (8, 4, 2, 1):
            for k in range(PEER_TOPK):
                if k & d == 0:
                    y[k], y[k + d] = jnp.maximum(y[k], y[k + d]), jnp.minimum(y[k], y[k + d])
        x = y
    return x


def _peer_kernel(h2t_ref, x1_ref, g2_ref, wqt_ref, sk_ref, u_ref, vt_ref, o_ref,
                 s1_ref, s2_ref, thr_ref, p_ref, acc_ref, s1b_ref, qt_ref, *, tm, eb, ne):
    e = pl.program_id(1)
    nsub = eb // PEER_KEYS
    ntl = tm // LANES

    @pl.when(e == 0)
    def _prologue():
        h2t = h2t_ref[...]
        sub = lax.broadcasted_iota(I32, (8, tm), 0)
        ninf = jnp.full((8, tm), -jnp.inf, F32)

        def compact(t, base):
            out = t[base + 7]
            for r in range(6, -1, -1):
                out = jnp.where(sub == r, t[base + r], out)
            return out

        def split8(st):
            return [st[k * 8:(k + 1) * 8, :] for k in range(PEER_KEYS // 8)]

        qt_ref[...] = _dot(wqt_ref[...], h2t).astype(BF16).reshape(qt_ref.shape)

        def head(hd, carry):
            st1 = _dot(sk_ref[2 * hd], qt_ref[2 * hd]) * LOG2E
            st2 = _dot(sk_ref[2 * hd + 1], qt_ref[2 * hd + 1]) * LOG2E
            for tl in range(ntl):
                s2_ref[hd, tl] = st2[:, tl * LANES:(tl + 1) * LANES]
            t1 = _top16_desc(split8(st1))
            t2 = _top16_desc(split8(st2))
            a2lo, a2hi = compact(t2, 0), compact(t2, 8)

            def candidates(t1x):
                cands = [t1x[0] + a2lo, t1x[0] + a2hi, t1x[1] + a2lo]
                for r1 in range(2, 8):
                    cands.append(jnp.where(sub < PEER_TOPK // (r1 + 1), t1x[r1] + a2lo, ninf))
                cands.append(compact(t1x, 8) + t2[0])
                return cands

            def kth(cands):
                return _top16_desc(cands + [ninf] * (PEER_TOPK - len(cands)))[PEER_TOPK - 1]

            cands = candidates(t1)
            thr = kth(cands)
            m = t1[0] + t2[0]
            z8 = jnp.zeros((8, tm), F32)
            for c in cands:
                z8 = z8 + jnp.where(c >= thr, jnp.exp2(c - m), 0.0)
            mz = m + jnp.log2(jnp.broadcast_to(jnp.sum(z8, axis=0, keepdims=True), (8, tm))) + 1.0
            s1_ref[hd] = st1 - mz[0:1, :]
            thr_ref[pl.ds(hd, 1), :] = kth(candidates([t - mz for t in t1]))[0:1, :]
            return carry

        lax.fori_loop(0, PEER_HEADS, head, 0)
        acc_ref[...] = jnp.zeros_like(acc_ref)

    i0 = pl.multiple_of(e * nsub, nsub)
    s1rows = [s1_ref[hd, pl.ds(i0, nsub), :] for hd in range(PEER_HEADS)]
    for hd in range(PEER_HEADS):
        s1b_ref[hd] = s1rows[hd]
    gsz = MM_GROUP * PEER_KEYS
    for g in range(nsub // MM_GROUP):
        grows = slice(g * gsz, (g + 1) * gsz)
        a_grp = _dot(u_ref[grows, :], h2t_ref[...])
        for sg in range(MM_GROUP):
            ii = g * MM_GROUP + sg
            rows = slice(ii * PEER_KEYS, (ii + 1) * PEER_KEYS)
            for tl in range(ntl):
                cols = slice(tl * LANES, (tl + 1) * LANES)
                w = jnp.zeros((PEER_KEYS, LANES), F32)
                for hd in range(PEER_HEADS):
                    sm = s1b_ref[hd, ii:ii + 1, cols] + s2_ref[hd, tl]
                    w = w + jnp.where(sm >= thr_ref[hd:hd + 1, cols], jnp.exp2(sm), 0.0)
                a = a_grp[sg * PEER_KEYS:(sg + 1) * PEER_KEYS, cols]
                gl = a * (1.0 + lax.erf(a * (1.0 / math.sqrt(2.0))))
                p_ref[rows, cols] = (gl * w).astype(BF16)
    acc_ref[...] += _dot(vt_ref[...], p_ref[...])

    @pl.when(e == ne - 1)
    def _epilogue():
        o_ref[...] = x1_ref[...] + g2_ref[0] * acc_ref[...].T


def _peer(h2t, x1, g2, wq, sub_keys, u_tab, v_tab, seq, tm, eb):
    d, t = h2t.shape
    ne = u_tab.shape[0] // eb
    half = PEER_KEYS
    wqt = wq.T.astype(BF16)
    sk = sub_keys.reshape(2 * PEER_HEADS, PEER_KEYS, half).astype(BF16)
    ub = u_tab.astype(BF16)
    vtb = v_tab.T.astype(BF16)
    hs = (PEER_HEADS, PEER_KEYS, tm)
    return pl.pallas_call(
        functools.partial(_peer_kernel, tm=tm, eb=eb, ne=ne),
        grid=(t // tm, ne),
        in_specs=[pl.BlockSpec((d, tm), lambda i, e: (0, i)),
                  pl.BlockSpec((tm, d), lambda i, e: (i, 0)),
                  pl.BlockSpec((1, 1, d), lambda i, e: ((i * tm) // seq, 0, 0)),
                  pl.BlockSpec((2 * PEER_HEADS * half, d), lambda i, e: (0, 0)),
                  pl.BlockSpec((2 * PEER_HEADS, PEER_KEYS, half), lambda i, e: (0, 0, 0)),
                  pl.BlockSpec((eb, d), lambda i, e: (e, 0)),
                  pl.BlockSpec((d, eb), lambda i, e: (0, e))],
        out_specs=pl.BlockSpec((tm, d), lambda i, e: (i, 0)),
        out_shape=jax.ShapeDtypeStruct((t, d), F32),
        scratch_shapes=[pltpu.VMEM(hs, F32),
                        pltpu.VMEM((PEER_HEADS, tm // LANES, PEER_KEYS, LANES), F32),
                        pltpu.VMEM((PEER_HEADS, tm), F32),
                        pltpu.VMEM((eb, tm), BF16),
                        pltpu.VMEM((d, tm), F32),
                        pltpu.VMEM((PEER_HEADS, eb // PEER_KEYS, tm), F32),
                        pltpu.VMEM((2 * PEER_HEADS, half, tm), BF16)],
        compiler_params=pltpu.CompilerParams(
            dimension_semantics=("arbitrary", "arbitrary"), vmem_limit_bytes=VMEM_LIMIT_BYTES),
        name="peer",
    )(h2t, x1, g2, wqt, sk, ub, vtb)


def _tile(n, pref):
    t = min(pref, n)
    assert n % t == 0
    return t


def kernel(x, c, ada_w, ada_b, norm1_g, w_in, pool_w, pool_scale, pool_out_g, q_norm_g, k_norm_g,
           attn_out_g, w_out, norm2_g, peer_wq, peer_subkeys, peer_u, peer_v):
    b, s, d = x.shape
    depth = ada_w.shape[0]
    tm = _tile(s, 512)
    tq = _tile(s, 2 * LANES)
    ck = _tile(tm, 512)
    tp = _tile(s, 512)
    eb = 2048
    for l in range(depth):
        mod = _ada(c, ada_w[l], ada_b[l]).reshape(b, 1, 6 * d)
        sh1, sc1, g1, sh2, sc2, g2 = jnp.split(mod, 6, axis=-1)
        q, k, qi, kid, vt, wt, pool_out = _inproj(
            x, sh1, sc1, norm1_g[l], w_in[l], pool_w[l], pool_scale[l], pool_out_g[l],
            q_norm_g[l], k_norm_g[l], tm, ck)
        attn_out = _attn(q, k, qi, kid, vt, wt, attn_out_g[l], tq, ck)
        x1, h2t = _outproj(pool_out, attn_out, x, w_out[l], g1, sh2, sc2, norm2_g[l], tm)
        x = _peer(h2t, x1.reshape(b * s, d), g2, peer_wq[l], peer_subkeys[l],
                  peer_u[l], peer_v[l], s, tp, eb).reshape(b, s, d)
    return x
```
